```python
import math
import jax
import jax.numpy as jnp
from jax import lax
import numpy as np

D_MODEL = 2048
BATCH = 4
SEQ = 2048
DEPTH = 4
DEC_BATCH = 128
DEC_SEQ = 8
PAST_LEN = 16384
PAGE_SIZE = 128

LRU_WIDTH = D_MODEL // 2
LRU_BLOCKS = 16
LRU_BLOCK = LRU_WIDTH // LRU_BLOCKS
LRU_C = 8.0
CONV_WIDTH = 4
DN_HEAD_DIM = 128
DN_HEADS = (D_MODEL // 2) // DN_HEAD_DIM
DN_DK = DN_HEAD_DIM
DN_DV = DN_HEAD_DIM
DN_QK = DN_HEADS * DN_DK
DN_VW = DN_HEADS * DN_DV
DN_CHUNK = 64
POOL_WIDTH = D_MODEL // 2
POOL_WINDOWS = (2, 4, 8, 16)
POOL_GROUPS = 4
POOL_GROUP = POOL_WIDTH // POOL_GROUPS
POOL_BUF = 15
D_FF = 4 * D_MODEL
EPS = 1e-6
IN_SPLITS = (LRU_WIDTH, LRU_WIDTH, DN_QK, DN_QK, DN_VW, DN_VW, DN_HEADS, DN_HEADS, POOL_WIDTH, D_MODEL, D_MODEL, D_MODEL)
D_IN = sum(IN_SPLITS)

kernel_name = 'hybrid_rglru_deltanet_pool_step'


def rmsnorm(x, gain):
    xf = x.astype(jnp.float32)
    y = xf * lax.rsqrt(jnp.mean(xf * xf, axis=-1, keepdims=True) + EPS)
    return (y * gain.astype(jnp.float32)).astype(x.dtype)


def split_cols(p):
    outs = []
    start = 0
    for w in IN_SPLITS:
        outs.append(p[..., start:start + w])
        start += w
    return outs


def causal_dwconv(x, buf, w, b=None):
    full = jnp.concatenate([buf.astype(x.dtype), x], axis=1)
    T = x.shape[1]
    y = sum(full[:, j:j + T] * w[j].astype(x.dtype) for j in range(CONV_WIDTH))
    if b is not None:
        y = y + b.astype(x.dtype)
    return y, full[:, T:]


def _lin_combine(c1, c2):
    a1, b1 = c1
    a2, b2 = c2
    return a1 * a2, a2 * b1 + b2


def rg_lru(x, w_a, b_a, w_x, b_x, lam, h0):
    B, T, _ = x.shape
    f32 = jnp.float32
    xf = x.astype(f32)
    xb = xf.reshape(B, T, LRU_BLOCKS, LRU_BLOCK)
    r = jax.nn.sigmoid(jnp.einsum('btnc,ncd->btnd', xb, w_a.astype(f32)).reshape(B, T, LRU_WIDTH) + b_a.astype(f32))
    i = jax.nn.sigmoid(jnp.einsum('btnc,ncd->btnd', xb, w_x.astype(f32)).reshape(B, T, LRU_WIDTH) + b_x.astype(f32))
    log_a = -LRU_C * r * jax.nn.softplus(-lam.astype(f32))
    a = jnp.exp(log_a)
    bterm = jnp.sqrt(-jnp.expm1(2.0 * log_a)) * (i * xf)
    bterm = bterm.at[:, 0].add(a[:, 0] * h0.astype(f32))
    _, h = lax.associative_scan(_lin_combine, (a, bterm), axis=1)
    return h, h[:, -1]


def l2norm(x):
    return x * lax.rsqrt(jnp.sum(x * x, axis=-1, keepdims=True) + EPS)


def gated_delta_rule(q, k, v, beta, g, s0):
    B, T, H, DK = q.shape
    DV = v.shape[-1]
    C = DN_CHUNK if T >= DN_CHUNK else T
    n = -(-T // C)
    pad = n * C - T

    def prep(a):
        a = jnp.pad(a, [(0, 0), (0, pad)] + [(0, 0)] * (a.ndim - 2))
        a = a.reshape((B, n, C) + a.shape[2:])
        return jnp.moveaxis(a, (1, 3), (0, 2))

    q, k, v, beta, g = prep(q), prep(k), prep(v), prep(beta), prep(g)
    gamma = jnp.cumsum(g, axis=-1)
    incl = jnp.tril(jnp.ones((C, C), dtype=bool))
    strict = jnp.tril(jnp.ones((C, C), dtype=bool), k=-1)
    decay = jnp.exp(jnp.where(incl, gamma[..., :, None] - gamma[..., None, :], -jnp.inf))
    kk = jnp.einsum('nbhik,nbhjk->nbhij', k, k)
    a_mat = jnp.eye(C, dtype=jnp.float32) + jnp.where(strict, beta[..., None] * decay * kk, 0.0)
    eg = jnp.exp(gamma)
    rhs = jnp.concatenate([beta[..., None] * v, (beta * eg)[..., None] * k], axis=-1)
    sol = lax.linalg.triangular_solve(a_mat, rhs, left_side=True, lower=True, unit_diagonal=True)
    u_c, wk_c = sol[..., :DV], sol[..., DV:]
    qk = jnp.einsum('nbhik,nbhjk->nbhij', q, k) * decay
    qg = q * eg[..., None]
    kd = k * jnp.exp(gamma[..., -1:] - gamma)[..., None]
    g_last = jnp.exp(gamma[..., -1])

    def step(s, inp):
        qg_c, kd_c, uc, wkc, qk_c, gl_c = inp
        w = uc - jnp.einsum('bhck,bhkv->bhcv', wkc, s)
        o = jnp.einsum('bhck,bhkv->bhcv', qg_c, s) + jnp.einsum('bhij,bhjv->bhiv', qk_c, w)
        s = gl_c[..., None, None] * s + jnp.einsum('bhck,bhcv->bhkv', kd_c, w)
        return s, o

    s_last, o = lax.scan(step, s0, (qg, kd, u_c, wk_c, qk, g_last))
    o = jnp.moveaxis(o, (0, 2), (1, 3)).reshape(B, n * C, H, DV)[:, :T]
    return o, s_last


def multiscale_pool(u, buf, past_len, w_pool, scale):
    B, T, _ = u.shape
    f32 = jnp.float32
    P = POOL_BUF
    full = jnp.concatenate([buf.astype(f32), u.astype(f32)], axis=1)
    cs = jnp.concatenate([jnp.zeros((B, 1, POOL_WIDTH), f32), jnp.cumsum(full, axis=1)], axis=1)
    pos = jnp.arange(T) + 1 + past_len
    means = []
    for gi, w in enumerate(POOL_WINDOWS):
        c0 = gi * POOL_GROUP
        c1 = c0 + POOL_GROUP
        s = cs[:, P + 1:P + 1 + T, c0:c1] - cs[:, P + 1 - w:P + 1 - w + T, c0:c1]
        cnt = jnp.minimum(pos, w).astype(f32)
        means.append(s / cnt[None, :, None])
    pooled = jnp.concatenate(means, axis=-1)
    d = (pooled - full[:, P:]).reshape(B, T, POOL_GROUPS, POOL_GROUP)
    y = jnp.einsum('btgc,gcd->btgd', d, w_pool.astype(f32)).reshape(B, T, POOL_WIDTH) * scale.astype(f32)
    return y.astype(u.dtype), full[:, T:].astype(u.dtype)


def layer(x, past_len, h0, lru_buf, s0, dn_buf, pool_buf,
          norm_mix, w_in, lru_conv_w, lru_conv_b, lru_gate_a_w, lru_gate_a_b, lru_gate_x_w, lru_gate_x_b,
          lru_lambda, dn_conv_w, dn_a_log, dn_dt_bias, dn_norm_w, pool_w, pool_scale,
          w_br_lru, w_br_dn, w_br_pool, w_out, norm_mlp, w_up, w_down):
    B, T, _ = x.shape
    f32 = jnp.float32
    xn = rmsnorm(x, norm_mix)
    xl, gl, q, k, v, z, b_raw, a_raw, u_pool, g_lru, g_dn, g_pool = split_cols(xn @ w_in)

    xl, new_lru_buf = causal_dwconv(xl, lru_buf, lru_conv_w, lru_conv_b)
    h, h_last = rg_lru(xl, lru_gate_a_w, lru_gate_a_b, lru_gate_x_w, lru_gate_x_b, lru_lambda, h0)
    o_lru = (h * jax.nn.gelu(gl.astype(f32))).astype(x.dtype)

    qkv, new_dn_buf = causal_dwconv(jnp.concatenate([q, k, v], axis=-1), dn_buf, dn_conv_w)
    qkv = jax.nn.silu(qkv.astype(f32))
    qh = l2norm(qkv[..., :DN_QK].reshape(B, T, DN_HEADS, DN_DK)) * (DN_DK ** -0.5)
    kh = l2norm(qkv[..., DN_QK:2 * DN_QK].reshape(B, T, DN_HEADS, DN_DK))
    vh = qkv[..., 2 * DN_QK:].reshape(B, T, DN_HEADS, DN_DV)
    beta = jax.nn.sigmoid(b_raw.astype(f32))
    g = -jnp.exp(dn_a_log.astype(f32)) * jax.nn.softplus(a_raw.astype(f32) + dn_dt_bias.astype(f32))
    o, s_last = gated_delta_rule(qh, kh, vh, beta, g, s0.astype(f32))
    o = o * lax.rsqrt(jnp.mean(o * o, axis=-1, keepdims=True) + EPS) * dn_norm_w.astype(f32)
    o = o * jax.nn.silu(z.astype(f32).reshape(B, T, DN_HEADS, DN_DV))
    o_dn = o.reshape(B, T, DN_VW).astype(x.dtype)

    o_pool, new_pool_buf = multiscale_pool(u_pool, pool_buf, past_len, pool_w, pool_scale)

    merged = (jax.nn.sigmoid(g_lru) * (o_lru @ w_br_lru)
              + jax.nn.sigmoid(g_dn) * (o_dn @ w_br_dn)
              + jax.nn.sigmoid(g_pool) * (o_pool @ w_br_pool))
    x = x + merged @ w_out

    hm = jnp.square(jax.nn.relu(rmsnorm(x, norm_mlp) @ w_up))
    x = x + hm @ w_down
    return x, (h_last, new_lru_buf, s_last, new_dn_buf, new_pool_buf)


def setup_inputs(seed: int = 0) -> dict:
    key = jax.random.key(seed)
    ks = iter(jax.random.split(key, 40))
    f32 = jnp.float32
    L = DEPTH
    dn_conv_ch = 2 * DN_QK + DN_VW

    def nrm(shape, scale):
        return jax.random.normal(next(ks), shape, f32) * scale

    x_prompt = nrm((BATCH, SEQ, D_MODEL), 1.0)
    x_sample = nrm((DEC_BATCH, DEC_SEQ, D_MODEL), 1.0)
    state_lru_h = nrm((L, DEC_BATCH, LRU_WIDTH), 0.5)
    state_lru_conv = nrm((L, DEC_BATCH, CONV_WIDTH - 1, LRU_WIDTH), 1.0)
    state_dn_s = nrm((L, DEC_BATCH, DN_HEADS, DN_DK, DN_DV), 0.05)
    state_dn_conv = nrm((L, DEC_BATCH, CONV_WIDTH - 1, dn_conv_ch), 1.0)
    state_pool = nrm((L, DEC_BATCH, POOL_BUF, POOL_WIDTH), 1.0)
    norm_mix = 1.0 + nrm((L, D_MODEL), 0.02)
    w_in = nrm((L, D_MODEL, D_IN), D_MODEL ** -0.5)
    lru_conv_w = nrm((L, CONV_WIDTH, LRU_WIDTH), CONV_WIDTH ** -0.5)
    lru_conv_b = nrm((L, LRU_WIDTH), 0.02)
    lru_gate_a_w = nrm((L, LRU_BLOCKS, LRU_BLOCK, LRU_BLOCK), LRU_BLOCK ** -0.5)
    lru_gate_a_b = nrm((L, LRU_WIDTH), 0.02)
    lru_gate_x_w = nrm((L, LRU_BLOCKS, LRU_BLOCK, LRU_BLOCK), LRU_BLOCK ** -0.5)
    lru_gate_x_b = nrm((L, LRU_WIDTH), 0.02)
    a0 = jax.random.uniform(next(ks), (L, LRU_WIDTH), f32, 0.9, 0.999)
    lru_lambda = jnp.log(a0) - jnp.log1p(-a0)
    dn_conv_w = nrm((L, CONV_WIDTH, dn_conv_ch), CONV_WIDTH ** -0.5)
    dn_a_log = jnp.log(jax.random.uniform(next(ks), (L, DN_HEADS), f32, 1.0, 16.0))
    dt = jnp.exp(jax.random.uniform(next(ks), (L, DN_HEADS), f32, math.log(1e-3), math.log(1e-1)))
    dn_dt_bias = dt + jnp.log(-jnp.expm1(-dt))
    dn_norm_w = 1.0 + nrm((L, DN_DV), 0.02)
    pool_w = nrm((L, POOL_GROUPS, POOL_GROUP, POOL_GROUP), POOL_GROUP ** -0.5)
    pool_scale = 1.0 + nrm((L, POOL_WIDTH), 0.02)
    w_br_lru = nrm((L, LRU_WIDTH, D_MODEL), LRU_WIDTH ** -0.5)
    w_br_dn = nrm((L, DN_VW, D_MODEL), DN_VW ** -0.5)
    w_br_pool = nrm((L, POOL_WIDTH, D_MODEL), POOL_WIDTH ** -0.5)
    w_out = nrm((L, D_MODEL, D_MODEL), D_MODEL ** -0.5)
    norm_mlp = 1.0 + nrm((L, D_MODEL), 0.02)
    w_up = nrm((L, D_MODEL, D_FF), D_MODEL ** -0.5)
    w_down = nrm((L, D_FF, D_MODEL), D_FF ** -0.5)
    norm_final = 1.0 + nrm((D_MODEL,), 0.02)
    return {'x_prompt': x_prompt, 'x_sample': x_sample,
            'state_lru_h': state_lru_h, 'state_lru_conv': state_lru_conv, 'state_dn_s': state_dn_s,
            'state_dn_conv': state_dn_conv, 'state_pool': state_pool,
            'norm_mix': norm_mix, 'w_in': w_in, 'lru_conv_w': lru_conv_w, 'lru_conv_b': lru_conv_b,
            'lru_gate_a_w': lru_gate_a_w, 'lru_gate_a_b': lru_gate_a_b, 'lru_gate_x_w': lru_gate_x_w,
            'lru_gate_x_b': lru_gate_x_b, 'lru_lambda': lru_lambda, 'dn_conv_w': dn_conv_w,
            'dn_a_log': dn_a_log, 'dn_dt_bias': dn_dt_bias, 'dn_norm_w': dn_norm_w, 'pool_w': pool_w,
            'pool_scale': pool_scale, 'w_br_lru': w_br_lru, 'w_br_dn': w_br_dn, 'w_br_pool': w_br_pool,
            'w_out': w_out, 'norm_mlp': norm_mlp, 'w_up': w_up, 'w_down': w_down, 'norm_final': norm_final}


def reference(x_prompt, x_sample, state_lru_h, state_lru_conv, state_dn_s, state_dn_conv, state_pool,
              norm_mix, w_in, lru_conv_w, lru_conv_b, lru_gate_a_w, lru_gate_a_b, lru_gate_x_w, lru_gate_x_b,
              lru_lambda, dn_conv_w, dn_a_log, dn_dt_bias, dn_norm_w, pool_w, pool_scale,
              w_br_lru, w_br_dn, w_br_pool, w_out, norm_mlp, w_up, w_down, norm_final):
    f32 = jnp.float32
    bp = x_prompt.shape[0]
    dt = x_prompt.dtype
    p_h0 = jnp.zeros((bp, LRU_WIDTH), f32)
    p_lru_buf = jnp.zeros((bp, CONV_WIDTH - 1, LRU_WIDTH), dt)
    p_s0 = jnp.zeros((bp, DN_HEADS, DN_DK, DN_DV), f32)
    p_dn_buf = jnp.zeros((bp, CONV_WIDTH - 1, 2 * DN_QK + DN_VW), dt)
    p_pool_buf = jnp.zeros((bp, POOL_BUF, POOL_WIDTH), dt)
    xp, xs = x_prompt, x_sample
    p_st, s_st = [], []
    for l in range(DEPTH):
        params = (norm_mix[l], w_in[l], lru_conv_w[l], lru_conv_b[l], lru_gate_a_w[l], lru_gate_a_b[l],
                  lru_gate_x_w[l], lru_gate_x_b[l], lru_lambda[l], dn_conv_w[l], dn_a_log[l], dn_dt_bias[l],
                  dn_norm_w[l], pool_w[l], pool_scale[l], w_br_lru[l], w_br_dn[l], w_br_pool[l], w_out[l],
                  norm_mlp[l], w_up[l], w_down[l])
        xp, st = layer(xp, 0, p_h0, p_lru_buf, p_s0, p_dn_buf, p_pool_buf, *params)
        p_st.append(st)
        xs, st = layer(xs, PAST_LEN, state_lru_h[l], state_lru_conv[l], state_dn_s[l], state_dn_conv[l],
                       state_pool[l], *params)
        s_st.append(st)
    y_prompt = rmsnorm(xp, norm_final)
    y_sample = rmsnorm(xs, norm_final)
    p_lru_h = jnp.stack([s[0] for s in p_st])
    p_lru_conv = jnp.stack([s[1] for s in p_st])
    p_dn_s = jnp.stack([s[2] for s in p_st])
    p_dn_conv = jnp.stack([s[3] for s in p_st])
    p_pool = jnp.stack([s[4] for s in p_st])
    s_lru_h = jnp.stack([s[0] for s in s_st])
    s_lru_conv = jnp.stack([s[1] for s in s_st])
    s_dn_s = jnp.stack([s[2] for s in s_st])
    s_dn_conv = jnp.stack([s[3] for s in s_st])
    s_pool = jnp.stack([s[4] for s in s_st])
    return (y_prompt, y_sample, p_lru_h, p_lru_conv, p_dn_s, p_dn_conv, p_pool,
            s_lru_h, s_lru_conv, s_dn_s, s_dn_conv, s_pool)
```

```python
import functools

import jax
import jax.numpy as jnp
from jax import lax
from jax.experimental import pallas as pl
from jax.experimental.pallas import tpu as pltpu

F32 = jnp.float32
BF16 = jnp.bfloat16

EPS = 1e-6
LRU_C = 8.0
CONV_WIDTH = 4
HEAD_DIM = 128
LRU_BLOCK = 64
POOL_WINDOWS = (2, 4, 8, 16)
POOL_BUF = 15
DN_CHUNK = 64
PAST_LEN = 16384
SUBLANES = 8
GATE_TILE = 256
VMEM_LIMIT = 56 * 1024 * 1024

_NT = (((1,), (1,)), ((), ()))
_TN = (((0,), (0,)), ((), ()))
_HI = lax.Precision.HIGHEST


def _params(*sem):
    return pltpu.CompilerParams(dimension_semantics=sem, vmem_limit_bytes=VMEM_LIMIT)


def _full(shape):
    return pl.BlockSpec(shape, lambda *_: (0,) * len(shape))


def _layer(l, shape):
    return pl.BlockSpec((None,) + shape, lambda *_: (l,) + (0,) * len(shape))


def _rms_matmul_body(*refs, relu2, small):
    if small:
        x_ref, g_ref, w_ref, ws_ref, o_ref, os_ref, xn_ref = refs
    else:
        x_ref, g_ref, w_ref, o_ref, xn_ref = refs

    @pl.when(pl.program_id(1) == 0)
    def _():
        x = x_ref[...]
        inv = lax.rsqrt(jnp.mean(x * x, axis=-1, keepdims=True) + EPS)
        xn = (x * inv * g_ref[...]).astype(BF16)
        xn_ref[...] = xn
        if small:
            os_ref[...] = jnp.dot(xn, ws_ref[...], preferred_element_type=F32)

    acc = jnp.dot(xn_ref[...], w_ref[...], preferred_element_type=F32)
    if relu2:
        acc = jnp.square(jnp.maximum(acc, 0.0))
    o_ref[...] = acc.astype(o_ref.dtype)


def _rms_matmul(x, gain, w, l, *, tm, tn, out_dtype, relu2=False, w_small=None):
    m, d = x.shape
    n = w.shape[-1]
    small = w_small is not None
    in_specs = [pl.BlockSpec((tm, d), lambda i, j: (i, 0)),
                _layer(l, (1, d)),
                pl.BlockSpec((None, d, tn), lambda i, j: (l, 0, j))]
    out_specs = [pl.BlockSpec((tm, tn), lambda i, j: (i, j))]
    out_shape = [jax.ShapeDtypeStruct((m, n), out_dtype)]
    args = [x, gain, w]
    if small:
        ns = w_small.shape[-1]
        in_specs.append(_layer(l, (d, ns)))
        out_specs.append(pl.BlockSpec((tm, ns), lambda i, j: (i, 0)))
        out_shape.append(jax.ShapeDtypeStruct((m, ns), F32))
        args.append(w_small)
    res = pl.pallas_call(
        functools.partial(_rms_matmul_body, relu2=relu2, small=small),
        grid=(m // tm, n // tn),
        in_specs=in_specs, out_specs=out_specs, out_shape=out_shape,
        scratch_shapes=[pltpu.VMEM((tm, d), BF16)],
        compiler_params=_params("parallel", "arbitrary"),
        name="rms_matmul_small" if small else "rms_matmul",
    )(*args)
    return res if small else res[0]


def _matmul_res_body(a_ref, w_ref, r_ref, o_ref):
    @pl.when(pl.program_id(2) == 0)
    def _():
        o_ref[...] = r_ref[...]

    o_ref[...] += jnp.dot(a_ref[...], w_ref[...], preferred_element_type=F32)


def _matmul_res(a, w, l, res, *, tm, tn, tk):
    m, k = a.shape
    n = w.shape[-1]
    return pl.pallas_call(
        _matmul_res_body,
        grid=(m // tm, n // tn, k // tk),
        in_specs=[pl.BlockSpec((tm, tk), lambda i, j, kk: (i, kk)),
                  pl.BlockSpec((None, tk, tn), lambda i, j, kk: (l, kk, j)),
                  pl.BlockSpec((tm, tn), lambda i, j, kk: (i, j))],
        out_specs=pl.BlockSpec((tm, tn), lambda i, j, kk: (i, j)),
        out_shape=jax.ShapeDtypeStruct((m, n), F32),
        compiler_params=_params("parallel", "parallel", "arbitrary"),
        name="matmul_res",
    )(a, w, res)


def _merge_body(ol_ref, od_ref, op_ref, gl_ref, gd_ref, gp_ref, wl_ref, wd_ref, wp_ref, o_ref):
    def branch(o_ref_, g_ref_, w_ref_):
        return jax.nn.sigmoid(g_ref_[...]) * jnp.dot(o_ref_[...], w_ref_[...], preferred_element_type=F32)

    m = branch(ol_ref, gl_ref, wl_ref) + branch(od_ref, gd_ref, wd_ref) + branch(op_ref, gp_ref, wp_ref)
    o_ref[...] = m.astype(o_ref.dtype)


def _merge(o_lru, o_dn, o_pool, proj, gate_col0, w_lru, w_dn, w_pool, l, *, tm, tn):
    m, half = o_lru.shape
    d = w_lru.shape[-1]
    nj = d // tn
    g0 = gate_col0 // tn

    def o_spec():
        return pl.BlockSpec((tm, half), lambda i, j: (i, 0))

    def g_spec(b):
        return pl.BlockSpec((tm, tn), lambda i, j: (i, g0 + b * nj + j))

    def w_spec():
        return pl.BlockSpec((None, half, tn), lambda i, j: (l, 0, j))

    return pl.pallas_call(
        _merge_body,
        grid=(m // tm, nj),
        in_specs=[o_spec(), o_spec(), o_spec(), g_spec(0), g_spec(1), g_spec(2), w_spec(), w_spec(), w_spec()],
        out_specs=pl.BlockSpec((tm, tn), lambda i, j: (i, j)),
        out_shape=jax.ShapeDtypeStruct((m, d), BF16),
        compiler_params=_params("parallel", "arbitrary"),
        name="merge",
    )(o_lru, o_dn, o_pool, proj, proj, proj, w_lru, w_dn, w_pool)


def _rmsnorm_body(x_ref, g_ref, o_ref):
    x = x_ref[...]
    inv = lax.rsqrt(jnp.mean(x * x, axis=-1, keepdims=True) + EPS)
    o_ref[...] = x * inv * g_ref[...]


def _final_norm(x, gain, row0, rows, *, tm):
    d = x.shape[-1]
    b0 = row0 // tm
    return pl.pallas_call(
        _rmsnorm_body,
        grid=(rows // tm,),
        in_specs=[pl.BlockSpec((tm, d), lambda i: (b0 + i, 0)), _full((1, d))],
        out_specs=pl.BlockSpec((tm, d), lambda i: (i, 0)),
        out_shape=jax.ShapeDtypeStruct((rows, d), F32),
        compiler_params=_params("parallel"),
        name="final_norm",
    )(x, gain)


def _softplus(x):
    return jnp.maximum(x, 0.0) + jnp.log1p(jnp.exp(-jnp.abs(x)))


def _shift_rows(cur, prev, k):
    ax = cur.ndim - 2
    row = lax.broadcasted_iota(jnp.int32, cur.shape, ax)
    return jnp.where(row >= k, pltpu.roll(cur, k, ax), pltpu.roll(prev, k, ax))


def _prev_groups(x, first):
    return jnp.concatenate([first[None], x[:-1]], axis=0)


def _causal_conv(x, prev, cw):
    y = cw[CONV_WIDTH - 1:CONV_WIDTH] * x
    for k in range(1, CONV_WIDTH):
        y = y + cw[CONV_WIDTH - 1 - k:CONV_WIDTH - k] * _shift_rows(x, prev, k)
    return y


def _lru_body(*refs, long_seq):
    (x_ref, gate_ref, prev_ref, h0_ref, cw_ref, cb_ref, wa_ref, ba_ref, wx_ref, bx_ref, lam_ref) = refs[:11]
    refs = refs[11:]
    if not long_seq:
        refs = refs[1:]
    o_ref, hl_ref, nb_ref, a_scr, b_scr, h_scr, cx_scr, ch_scr = refs
    g_n, _, c = x_ref.shape
    rows = g_n * SUBLANES
    x = x_ref[...]
    if long_seq:
        @pl.when(pl.program_id(1) == 0)
        def _():
            cx_scr[...] = prev_ref[0]
            ch_scr[...] = h0_ref[0]

        prev = _prev_groups(x, cx_scr[...])
    else:
        prev = prev_ref[...]
    xc = _causal_conv(x, prev, cw_ref[...]) + cb_ref[...]
    if long_seq:
        cx_scr[...] = x[g_n - 1]
        nb_ref[0] = x[g_n - 1]
    else:
        nb_ref[...] = x

    xc2 = xc.reshape(rows, c)
    xb = xc2.astype(BF16)
    nblk = c // GATE_TILE

    def gate(w_ref, b_ref):
        parts = [jnp.dot(xb[:, i * GATE_TILE:(i + 1) * GATE_TILE], w_ref[i], preferred_element_type=F32)
                 for i in range(nblk)]
        return jax.nn.sigmoid(jnp.concatenate(parts, axis=1) + b_ref[...])

    r = gate(wa_ref, ba_ref)
    i_g = gate(wx_ref, bx_ref)
    log_a = (-LRU_C) * r * _softplus(-lam_ref[...])
    a = jnp.exp(log_a)
    one_minus_a2 = -jnp.tanh(log_a) * (a * a + 1.0)
    b = jnp.sqrt(one_minus_a2) * (i_g * xc2)

    a3 = a.reshape(g_n, SUBLANES, c)
    b3 = b.reshape(g_n, SUBLANES, c)
    row = lax.broadcasted_iota(jnp.int32, a3.shape, 1)
    s = 1
    while s < SUBLANES:
        m = row >= s
        a_sh = pltpu.roll(a3, s, 1)
        b_sh = pltpu.roll(b3, s, 1)
        b3 = jnp.where(m, a3 * b_sh + b3, b3)
        a3 = jnp.where(m, a3 * a_sh, a3)
        s *= 2
    a_scr[...] = a3
    b_scr[...] = b3

    def chain(g, h_in):
        if not long_seq:
            h_in = h0_ref[g]
        hg = a_scr[g] * h_in + b_scr[g]
        h_scr[g] = hg
        h_out = hg[SUBLANES - 1:SUBLANES]
        if not long_seq:
            hl_ref[g] = h_out
        return h_out

    h_init = ch_scr[...] if long_seq else jnp.zeros((1, c), F32)
    h_fin = lax.fori_loop(0, g_n, chain, h_init)
    if long_seq:
        ch_scr[...] = h_fin
        hl_ref[0] = h_fin
    h = h_scr[...].reshape(rows, c)
    gl = gate_ref[...].reshape(rows, c)
    o_ref[...] = (h * jax.nn.gelu(gl)).astype(o_ref.dtype)


def _lru_call(proj3, o_prev, prev, h0, weights, l, *, long_seq, n_outer, n_inner, g_n, group0, state_layer):
    cw, cb, wa, ba, wx, bx, lam = weights
    c = cw.shape[-1]
    m = proj3.shape[0] * SUBLANES
    nseq_blk = 1 if long_seq else g_n
    nseq = n_outer * nseq_blk
    gb0 = group0 // g_n

    def x_spec(col):
        return pl.BlockSpec((g_n, SUBLANES, c), lambda i, t: (gb0 + i * n_inner + t, 0, col))

    def state_spec(r):
        if state_layer is None:
            return pl.BlockSpec((nseq_blk, r, c), lambda i, t: (i, 0, 0))
        return pl.BlockSpec((None, nseq_blk, r, c), lambda i, t: (state_layer, i, 0, 0))

    ng = c // GATE_TILE
    in_specs = [x_spec(0), x_spec(1), state_spec(SUBLANES), state_spec(1),
                _layer(l, (CONV_WIDTH, c)), _layer(l, (1, c)),
                _layer(l, (ng, GATE_TILE, GATE_TILE)), _layer(l, (1, c)),
                _layer(l, (ng, GATE_TILE, GATE_TILE)), _layer(l, (1, c)), _layer(l, (1, c))]
    args = [proj3, proj3, prev, h0, cw, cb, wa, ba, wx, bx, lam]
    aliases = {}
    if o_prev is not None:
        in_specs.append(pl.BlockSpec(memory_space=pl.ANY))
        args.append(o_prev)
        aliases = {len(args) - 1: 0}
    out_specs = [pl.BlockSpec((g_n * SUBLANES, c), lambda i, t: (gb0 + i * n_inner + t, 0)),
                 pl.BlockSpec((nseq_blk, 1, c), lambda i, t: (i, 0, 0)),
                 pl.BlockSpec((nseq_blk, SUBLANES, c), lambda i, t: (i, 0, 0))]
    out_shape = [jax.ShapeDtypeStruct((m, c), BF16),
                 jax.ShapeDtypeStruct((nseq, 1, c), F32),
                 jax.ShapeDtypeStruct((nseq, SUBLANES, c), F32)]
    grp = (g_n, SUBLANES, c)
    return pl.pallas_call(
        functools.partial(_lru_body, long_seq=long_seq),
        grid=(n_outer, n_inner),
        in_specs=in_specs, out_specs=out_specs, out_shape=out_shape,
        scratch_shapes=[pltpu.VMEM(grp, F32), pltpu.VMEM(grp, F32), pltpu.VMEM(grp, F32),
                        pltpu.VMEM((SUBLANES, c), F32), pltpu.VMEM((1, c), F32)],
        input_output_aliases=aliases,
        compiler_params=_params("parallel", "arbitrary"),
        name="lru_long" if long_seq else "lru_short",
    )(*args)


def _pool_body(*refs, long_seq, past_len):
    u_ref, hist_ref, pw_ref, ps_ref = refs[:4]
    refs = refs[4:]
    if not long_seq:
        refs = refs[1:]
    o_ref, nh_ref, c_scr = refs
    g_n, _, c = u_ref.shape
    rows = g_n * SUBLANES
    x = u_ref[...]
    t = pl.program_id(1)
    if long_seq:
        @pl.when(t == 0)
        def _():
            c_scr[0] = hist_ref[0, 0:SUBLANES]
            c_scr[1] = hist_ref[0, SUBLANES:2 * SUBLANES]

        ext = jnp.concatenate([c_scr[...], x], axis=0)
        gs = 1
    else:
        hist = hist_ref[...]
        ext = jnp.concatenate([hist[:, 0:SUBLANES], hist[:, SUBLANES:2 * SUBLANES], x], axis=0)
        gs = g_n
    nh_ref[:, 0:SUBLANES] = ext[-2 * gs:-gs]
    nh_ref[:, SUBLANES:2 * SUBLANES] = ext[-gs:]
    if long_seq:
        c_scr[0] = ext[g_n]
        c_scr[1] = ext[g_n + 1]

    def prev(a):
        return jnp.concatenate([a[:gs], a[:-gs]], axis=0)

    n_grp = len(POOL_WINDOWS)
    pg = c // n_grp
    i0 = lax.broadcasted_iota(jnp.int32, (g_n, SUBLANES, pg), 0)
    i1 = lax.broadcasted_iota(jnp.int32, (g_n, SUBLANES, pg), 1)
    t_abs = (t * g_n + i0) * SUBLANES + i1 if long_seq else i1
    outs = []
    for gi, w in enumerate(POOL_WINDOWS):
        eg = ext[:, :, gi * pg:(gi + 1) * pg]
        s = eg
        k = 1
        while k < w:
            s = s + (_shift_rows(s, prev(s), k) if k < SUBLANES else prev(s))
            k *= 2
        cnt = jnp.minimum(t_abs + (1 + past_len), w).astype(F32)
        d = s[2 * gs:] / cnt - eg[2 * gs:]
        d2 = d.reshape(rows, pg).astype(BF16)
        outs.append(jnp.dot(d2, pw_ref[gi], preferred_element_type=F32))
    y = jnp.concatenate(outs, axis=1) * ps_ref[...]
    o_ref[...] = y.astype(o_ref.dtype)


def _pool_call(proj3, col, o_prev, hist, pool_w, pool_scale, l, *, long_seq, past_len, n_outer, n_inner, g_n,
               group0, state_layer):
    c = pool_scale.shape[-1]
    m = proj3.shape[0] * SUBLANES
    nseq_blk = 1 if long_seq else g_n
    nseq = n_outer * nseq_blk
    gb0 = group0 // g_n
    pg = c // len(POOL_WINDOWS)
    hrows = 2 * SUBLANES
    if state_layer is None:
        h_spec = pl.BlockSpec((nseq_blk, hrows, c), lambda i, t: (i, 0, 0))
    else:
        h_spec = pl.BlockSpec((None, nseq_blk, hrows, c), lambda i, t: (state_layer, i, 0, 0))
    in_specs = [pl.BlockSpec((g_n, SUBLANES, c), lambda i, t: (gb0 + i * n_inner + t, 0, col)), h_spec,
                _layer(l, (len(POOL_WINDOWS), pg, pg)), _layer(l, (1, c))]
    args = [proj3, hist, pool_w, pool_scale]
    aliases = {}
    if o_prev is not None:
        in_specs.append(pl.BlockSpec(memory_space=pl.ANY))
        args.append(o_prev)
        aliases = {len(args) - 1: 0}
    return pl.pallas_call(
        functools.partial(_pool_body, long_seq=long_seq, past_len=past_len),
        grid=(n_outer, n_inner),
        in_specs=in_specs,
        out_specs=[pl.BlockSpec((g_n * SUBLANES, c), lambda i, t: (gb0 + i * n_inner + t, 0)),
                   pl.BlockSpec((nseq_blk, hrows, c), lambda i, t: (i, 0, 0))],
        out_shape=[jax.ShapeDtypeStruct((m, c), BF16), jax.ShapeDtypeStruct((nseq, hrows, c), F32)],
        scratch_shapes=[pltpu.VMEM((2, SUBLANES, c), F32)],
        input_output_aliases=aliases,
        compiler_params=_params("parallel", "arbitrary"),
        name="pool_long" if long_seq else "pool_short",
    )(*args)


def _delta_body(*refs, long_seq, tseq):
    (q_ref, k_ref, v_ref, z_ref, ba_ref, prev_ref, s0_ref, cw_ref, alog_ref, dtb_ref, nw_ref) = refs[:11]
    refs = refs[11:]
    if not long_seq:
        refs = refs[1:]
    o_ref, s_ref, nb_ref, cx_scr = refs
    g_n, _, c = q_ref.shape
    rows = g_n * SUBLANES
    nseq_blk = rows // tseq
    heads = c // HEAD_DIM
    first = pl.program_id(1) == 0

    if long_seq:
        @pl.when(first)
        def _():
            cx_scr[...] = prev_ref[0]
            s_ref[...] = s0_ref[...]

    cw = cw_ref[...]
    streams = []
    for idx, x_ref in enumerate((q_ref, k_ref, v_ref)):
        x = x_ref[...]
        lo, hi = idx * c, (idx + 1) * c
        if long_seq:
            prev = _prev_groups(x, cx_scr[:, lo:hi])
            cx_scr[:, lo:hi] = x[g_n - 1]
            nb_ref[0, :, lo:hi] = x[g_n - 1]
        else:
            prev = prev_ref[:, :, lo:hi]
            nb_ref[:, :, lo:hi] = x
        y = _causal_conv(x, prev, cw[:, lo:hi])
        streams.append((y * jax.nn.sigmoid(y)).reshape(rows, c))
    q_all, k_all, v_all = streams
    z_all = z_ref[...].reshape(rows, c)

    ba = ba_ref[...]
    beta_all = jax.nn.sigmoid(ba[:, 0:HEAD_DIM])
    g_all = -jnp.exp(alog_ref[...]) * _softplus(ba[:, HEAD_DIM:2 * HEAD_DIM] + dtb_ref[...])

    ri = lax.broadcasted_iota(jnp.int32, (rows, rows), 0)
    ci = lax.broadcasted_iota(jnp.int32, (rows, rows), 1)
    same = (ri // tseq) == (ci // tseq)
    incl = same & (ci <= ri)
    strict = same & (ci < ri)
    gamma_all = jnp.dot(incl.astype(F32), g_all, precision=_HI, preferred_element_type=F32)
    gtot_all = jnp.dot(same.astype(F32), g_all, precision=_HI, preferred_element_type=F32)
    gamma_t = gamma_all.T
    eye = (ri == ci).astype(F32)

    s_src = s_ref if long_seq else s0_ref
    for h in range(heads):
        hs = slice(h * HEAD_DIM, (h + 1) * HEAD_DIM)
        q_h, k_h, v_h = q_all[:, hs], k_all[:, hs], v_all[:, hs]
        q_h = q_h * lax.rsqrt(jnp.sum(q_h * q_h, axis=-1, keepdims=True) + EPS) * (HEAD_DIM ** -0.5)
        k_h = k_h * lax.rsqrt(jnp.sum(k_h * k_h, axis=-1, keepdims=True) + EPS)
        beta = beta_all[:, h:h + 1]
        gcol = gamma_all[:, h:h + 1]
        gtot = gtot_all[:, h:h + 1]
        decay = jnp.where(incl, jnp.exp(gcol - gamma_t[h:h + 1, :]), 0.0)
        qb = q_h.astype(BF16)
        kb = k_h.astype(BF16)
        kk = lax.dot_general(kb, kb, _NT, preferred_element_type=F32)
        a_mat = jnp.where(strict, beta * decay * kk, 0.0)
        t_inv = eye - jnp.where((ri // 2 == ci // 2), a_mat, 0.0)
        s = 2
        while s < tseq:
            off = jnp.where((ri // (2 * s) == ci // (2 * s)) & (ri % (2 * s) >= s) & (ci % (2 * s) < s), a_mat, 0.0)
            t_off = jnp.dot(t_inv, off, precision=_HI, preferred_element_type=F32)
            t_inv = t_inv - jnp.dot(t_off, t_inv, precision=_HI, preferred_element_type=F32)
            s *= 2
        e_g = jnp.exp(gcol)
        rhs = jnp.concatenate([beta * v_h, (beta * e_g) * k_h], axis=1)
        sol = jnp.dot(t_inv, rhs, precision=_HI, preferred_element_type=F32)
        u_c, wk_c = sol[:, :HEAD_DIM], sol[:, HEAD_DIM:]
        qk = lax.dot_general(qb, kb, _NT, preferred_element_type=F32) * decay
        qg = (q_h * e_g).astype(BF16)
        kd = (k_h * jnp.exp(gtot - gcol)).astype(BF16)
        wkb = wk_c.astype(BF16)

        w_parts, o_parts, s_old = [], [], []
        for sq in range(nseq_blk):
            rs = slice(sq * tseq, (sq + 1) * tseq)
            s_mat = s_src[sq, h]
            sb = s_mat.astype(BF16)
            s_old.append(s_mat)
            w_parts.append(u_c[rs] - jnp.dot(wkb[rs], sb, preferred_element_type=F32))
            o_parts.append(jnp.dot(qg[rs], sb, preferred_element_type=F32))
        w = jnp.concatenate(w_parts, axis=0) if nseq_blk > 1 else w_parts[0]
        o = jnp.concatenate(o_parts, axis=0) if nseq_blk > 1 else o_parts[0]
        wb = w.astype(BF16)
        o = o + jnp.dot(qk.astype(BF16), wb, preferred_element_type=F32)
        for sq in range(nseq_blk):
            rs = slice(sq * tseq, (sq + 1) * tseq)
            g_last = jnp.exp(gtot[sq * tseq:sq * tseq + 1])
            s_ref[sq, h] = g_last * s_old[sq] + lax.dot_general(kd[rs], wb[rs], _TN, preferred_element_type=F32)

        o = o * lax.rsqrt(jnp.mean(o * o, axis=-1, keepdims=True) + EPS) * nw_ref[...]
        z_h = z_all[:, hs]
        o_ref[:, hs] = (o * (z_h * jax.nn.sigmoid(z_h))).astype(o_ref.dtype)


def _delta_call(proj3, ba, o_prev, prev, s0, conv_w, alog, dtb, norm_w, l, *, long_seq, tseq, n_outer, n_inner,
                group0, state_layer):
    c = conv_w.shape[-1] // 3
    heads = c // HEAD_DIM
    g_n = DN_CHUNK // SUBLANES
    m = proj3.shape[0] * SUBLANES
    nseq_blk = 1 if long_seq else DN_CHUNK // tseq
    nseq = n_outer * nseq_blk
    gb0 = group0 // g_n

    def x_spec(col):
        return pl.BlockSpec((g_n, SUBLANES, c), lambda i, t: (gb0 + i * n_inner + t, 0, col))

    if state_layer is None:
        p_spec = pl.BlockSpec((nseq_blk, SUBLANES, 3 * c), lambda i, t: (i, 0, 0))
        s_spec = pl.BlockSpec((nseq_blk, heads, HEAD_DIM, HEAD_DIM), lambda i, t: (i, 0, 0, 0))
    else:
        p_spec = pl.BlockSpec((None, nseq_blk, SUBLANES, 3 * c), lambda i, t: (state_layer, i, 0, 0))
        s_spec = pl.BlockSpec((None, nseq_blk, heads, HEAD_DIM, HEAD_DIM), lambda i, t: (state_layer, i, 0, 0, 0))
    in_specs = [x_spec(2), x_spec(3), x_spec(4), x_spec(5),
                pl.BlockSpec((DN_CHUNK, 2 * HEAD_DIM), lambda i, t: (gb0 + i * n_inner + t, 0)),
                p_spec, s_spec,
                _layer(l, (CONV_WIDTH, 3 * c)), _layer(l, (1, HEAD_DIM)), _layer(l, (1, HEAD_DIM)),
                _layer(l, (1, HEAD_DIM))]
    args = [proj3, proj3, proj3, proj3, ba, prev, s0, conv_w, alog, dtb, norm_w]
    aliases = {}
    if o_prev is not None:
        in_specs.append(pl.BlockSpec(memory_space=pl.ANY))
        args.append(o_prev)
        aliases = {len(args) - 1: 0}
    return pl.pallas_call(
        functools.partial(_delta_body, long_seq=long_seq, tseq=tseq),
        grid=(n_outer, n_inner),
        in_specs=in_specs,
        out_specs=[pl.BlockSpec((DN_CHUNK, c), lambda i, t: (gb0 + i * n_inner + t, 0)),
                   pl.BlockSpec((nseq_blk, heads, HEAD_DIM, HEAD_DIM), lambda i, t: (i, 0, 0, 0)),
                   pl.BlockSpec((nseq_blk, SUBLANES, 3 * c), lambda i, t: (i, 0, 0))],
        out_shape=[jax.ShapeDtypeStruct((m, c), BF16),
                   jax.ShapeDtypeStruct((nseq, heads, HEAD_DIM, HEAD_DIM), F32),
                   jax.ShapeDtypeStruct((nseq, SUBLANES, 3 * c), F32)],
        scratch_shapes=[pltpu.VMEM((SUBLANES, 3 * c), F32)],
        input_output_aliases=aliases,
        compiler_params=_params("parallel", "arbitrary"),
        name="delta_long" if long_seq else "delta_short",
    )(*args)


def _pad_front(a, rows):
    pad = [(0, 0)] * a.ndim
    pad[-2] = (rows - a.shape[-2], 0)
    return jnp.pad(a, pad)


def _block_diag_tiles(w):
    depth, nb, b, _ = w.shape
    per = GATE_TILE // b
    w5 = w.reshape(depth, nb // per, per, b, b)
    eye = jnp.eye(per, dtype=w.dtype)
    t = jnp.einsum('lijcd,jk->lijckd', w5, eye)
    return t.reshape(depth, nb // per, GATE_TILE, GATE_TILE).astype(BF16)


def kernel(x_prompt, x_sample, state_lru_h, state_lru_conv, state_dn_s, state_dn_conv, state_pool, norm_mix, w_in, lru_conv_w, lru_conv_b, lru_gate_a_w, lru_gate_a_b, lru_gate_x_w, lru_gate_x_b, lru_lambda, dn_conv_w, dn_a_log, dn_dt_bias, dn_norm_w, pool_w, pool_scale, w_br_lru, w_br_dn, w_br_pool, w_out, norm_mlp, w_up, w_down, norm_final):
    bp, tp, d = x_prompt.shape
    bs, ts, _ = x_sample.shape
    depth = w_in.shape[0]
    half = d // 2
    heads = half // HEAD_DIM
    mp, ms = bp * tp, bs * ts
    m = mp + ms
    assert ts == SUBLANES and tp % DN_CHUNK == 0 and bs % (DN_CHUNK // ts) == 0
    assert lru_gate_a_w.shape[-1] == LRU_BLOCK and half % GATE_TILE == 0
    assert state_pool.shape[-2] == POOL_BUF

    n_pre = 6 * half
    ba0 = n_pre
    w_main = jnp.concatenate([w_in[:, :, :n_pre], w_in[:, :, ba0 + 2 * heads:]], axis=2).astype(BF16)
    w_ba = jnp.zeros((depth, d, 2 * HEAD_DIM), F32)
    w_ba = w_ba.at[:, :, :heads].set(w_in[:, :, ba0:ba0 + heads])
    w_ba = w_ba.at[:, :, HEAD_DIM:HEAD_DIM + heads].set(w_in[:, :, ba0 + heads:ba0 + 2 * heads]).astype(BF16)
    pool_col = n_pre // half
    gate_col0 = n_pre + half

    def row(a):
        return a.reshape(depth, 1, a.shape[-1])

    def lane_pad(a):
        return jnp.pad(a, ((0, 0), (0, HEAD_DIM - a.shape[-1]))).reshape(depth, 1, HEAD_DIM)

    lru_w = (lru_conv_w, row(lru_conv_b), _block_diag_tiles(lru_gate_a_w), row(lru_gate_a_b),
             _block_diag_tiles(lru_gate_x_w), row(lru_gate_x_b), row(lru_lambda))
    alog, dtb, dn_nw = lane_pad(dn_a_log), lane_pad(dn_dt_bias), row(dn_norm_w)
    pool_wb, pool_sc = pool_w.astype(BF16), row(pool_scale)
    w_br_lru_b, w_br_dn_b, w_br_pool_b = w_br_lru.astype(BF16), w_br_dn.astype(BF16), w_br_pool.astype(BF16)
    w_out_b, w_up_b, w_down_b = w_out.astype(BF16), w_up.astype(BF16), w_down.astype(BF16)
    g_mix, g_mlp = row(norm_mix), row(norm_mlp)

    s_lru_h0 = state_lru_h.reshape(depth, bs, 1, half)
    s_lru_prev = _pad_front(state_lru_conv, SUBLANES)
    s_dn_prev = _pad_front(state_dn_conv, SUBLANES)
    s_pool_hist = _pad_front(state_pool, 2 * SUBLANES)
    p_lru_h0 = jnp.zeros((bp, 1, half), F32)
    p_lru_prev = jnp.zeros((bp, SUBLANES, half), F32)
    p_dn_prev = jnp.zeros((bp, SUBLANES, 3 * half), F32)
    p_dn_s0 = jnp.zeros((bp, heads, HEAD_DIM, HEAD_DIM), F32)
    p_pool_hist = jnp.zeros((bp, 2 * SUBLANES, half), F32)

    x = jnp.concatenate([x_prompt.reshape(mp, d), x_sample.reshape(ms, d)], axis=0)

    tm = 512 if (mp % 512 == 0 and ms % 512 == 0) else 256
    tm_down = 1024 if (mp % 1024 == 0 and ms % 1024 == 0) else tm
    lru_g = 32
    assert (tp // SUBLANES) % lru_g == 0 and bs % lru_g == 0 and m % tm == 0 and mp % tm == 0
    n_t = tp // SUBLANES // lru_g
    long_kw = dict(long_seq=True, n_outer=bp, n_inner=n_t, g_n=lru_g, group0=0, state_layer=None)
    short_kw = dict(long_seq=False, n_outer=bs // lru_g, n_inner=1, g_n=lru_g, group0=mp // SUBLANES)
    seq_per_blk = DN_CHUNK // ts

    p_states, s_states = [], []
    for l in range(depth):
        proj, ba = _rms_matmul(x, g_mix, w_main, l, tm=tm, tn=half, out_dtype=F32, w_small=w_ba)
        proj3 = proj.reshape(m // SUBLANES, SUBLANES, proj.shape[-1])

        o_lru, p_h, p_cb = _lru_call(proj3, None, p_lru_prev, p_lru_h0, lru_w, l, **long_kw)
        o_lru, s_h, s_cb = _lru_call(proj3, o_lru, s_lru_prev, s_lru_h0, lru_w, l, state_layer=l, **short_kw)

        o_dn, p_s, p_db = _delta_call(proj3, ba, None, p_dn_prev, p_dn_s0, dn_conv_w, alog, dtb, dn_nw, l,
                                      long_seq=True, tseq=DN_CHUNK, n_outer=bp, n_inner=tp // DN_CHUNK, group0=0,
                                      state_layer=None)
        o_dn, s_s, s_db = _delta_call(proj3, ba, o_dn, s_dn_prev, state_dn_s, dn_conv_w, alog, dtb, dn_nw, l,
                                      long_seq=False, tseq=ts, n_outer=bs // seq_per_blk, n_inner=1,
                                      group0=mp // SUBLANES, state_layer=l)

        o_pool, p_ph = _pool_call(proj3, pool_col, None, p_pool_hist, pool_wb, pool_sc, l, past_len=0, **long_kw)
        o_pool, s_ph = _pool_call(proj3, pool_col, o_pool, s_pool_hist, pool_wb, pool_sc, l, past_len=PAST_LEN,
                                  state_layer=l, **short_kw)

        merged = _merge(o_lru, o_dn, o_pool, proj, gate_col0, w_br_lru_b, w_br_dn_b, w_br_pool_b, l, tm=tm, tn=half)
        x = _matmul_res(merged, w_out_b, l, x, tm=tm, tn=half, tk=d)
        hm = _rms_matmul(x, g_mlp, w_up_b, l, tm=tm, tn=half, out_dtype=BF16, relu2=True)
        x = _matmul_res(hm, w_down_b, l, x, tm=tm_down, tn=half, tk=half)
        p_states.append((p_h, p_cb, p_s, p_db, p_ph))
        s_states.append((s_h, s_cb, s_s, s_db, s_ph))

    y_prompt = _final_norm(x, norm_final.reshape(1, d), 0, mp, tm=tm).reshape(bp, tp, d)
    y_sample = _final_norm(x, norm_final.reshape(1, d), mp, ms, tm=tm).reshape(bs, ts, d)

    def collect(states, nseq):
        tail = CONV_WIDTH - 1
        h = jnp.stack([s[0] for s in states]).reshape(depth, nseq, half)
        cb = jnp.stack([s[1] for s in states])[:, :, SUBLANES - tail:]
        sm = jnp.stack([s[2] for s in states])
        db = jnp.stack([s[3] for s in states])[:, :, SUBLANES - tail:]
        ph = jnp.stack([s[4] for s in states])[:, :, 2 * SUBLANES - POOL_BUF:]
        return h, cb, sm, db, ph

    return (y_prompt, y_sample) + collect(p_states, bp) + collect(s_states, bs)
```

```python
import functools

import jax
import jax.numpy as jnp
from jax import lax
from jax.experimental import pallas as pl
from jax.experimental.pallas import tpu as pltpu

F32 = jnp.float32
BF16 = jnp.bfloat16

EPS = 1e-6
LRU_C = 8.0
CONV_WIDTH = 4
HEAD_DIM = 128
LRU_BLOCK = 64
POOL_WINDOWS = (2, 4, 8, 16)
POOL_BUF = 15
DN_CHUNK = 64
PAST_LEN = 16384
SUBLANES = 8
GATE_TILE = 256
VMEM_LIMIT = 56 * 1024 * 1024

_NT = (((1,), (1,)), ((), ()))
_TN = (((0,), (0,)), ((), ()))
_HI = lax.Precision.HIGHEST


def _params(*sem):
    return pltpu.CompilerParams(dimension_semantics=sem, vmem_limit_bytes=VMEM_LIMIT)


def _full(shape):
    return pl.BlockSpec(shape, lambda *_: (0,) * len(shape))


def _layer(l, shape):
    return pl.BlockSpec((None,) + shape, lambda *_: (l,) + (0,) * len(shape))


def _rms_matmul_body(*refs, relu2, small):
    if small:
        x_ref, g_ref, w_ref, ws_ref, o_ref, os_ref, xn_ref = refs
    else:
        x_ref, g_ref, w_ref, o_ref, xn_ref = refs

    @pl.when(pl.program_id(1) == 0)
    def _():
        x = x_ref[...]
        inv = lax.rsqrt(jnp.mean(x * x, axis=-1, keepdims=True) + EPS)
        xn = (x * inv * g_ref[...]).astype(BF16)
        xn_ref[...] = xn
        if small:
            os_ref[...] = jnp.dot(xn, ws_ref[...], preferred_element_type=F32)

    acc = jnp.dot(xn_ref[...], w_ref[...], preferred_element_type=F32)
    if relu2:
        acc = jnp.square(jnp.maximum(acc, 0.0))
    o_ref[...] = acc.astype(o_ref.dtype)


def _rms_matmul(x, gain, w, l, *, tm, tn, out_dtype, relu2=False, w_small=None):
    m, d = x.shape
    n = w.shape[-1]
    small = w_small is not None
    in_specs = [pl.BlockSpec((tm, d), lambda i, j: (i, 0)),
                _layer(l, (1, d)),
                pl.BlockSpec((None, d, tn), lambda i, j: (l, 0, j))]
    out_specs = [pl.BlockSpec((tm, tn), lambda i, j: (i, j))]
    out_shape = [jax.ShapeDtypeStruct((m, n), out_dtype)]
    args = [x, gain, w]
    if small:
        ns = w_small.shape[-1]
        in_specs.append(_layer(l, (d, ns)))
        out_specs.append(pl.BlockSpec((tm, ns), lambda i, j: (i, 0)))
        out_shape.append(jax.ShapeDtypeStruct((m, ns), F32))
        args.append(w_small)
    res = pl.pallas_call(
        functools.partial(_rms_matmul_body, relu2=relu2, small=small),
        grid=(m // tm, n // tn),
        in_specs=in_specs, out_specs=out_specs, out_shape=out_shape,
        scratch_shapes=[pltpu.VMEM((tm, d), BF16)],
        compiler_params=_params("parallel", "arbitrary"),
        name="rms_matmul_small" if small else "rms_matmul",
    )(*args)
    return res if small else res[0]


def _matmul_res_body(a_ref, w_ref, r_ref, o_ref):
    @pl.when(pl.program_id(2) == 0)
    def _():
        o_ref[...] = r_ref[...]

    o_ref[...] += jnp.dot(a_ref[...], w_ref[...], preferred_element_type=F32)


def _matmul_res(a, w, l, res, *, tm, tn, tk):
    m, k = a.shape
    n = w.shape[-1]
    return pl.pallas_call(
        _matmul_res_body,
        grid=(m // tm, n // tn, k // tk),
        in_specs=[pl.BlockSpec((tm, tk), lambda i, j, kk: (i, kk)),
                  pl.BlockSpec((None, tk, tn), lambda i, j, kk: (l, kk, j)),
                  pl.BlockSpec((tm, tn), lambda i, j, kk: (i, j))],
        out_specs=pl.BlockSpec((tm, tn), lambda i, j, kk: (i, j)),
        out_shape=jax.ShapeDtypeStruct((m, n), F32),
        compiler_params=_params("parallel", "parallel", "arbitrary"),
        name="matmul_res",
    )(a, w, res)


def _merge_body(ol_ref, od_ref, op_ref, gl_ref, gd_ref, gp_ref, wl_ref, wd_ref, wp_ref, o_ref):
    def branch(o_ref_, g_ref_, w_ref_):
        return jax.nn.sigmoid(g_ref_[...]) * jnp.dot(o_ref_[...], w_ref_[...], preferred_element_type=F32)

    m = branch(ol_ref, gl_ref, wl_ref) + branch(od_ref, gd_ref, wd_ref) + branch(op_ref, gp_ref, wp_ref)
    o_ref[...] = m.astype(o_ref.dtype)


def _merge(o_lru, o_dn, o_pool, proj, gate_col0, w_lru, w_dn, w_pool, l, *, tm, tn):
    m, half = o_lru.shape
    d = w_lru.shape[-1]
    nj = d // tn
    g0 = gate_col0 // tn

    def o_spec():
        return pl.BlockSpec((tm, half), lambda i, j: (i, 0))

    def g_spec(b):
        return pl.BlockSpec((tm, tn), lambda i, j: (i, g0 + b * nj + j))

    def w_spec():
        return pl.BlockSpec((None, half, tn), lambda i, j: (l, 0, j))

    return pl.pallas_call(
        _merge_body,
        grid=(m // tm, nj),
        in_specs=[o_spec(), o_spec(), o_spec(), g_spec(0), g_spec(1), g_spec(2), w_spec(), w_spec(), w_spec()],
        out_specs=pl.BlockSpec((tm, tn), lambda i, j: (i, j)),
        out_shape=jax.ShapeDtypeStruct((m, d), BF16),
        compiler_params=_params("parallel", "arbitrary"),
        name="merge",
    )(o_lru, o_dn, o_pool, proj, proj, proj, w_lru, w_dn, w_pool)


def _rmsnorm_body(x_ref, g_ref, o_ref):
    x = x_ref[...]
    inv = lax.rsqrt(jnp.mean(x * x, axis=-1, keepdims=True) + EPS)
    o_ref[...] = x * inv * g_ref[...]


def _final_norm(x, gain, row0, rows, *, tm):
    d = x.shape[-1]
    b0 = row0 // tm
    return pl.pallas_call(
        _rmsnorm_body,
        grid=(rows // tm,),
        in_specs=[pl.BlockSpec((tm, d), lambda i: (b0 + i, 0)), _full((1, d))],
        out_specs=pl.BlockSpec((tm, d), lambda i: (i, 0)),
        out_shape=jax.ShapeDtypeStruct((rows, d), F32),
        compiler_params=_params("parallel"),
        name="final_norm",
    )(x, gain)


def _softplus(x):
    return jnp.maximum(x, 0.0) + jnp.log1p(jnp.exp(-jnp.abs(x)))


def _shift_rows(cur, prev, k):
    ax = cur.ndim - 2
    row = lax.broadcasted_iota(jnp.int32, cur.shape, ax)
    return jnp.where(row >= k, pltpu.roll(cur, k, ax), pltpu.roll(prev, k, ax))


def _prev_groups(x, first):
    return jnp.concatenate([first[None], x[:-1]], axis=0)


def _causal_conv(x, prev, cw):
    y = cw[CONV_WIDTH - 1:CONV_WIDTH] * x
    for k in range(1, CONV_WIDTH):
        y = y + cw[CONV_WIDTH - 1 - k:CONV_WIDTH - k] * _shift_rows(x, prev, k)
    return y


def _lru_body(*refs, long_seq):
    (x_ref, gate_ref, prev_ref, h0_ref, cw_ref, cb_ref, wa_ref, ba_ref, wx_ref, bx_ref, lam_ref) = refs[:11]
    refs = refs[11:]
    if not long_seq:
        refs = refs[1:]
    o_ref, hl_ref, nb_ref, a_scr, b_scr, h_scr, cx_scr, ch_scr = refs
    g_n, _, c = x_ref.shape
    rows = g_n * SUBLANES
    x = x_ref[...]
    if long_seq:
        @pl.when(pl.program_id(1) == 0)
        def _():
            cx_scr[...] = prev_ref[0]
            ch_scr[...] = h0_ref[0]

        prev = _prev_groups(x, cx_scr[...])
    else:
        prev = prev_ref[...]
    xc = _causal_conv(x, prev, cw_ref[...]) + cb_ref[...]
    if long_seq:
        cx_scr[...] = x[g_n - 1]
        nb_ref[0] = x[g_n - 1]
    else:
        nb_ref[...] = x

    xc2 = xc.reshape(rows, c)
    xb = xc2.astype(BF16)
    nblk = c // GATE_TILE

    def gate(w_ref, b_ref):
        parts = [jnp.dot(xb[:, i * GATE_TILE:(i + 1) * GATE_TILE], w_ref[i], preferred_element_type=F32)
                 for i in range(nblk)]
        return jax.nn.sigmoid(jnp.concatenate(parts, axis=1) + b_ref[...])

    r = gate(wa_ref, ba_ref)
    i_g = gate(wx_ref, bx_ref)
    log_a = (-LRU_C) * r * _softplus(-lam_ref[...])
    a = jnp.exp(log_a)
    one_minus_a2 = -jnp.tanh(log_a) * (a * a + 1.0)
    b = jnp.sqrt(one_minus_a2) * (i_g * xc2)

    a3 = a.reshape(g_n, SUBLANES, c)
    b3 = b.reshape(g_n, SUBLANES, c)
    row = lax.broadcasted_iota(jnp.int32, a3.shape, 1)
    s = 1
    while s < SUBLANES:
        m = row >= s
        a_sh = pltpu.roll(a3, s, 1)
        b_sh = pltpu.roll(b3, s, 1)
        b3 = jnp.where(m, a3 * b_sh + b3, b3)
        a3 = jnp.where(m, a3 * a_sh, a3)
        s *= 2
    a_scr[...] = a3
    b_scr[...] = b3

    def chain(g, h_in):
        if not long_seq:
            h_in = h0_ref[g]
        hg = a_scr[g] * h_in + b_scr[g]
        h_scr[g] = hg
        h_out = hg[SUBLANES - 1:SUBLANES]
        if not long_seq:
            hl_ref[g] = h_out
        return h_out

    h_init = ch_scr[...] if long_seq else jnp.zeros((1, c), F32)
    h_fin = lax.fori_loop(0, g_n, chain, h_init)
    if long_seq:
        ch_scr[...] = h_fin
        hl_ref[0] = h_fin
    h = h_scr[...].reshape(rows, c)
    gl = gate_ref[...].reshape(rows, c)
    o_ref[...] = (h * jax.nn.gelu(gl)).astype(o_ref.dtype)


def _lru_call(proj3, col0, o_prev, prev, h0, weights, l, *, long_seq, n_outer, n_inner, g_n, group0, state_layer):
    cw, cb, wa, ba, wx, bx, lam = weights
    c = cw.shape[-1]
    m = proj3.shape[0] * SUBLANES
    nseq_blk = 1 if long_seq else g_n
    nseq = n_outer * nseq_blk
    gb0 = group0 // g_n

    def x_spec(col):
        return pl.BlockSpec((g_n, SUBLANES, c), lambda i, t: (gb0 + i * n_inner + t, 0, col))

    def state_spec(r):
        if state_layer is None:
            return pl.BlockSpec((nseq_blk, r, c), lambda i, t: (i, 0, 0))
        return pl.BlockSpec((None, nseq_blk, r, c), lambda i, t: (state_layer, i, 0, 0))

    ng = c // GATE_TILE
    in_specs = [x_spec(col0), x_spec(col0 + 1), state_spec(SUBLANES), state_spec(1),
                _layer(l, (CONV_WIDTH, c)), _layer(l, (1, c)),
                _layer(l, (ng, GATE_TILE, GATE_TILE)), _layer(l, (1, c)),
                _layer(l, (ng, GATE_TILE, GATE_TILE)), _layer(l, (1, c)), _layer(l, (1, c))]
    args = [proj3, proj3, prev, h0, cw, cb, wa, ba, wx, bx, lam]
    aliases = {}
    if o_prev is not None:
        in_specs.append(pl.BlockSpec(memory_space=pl.ANY))
        args.append(o_prev)
        aliases = {len(args) - 1: 0}
    out_specs = [pl.BlockSpec((g_n * SUBLANES, c), lambda i, t: (gb0 + i * n_inner + t, 0)),
                 pl.BlockSpec((nseq_blk, 1, c), lambda i, t: (i, 0, 0)),
                 pl.BlockSpec((nseq_blk, SUBLANES, c), lambda i, t: (i, 0, 0))]
    out_shape = [jax.ShapeDtypeStruct((m, c), BF16),
                 jax.ShapeDtypeStruct((nseq, 1, c), F32),
                 jax.ShapeDtypeStruct((nseq, SUBLANES, c), F32)]
    grp = (g_n, SUBLANES, c)
    return pl.pallas_call(
        functools.partial(_lru_body, long_seq=long_seq),
        grid=(n_outer, n_inner),
        in_specs=in_specs, out_specs=out_specs, out_shape=out_shape,
        scratch_shapes=[pltpu.VMEM(grp, F32), pltpu.VMEM(grp, F32), pltpu.VMEM(grp, F32),
                        pltpu.VMEM((SUBLANES, c), F32), pltpu.VMEM((1, c), F32)],
        input_output_aliases=aliases,
        compiler_params=_params("parallel", "arbitrary"),
        name="lru_long" if long_seq else "lru_short",
    )(*args)


def _pool_body(*refs, long_seq, past_len):
    u_ref, hist_ref, pw_ref, ps_ref = refs[:4]
    refs = refs[4:]
    if not long_seq:
        refs = refs[1:]
    o_ref, nh_ref, c_scr = refs
    g_n, _, c = u_ref.shape
    rows = g_n * SUBLANES
    x = u_ref[...]
    t = pl.program_id(1)
    if long_seq:
        @pl.when(t == 0)
        def _():
            c_scr[0] = hist_ref[0, 0:SUBLANES]
            c_scr[1] = hist_ref[0, SUBLANES:2 * SUBLANES]

        ext = jnp.concatenate([c_scr[...], x], axis=0)
        gs = 1
    else:
        hist = hist_ref[...]
        ext = jnp.concatenate([hist[:, 0:SUBLANES], hist[:, SUBLANES:2 * SUBLANES], x], axis=0)
        gs = g_n
    nh_ref[:, 0:SUBLANES] = ext[-2 * gs:-gs]
    nh_ref[:, SUBLANES:2 * SUBLANES] = ext[-gs:]
    if long_seq:
        c_scr[0] = ext[g_n]
        c_scr[1] = ext[g_n + 1]

    def prev(a):
        return jnp.concatenate([a[:gs], a[:-gs]], axis=0)

    n_grp = len(POOL_WINDOWS)
    pg = c // n_grp
    i0 = lax.broadcasted_iota(jnp.int32, (g_n, SUBLANES, pg), 0)
    i1 = lax.broadcasted_iota(jnp.int32, (g_n, SUBLANES, pg), 1)
    t_abs = (t * g_n + i0) * SUBLANES + i1 if long_seq else i1
    outs = []
    for gi, w in enumerate(POOL_WINDOWS):
        eg = ext[:, :, gi * pg:(gi + 1) * pg]
        s = eg
        k = 1
        while k < w:
            s = s + (_shift_rows(s, prev(s), k) if k < SUBLANES else prev(s))
            k *= 2
        cnt = jnp.minimum(t_abs + (1 + past_len), w).astype(F32)
        d = s[2 * gs:] / cnt - eg[2 * gs:]
        d2 = d.reshape(rows, pg).astype(BF16)
        outs.append(jnp.dot(d2, pw_ref[gi], preferred_element_type=F32))
    y = jnp.concatenate(outs, axis=1) * ps_ref[...]
    o_ref[...] = y.astype(o_ref.dtype)


def _pool_call(proj3, col, o_prev, hist, pool_w, pool_scale, l, *, long_seq, past_len, n_outer, n_inner, g_n,
               group0, state_layer):
    c = pool_scale.shape[-1]
    m = proj3.shape[0] * SUBLANES
    nseq_blk = 1 if long_seq else g_n
    nseq = n_outer * nseq_blk
    gb0 = group0 // g_n
    pg = c // len(POOL_WINDOWS)
    hrows = 2 * SUBLANES
    if state_layer is None:
        h_spec = pl.BlockSpec((nseq_blk, hrows, c), lambda i, t: (i, 0, 0))
    else:
        h_spec = pl.BlockSpec((None, nseq_blk, hrows, c), lambda i, t: (state_layer, i, 0, 0))
    in_specs = [pl.BlockSpec((g_n, SUBLANES, c), lambda i, t: (gb0 + i * n_inner + t, 0, col)), h_spec,
                _layer(l, (len(POOL_WINDOWS), pg, pg)), _layer(l, (1, c))]
    args = [proj3, hist, pool_w, pool_scale]
    aliases = {}
    if o_prev is not None:
        in_specs.append(pl.BlockSpec(memory_space=pl.ANY))
        args.append(o_prev)
        aliases = {len(args) - 1: 0}
    return pl.pallas_call(
        functools.partial(_pool_body, long_seq=long_seq, past_len=past_len),
        grid=(n_outer, n_inner),
        in_specs=in_specs,
        out_specs=[pl.BlockSpec((g_n * SUBLANES, c), lambda i, t: (gb0 + i * n_inner + t, 0)),
                   pl.BlockSpec((nseq_blk, hrows, c), lambda i, t: (i, 0, 0))],
        out_shape=[jax.ShapeDtypeStruct((m, c), BF16), jax.ShapeDtypeStruct((nseq, hrows, c), F32)],
        scratch_shapes=[pltpu.VMEM((2, SUBLANES, c), F32)],
        input_output_aliases=aliases,
        compiler_params=_params("parallel", "arbitrary"),
        name="pool_long" if long_seq else "pool_short",
    )(*args)


SOLVE_PASSES = 1


def _mm(a, b):
    a_hi, b_hi = a.astype(BF16), b.astype(BF16)
    out = jnp.dot(a_hi, b_hi, preferred_element_type=F32)
    if SOLVE_PASSES == 3:
        a_lo = (a - a_hi.astype(F32)).astype(BF16)
        b_lo = (b - b_hi.astype(F32)).astype(BF16)
        out = out + jnp.dot(a_hi, b_lo, preferred_element_type=F32) + jnp.dot(a_lo, b_hi, preferred_element_type=F32)
    return out


def _delta_body(*refs, long_seq, tseq, n_aliased):
    (q_ref, k_ref, v_ref, z_ref, ba_ref, prev_ref, s0_ref, cw_ref, alog_ref, dtb_ref, nw_ref) = refs[:11]
    o_ref, s_ref, nb_ref, cx_scr = refs[11 + n_aliased:]
    g_n, _, c = q_ref.shape
    rows = g_n * SUBLANES
    nseq_blk = rows // tseq
    heads = c // HEAD_DIM
    first = pl.program_id(1) == 0

    if long_seq:
        @pl.when(first)
        def _():
            cx_scr[...] = prev_ref[0]
            s_ref[...] = s0_ref[...]

    cw = cw_ref[...]
    streams = []
    for idx, x_ref in enumerate((q_ref, k_ref, v_ref)):
        x = x_ref[...]
        lo, hi = idx * c, (idx + 1) * c
        if long_seq:
            prev = _prev_groups(x, cx_scr[:, lo:hi])
            cx_scr[:, lo:hi] = x[g_n - 1]
            nb_ref[0, :, lo:hi] = x[g_n - 1]
        else:
            prev = prev_ref[:, :, lo:hi]
            nb_ref[:, :, lo:hi] = x
        y = _causal_conv(x, prev, cw[:, lo:hi])
        streams.append((y * jax.nn.sigmoid(y)).reshape(rows, c))
    q_all, k_all, v_all = streams
    z_all = z_ref[...].reshape(rows, c)

    ba = ba_ref[...]
    beta_all = jax.nn.sigmoid(ba[:, 0:HEAD_DIM])
    g_all = -jnp.exp(alog_ref[...]) * _softplus(ba[:, HEAD_DIM:2 * HEAD_DIM] + dtb_ref[...])

    ri = lax.broadcasted_iota(jnp.int32, (rows, rows), 0)
    ci = lax.broadcasted_iota(jnp.int32, (rows, rows), 1)
    same = (ri // tseq) == (ci // tseq)
    incl = same & (ci <= ri)
    strict = same & (ci < ri)
    gamma_all = jnp.dot(incl.astype(F32), g_all, precision=_HI, preferred_element_type=F32)
    gtot_all = jnp.dot(same.astype(F32), g_all, precision=_HI, preferred_element_type=F32)
    gamma_t = gamma_all.T
    eye = (ri == ci).astype(F32)

    s_src = s_ref if long_seq else s0_ref
    hrange = range(heads)
    q_l, k_l, v_l, beta_l, gcol_l, gtot_l, decay_l, qb_l, kb_l, a_l = ([] for _ in range(10))
    for h in hrange:
        hs = slice(h * HEAD_DIM, (h + 1) * HEAD_DIM)
        q_h, k_h = q_all[:, hs], k_all[:, hs]
        q_h = q_h * lax.rsqrt(jnp.sum(q_h * q_h, axis=-1, keepdims=True) + EPS) * (HEAD_DIM ** -0.5)
        k_h = k_h * lax.rsqrt(jnp.sum(k_h * k_h, axis=-1, keepdims=True) + EPS)
        gcol = gamma_all[:, h:h + 1]
        decay = jnp.where(incl, jnp.exp(gcol - gamma_t[h:h + 1, :]), 0.0)
        qb, kb = q_h.astype(BF16), k_h.astype(BF16)
        beta = beta_all[:, h:h + 1]
        kk = lax.dot_general(kb, kb, _NT, preferred_element_type=F32)
        q_l.append(q_h)
        k_l.append(k_h)
        v_l.append(v_all[:, hs])
        beta_l.append(beta)
        gcol_l.append(gcol)
        gtot_l.append(gtot_all[:, h:h + 1])
        decay_l.append(decay)
        qb_l.append(qb)
        kb_l.append(kb)
        a_l.append(jnp.where(strict, beta * decay * kk, 0.0))

    t_l = [eye - jnp.where(ri // 2 == ci // 2, a, 0.0) for a in a_l]
    s = 2
    while s < tseq:
        off_mask = (ri // (2 * s) == ci // (2 * s)) & (ri % (2 * s) >= s) & (ci % (2 * s) < s)
        t_off = [_mm(t, jnp.where(off_mask, a, 0.0)) for t, a in zip(t_l, a_l)]
        t_l = [t - _mm(to, t) for t, to in zip(t_l, t_off)]
        s *= 2

    eg_l = [jnp.exp(g) for g in gcol_l]
    sol_l = [_mm(t, jnp.concatenate([b * v, (b * e) * k], axis=1))
             for t, b, v, e, k in zip(t_l, beta_l, v_l, eg_l, k_l)]
    qk_l = [(lax.dot_general(qb, kb, _NT, preferred_element_type=F32) * d).astype(BF16)
            for qb, kb, d in zip(qb_l, kb_l, decay_l)]
    qg_l = [(q * e).astype(BF16) for q, e in zip(q_l, eg_l)]
    kd_l = [(k * jnp.exp(gt - g)).astype(BF16) for k, gt, g in zip(k_l, gtot_l, gcol_l)]

    w_l, o_l, s_old = [], [], []
    for h in hrange:
        u_c, wkb = sol_l[h][:, :HEAD_DIM], sol_l[h][:, HEAD_DIM:].astype(BF16)
        w_parts, o_parts = [], []
        for sq in range(nseq_blk):
            rs = slice(sq * tseq, (sq + 1) * tseq)
            s_mat = s_src[sq, h]
            sb = s_mat.astype(BF16)
            s_old.append(s_mat)
            w_parts.append(u_c[rs] - jnp.dot(wkb[rs], sb, preferred_element_type=F32))
            o_parts.append(jnp.dot(qg_l[h][rs], sb, preferred_element_type=F32))
        w_l.append((jnp.concatenate(w_parts, axis=0) if nseq_blk > 1 else w_parts[0]).astype(BF16))
        o_l.append(jnp.concatenate(o_parts, axis=0) if nseq_blk > 1 else o_parts[0])
    for h in hrange:
        hs = slice(h * HEAD_DIM, (h + 1) * HEAD_DIM)
        for sq in range(nseq_blk):
            rs = slice(sq * tseq, (sq + 1) * tseq)
            g_last = jnp.exp(gtot_l[h][sq * tseq:sq * tseq + 1])
            s_ref[sq, h] = g_last * s_old[h * nseq_blk + sq] + lax.dot_general(
                kd_l[h][rs], w_l[h][rs], _TN, preferred_element_type=F32)
        o = o_l[h] + jnp.dot(qk_l[h], w_l[h], preferred_element_type=F32)
        o = o * lax.rsqrt(jnp.mean(o * o, axis=-1, keepdims=True) + EPS) * nw_ref[...]
        z_h = z_all[:, hs]
        o_ref[:, hs] = (o * (z_h * jax.nn.sigmoid(z_h))).astype(o_ref.dtype)


def _delta_call(proj3, col0, ba, o_prev, prev, s0, conv_w, alog, dtb, norm_w, l, *, long_seq, tseq, n_outer, n_inner,
                group0, state_layer, s_stack=None):
    c = conv_w.shape[-1] // 3
    heads = c // HEAD_DIM
    g_n = DN_CHUNK // SUBLANES
    m = proj3.shape[0] * SUBLANES
    nseq_blk = 1 if long_seq else DN_CHUNK // tseq
    nseq = n_outer * nseq_blk
    gb0 = group0 // g_n

    def x_spec(col):
        return pl.BlockSpec((g_n, SUBLANES, c), lambda i, t: (gb0 + i * n_inner + t, 0, col))

    if state_layer is None:
        p_spec = pl.BlockSpec((nseq_blk, SUBLANES, 3 * c), lambda i, t: (i, 0, 0))
        s_spec = pl.BlockSpec((nseq_blk, heads, HEAD_DIM, HEAD_DIM), lambda i, t: (i, 0, 0, 0))
    else:
        p_spec = pl.BlockSpec((None, nseq_blk, SUBLANES, 3 * c), lambda i, t: (state_layer, i, 0, 0))
        s_spec = pl.BlockSpec((None, nseq_blk, heads, HEAD_DIM, HEAD_DIM), lambda i, t: (state_layer, i, 0, 0, 0))
    in_specs = [x_spec(col0), x_spec(col0 + 1), x_spec(col0 + 2), x_spec(col0 + 3),
                pl.BlockSpec((DN_CHUNK, 2 * HEAD_DIM), lambda i, t: (gb0 + i * n_inner + t, 0)),
                p_spec, s_spec,
                _layer(l, (CONV_WIDTH, 3 * c)), _layer(l, (1, HEAD_DIM)), _layer(l, (1, HEAD_DIM)),
                _layer(l, (1, HEAD_DIM))]
    args = [proj3, proj3, proj3, proj3, ba, prev, s0, conv_w, alog, dtb, norm_w]
    aliases = {}
    if o_prev is not None:
        in_specs.append(pl.BlockSpec(memory_space=pl.ANY))
        args.append(o_prev)
        aliases = {len(args) - 1: 0}
    s_block = (nseq_blk, heads, HEAD_DIM, HEAD_DIM)
    if s_stack is None:
        s_out_spec = pl.BlockSpec(s_block, lambda i, t: (i, 0, 0, 0))
        s_out_shape = jax.ShapeDtypeStruct((nseq,) + s_block[1:], F32)
    else:
        depth, stacked = s_stack
        s_out_spec = pl.BlockSpec((None,) + s_block, lambda i, t: (l, i, 0, 0, 0))
        s_out_shape = jax.ShapeDtypeStruct((depth, nseq) + s_block[1:], F32)
        if stacked is not None:
            in_specs.append(pl.BlockSpec(memory_space=pl.ANY))
            args.append(stacked)
            aliases[len(args) - 1] = 1
    return pl.pallas_call(
        functools.partial(_delta_body, long_seq=long_seq, tseq=tseq, n_aliased=len(aliases)),
        grid=(n_outer, n_inner),
        in_specs=in_specs,
        out_specs=[pl.BlockSpec((DN_CHUNK, c), lambda i, t: (gb0 + i * n_inner + t, 0)),
                   s_out_spec,
                   pl.BlockSpec((nseq_blk, SUBLANES, 3 * c), lambda i, t: (i, 0, 0))],
        out_shape=[jax.ShapeDtypeStruct((m, c), BF16),
                   s_out_shape,
                   jax.ShapeDtypeStruct((nseq, SUBLANES, 3 * c), F32)],
        scratch_shapes=[pltpu.VMEM((SUBLANES, 3 * c), F32)],
        input_output_aliases=aliases,
        compiler_params=_params("parallel", "arbitrary"),
        name="delta_long" if long_seq else "delta_short",
    )(*args)


def _pad_front(a, rows):
    pad = [(0, 0)] * a.ndim
    pad[-2] = (rows - a.shape[-2], 0)
    return jnp.pad(a, pad)


def _block_diag_tiles(w):
    depth, nb, b, _ = w.shape
    per = GATE_TILE // b
    w5 = w.reshape(depth, nb // per, per, b, b)
    eye = jnp.eye(per, dtype=w.dtype)
    t = jnp.einsum('lijcd,jk->lijckd', w5, eye)
    return t.reshape(depth, nb // per, GATE_TILE, GATE_TILE).astype(BF16)


def kernel(x_prompt, x_sample, state_lru_h, state_lru_conv, state_dn_s, state_dn_conv, state_pool, norm_mix, w_in, lru_conv_w, lru_conv_b, lru_gate_a_w, lru_gate_a_b, lru_gate_x_w, lru_gate_x_b, lru_lambda, dn_conv_w, dn_a_log, dn_dt_bias, dn_norm_w, pool_w, pool_scale, w_br_lru, w_br_dn, w_br_pool, w_out, norm_mlp, w_up, w_down, norm_final):
    bp, tp, d = x_prompt.shape
    bs, ts, _ = x_sample.shape
    depth = w_in.shape[0]
    half = d // 2
    heads = half // HEAD_DIM
    mp, ms = bp * tp, bs * ts
    m = mp + ms
    assert ts == SUBLANES and tp % DN_CHUNK == 0 and bs % (DN_CHUNK // ts) == 0
    assert lru_gate_a_w.shape[-1] == LRU_BLOCK and half % GATE_TILE == 0
    assert state_pool.shape[-2] == POOL_BUF

    n_pre = 6 * half
    ba0 = n_pre
    pool0 = ba0 + 2 * heads
    gates0 = pool0 + half
    w_main = jnp.concatenate([w_in[:, :, gates0:], w_in[:, :, :n_pre], w_in[:, :, pool0:gates0]], axis=2).astype(BF16)
    w_ba = jnp.zeros((depth, d, 2 * HEAD_DIM), F32)
    w_ba = w_ba.at[:, :, :heads].set(w_in[:, :, ba0:ba0 + heads])
    w_ba = w_ba.at[:, :, HEAD_DIM:HEAD_DIM + heads].set(w_in[:, :, ba0 + heads:ba0 + 2 * heads]).astype(BF16)
    col0 = 3 * d // half
    gate_col0 = 0

    def row(a):
        return a.reshape(depth, 1, a.shape[-1])

    def lane_pad(a):
        return jnp.pad(a, ((0, 0), (0, HEAD_DIM - a.shape[-1]))).reshape(depth, 1, HEAD_DIM)

    lru_w = (lru_conv_w, row(lru_conv_b), _block_diag_tiles(lru_gate_a_w), row(lru_gate_a_b),
             _block_diag_tiles(lru_gate_x_w), row(lru_gate_x_b), row(lru_lambda))
    alog, dtb, dn_nw = lane_pad(dn_a_log), lane_pad(dn_dt_bias), row(dn_norm_w)
    pool_wb, pool_sc = pool_w.astype(BF16), row(pool_scale)
    w_br_lru_b, w_br_dn_b, w_br_pool_b = w_br_lru.astype(BF16), w_br_dn.astype(BF16), w_br_pool.astype(BF16)
    w_out_b, w_up_b, w_down_b = w_out.astype(BF16), w_up.astype(BF16), w_down.astype(BF16)
    g_mix, g_mlp = row(norm_mix), row(norm_mlp)

    s_lru_h0 = state_lru_h.reshape(depth, bs, 1, half)
    s_lru_prev = _pad_front(state_lru_conv, SUBLANES)
    s_dn_prev = _pad_front(state_dn_conv, SUBLANES)
    s_pool_hist = _pad_front(state_pool, 2 * SUBLANES)
    p_lru_h0 = jnp.zeros((bp, 1, half), F32)
    p_lru_prev = jnp.zeros((bp, SUBLANES, half), F32)
    p_dn_prev = jnp.zeros((bp, SUBLANES, 3 * half), F32)
    p_dn_s0 = jnp.zeros((bp, heads, HEAD_DIM, HEAD_DIM), F32)
    p_pool_hist = jnp.zeros((bp, 2 * SUBLANES, half), F32)

    x = jnp.concatenate([x_prompt.reshape(mp, d), x_sample.reshape(ms, d)], axis=0)

    tm = 512 if (mp % 512 == 0 and ms % 512 == 0) else 256
    tm_big = 1024 if (mp % 1024 == 0 and ms % 1024 == 0) else tm
    lru_g = 32
    assert (tp // SUBLANES) % lru_g == 0 and bs % lru_g == 0 and m % tm == 0 and mp % tm == 0
    n_t = tp // SUBLANES // lru_g
    long_kw = dict(long_seq=True, n_outer=bp, n_inner=n_t, g_n=lru_g, group0=0, state_layer=None)
    short_kw = dict(long_seq=False, n_outer=bs // lru_g, n_inner=1, g_n=lru_g, group0=mp // SUBLANES)
    seq_per_blk = DN_CHUNK // ts

    p_states, s_states = [], []
    s_dn_stack = None
    for l in range(depth):
        proj, ba = _rms_matmul(x, g_mix, w_main, l, tm=tm_big, tn=half, out_dtype=F32, w_small=w_ba)
        proj3 = proj.reshape(m // SUBLANES, SUBLANES, proj.shape[-1])

        o_lru, p_h, p_cb = _lru_call(proj3, col0, None, p_lru_prev, p_lru_h0, lru_w, l, **long_kw)
        o_lru, s_h, s_cb = _lru_call(proj3, col0, o_lru, s_lru_prev, s_lru_h0, lru_w, l, state_layer=l, **short_kw)

        o_dn, p_s, p_db = _delta_call(proj3, col0 + 2, ba, None, p_dn_prev, p_dn_s0, dn_conv_w, alog, dtb, dn_nw, l,
                                      long_seq=True, tseq=DN_CHUNK, n_outer=bp, n_inner=tp // DN_CHUNK, group0=0,
                                      state_layer=None)
        o_dn, s_dn_stack, s_db = _delta_call(proj3, col0 + 2, ba, o_dn, s_dn_prev, state_dn_s, dn_conv_w, alog, dtb,
                                             dn_nw, l, long_seq=False, tseq=ts, n_outer=bs // seq_per_blk, n_inner=1,
                                             group0=mp // SUBLANES, state_layer=l, s_stack=(depth, s_dn_stack))

        o_pool, p_ph = _pool_call(proj3, col0 + 6, None, p_pool_hist, pool_wb, pool_sc, l, past_len=0, **long_kw)
        o_pool, s_ph = _pool_call(proj3, col0 + 6, o_pool, s_pool_hist, pool_wb, pool_sc, l, past_len=PAST_LEN,
                                  state_layer=l, **short_kw)

        merged = _merge(o_lru, o_dn, o_pool, proj, gate_col0, w_br_lru_b, w_br_dn_b, w_br_pool_b, l, tm=256, tn=d)
        x = _matmul_res(merged, w_out_b, l, x, tm=tm, tn=d, tk=d)
        hm = _rms_matmul(x, g_mlp, w_up_b, l, tm=tm_big, tn=half, out_dtype=BF16, relu2=True)
        x = _matmul_res(hm, w_down_b, l, x, tm=tm_big, tn=half, tk=d)
        p_states.append((p_h, p_cb, p_s, p_db, p_ph))
        s_states.append((s_h, s_cb, None, s_db, s_ph))

    y_prompt = _final_norm(x, norm_final.reshape(1, d), 0, mp, tm=tm).reshape(bp, tp, d)
    y_sample = _final_norm(x, norm_final.reshape(1, d), mp, ms, tm=tm).reshape(bs, ts, d)

    def collect(states, nseq, sm=None):
        tail = CONV_WIDTH - 1
        h = jnp.stack([s[0] for s in states]).reshape(depth, nseq, half)
        cb = jnp.stack([s[1] for s in states])[:, :, SUBLANES - tail:]
        if sm is None:
            sm = jnp.stack([s[2] for s in states])
        db = jnp.stack([s[3] for s in states])[:, :, SUBLANES - tail:]
        ph = jnp.stack([s[4] for s in states])[:, :, 2 * SUBLANES - POOL_BUF:]
        return h, cb, sm, db, ph

    return (y_prompt, y_sample) + collect(p_states, bp) + collect(s_states, bs, s_dn_stack)
```

```python
import functools

import jax
import jax.numpy as jnp
from jax import lax
from jax.experimental import pallas as pl
from jax.experimental.pallas import tpu as pltpu

F32 = jnp.float32
BF16 = jnp.bfloat16

EPS = 1e-6
LRU_C = 8.0
CONV_WIDTH = 4
HEAD_DIM = 128
LRU_BLOCK = 64
POOL_WINDOWS = (2, 4, 8, 16)
POOL_BUF = 15
DN_CHUNK = 64
PAST_LEN = 16384
SUBLANES = 8
GATE_TILE = 256
VMEM_LIMIT = 56 * 1024 * 1024

_NT = (((1,), (1,)), ((), ()))
_TN = (((0,), (0,)), ((), ()))
_HI = lax.Precision.HIGHEST


def _params(*sem):
    return pltpu.CompilerParams(dimension_semantics=sem, vmem_limit_bytes=VMEM_LIMIT)


def _full(shape):
    return pl.BlockSpec(shape, lambda *_: (0,) * len(shape))


def _layer(l, shape):
    return pl.BlockSpec((None,) + shape, lambda *_: (l,) + (0,) * len(shape))


def _rms(x, gain):
    return x * lax.rsqrt(jnp.mean(x * x, axis=-1, keepdims=True) + EPS) * gain


def _norm_cast_body(x_ref, g_ref, ws_ref, xn_ref, os_ref):
    xn = _rms(x_ref[...], g_ref[...]).astype(BF16)
    xn_ref[...] = xn
    os_ref[...] = jnp.dot(xn, ws_ref[...], preferred_element_type=F32)


def _norm_cast(x, gain, w_small, l, *, tm):
    m, d = x.shape
    ns = w_small.shape[-1]
    return pl.pallas_call(
        _norm_cast_body,
        grid=(m // tm,),
        in_specs=[pl.BlockSpec((tm, d), lambda i: (i, 0)), _layer(l, (1, d)), _layer(l, (d, ns))],
        out_specs=[pl.BlockSpec((tm, d), lambda i: (i, 0)), pl.BlockSpec((tm, ns), lambda i: (i, 0))],
        out_shape=[jax.ShapeDtypeStruct((m, d), BF16), jax.ShapeDtypeStruct((m, ns), F32)],
        compiler_params=_params("parallel"),
        name="norm_cast",
    )(x, gain, w_small)


def _stream_matmul_body(x_ref, w_ref, o_ref, wb_ref, *, w_rows_are_outputs, relu2):
    @pl.when(pl.program_id(1) == 0)
    def _():
        wb_ref[...] = w_ref[...].astype(BF16)

    dims = _NT if w_rows_are_outputs else (((1,), (0,)), ((), ()))
    acc = lax.dot_general(x_ref[...], wb_ref[...], dims, preferred_element_type=F32)
    if relu2:
        acc = jnp.square(jnp.maximum(acc, 0.0))
    o_ref[...] = acc.astype(o_ref.dtype)


def _stream_matmul(x, w, w_spec, w_block, n_tiles, *, tm, tn, out_dtype, w_rows_are_outputs, relu2=False):
    m, d = x.shape
    return pl.pallas_call(
        functools.partial(_stream_matmul_body, w_rows_are_outputs=w_rows_are_outputs, relu2=relu2),
        grid=(n_tiles, m // tm),
        in_specs=[pl.BlockSpec((tm, d), lambda j, i: (i, 0)), w_spec],
        out_specs=pl.BlockSpec((tm, tn), lambda j, i: (i, j)),
        out_shape=jax.ShapeDtypeStruct((m, n_tiles * tn), out_dtype),
        scratch_shapes=[pltpu.VMEM(w_block, BF16)],
        compiler_params=_params("parallel", "arbitrary"),
        name="stream_matmul_t" if w_rows_are_outputs else "stream_matmul",
    )(x, w)


def _matmul_res_body(*refs, n_k, emit_norm, emit_small):
    a_ref, w_ref, r_ref = refs[:3]
    refs = refs[3:]
    if emit_norm:
        g_ref = refs[0]
        refs = refs[1:]
    if emit_small:
        ws_ref = refs[0]
        refs = refs[1:]
    o_ref = refs[0]
    k = pl.program_id(1)

    @pl.when(k == 0)
    def _():
        o_ref[...] = r_ref[...]

    o_ref[...] += jnp.dot(a_ref[...], w_ref[...], preferred_element_type=F32)

    if emit_norm:
        @pl.when(k == n_k - 1)
        def _():
            xn = _rms(o_ref[...], g_ref[...]).astype(BF16)
            refs[1][...] = xn
            if emit_small:
                refs[2][...] = jnp.dot(xn, ws_ref[...], preferred_element_type=F32)


def _matmul_res(a, w, l, res, *, tm, tk, gain=None, gain_layer=0, w_small=None):
    m, k = a.shape
    d = w.shape[-1]
    n_k = k // tk
    emit_norm = gain is not None
    emit_small = w_small is not None
    in_specs = [pl.BlockSpec((tm, tk), lambda i, kk: (i, kk)),
                pl.BlockSpec((None, tk, d), lambda i, kk: (l, kk, 0)),
                pl.BlockSpec((tm, d), lambda i, kk: (i, 0))]
    out_specs = [pl.BlockSpec((tm, d), lambda i, kk: (i, 0))]
    out_shape = [jax.ShapeDtypeStruct((m, d), F32)]
    args = [a, w, res]
    if emit_norm:
        in_specs.append(_layer(gain_layer, (1, d)))
        args.append(gain)
        out_specs.append(pl.BlockSpec((tm, d), lambda i, kk: (i, 0)))
        out_shape.append(jax.ShapeDtypeStruct((m, d), BF16))
    if emit_small:
        ns = w_small.shape[-1]
        in_specs.append(_layer(gain_layer, (d, ns)))
        args.append(w_small)
        out_specs.append(pl.BlockSpec((tm, ns), lambda i, kk: (i, 0)))
        out_shape.append(jax.ShapeDtypeStruct((m, ns), F32))
    return pl.pallas_call(
        functools.partial(_matmul_res_body, n_k=n_k, emit_norm=emit_norm, emit_small=emit_small),
        grid=(m // tm, n_k),
        in_specs=in_specs, out_specs=out_specs, out_shape=out_shape,
        compiler_params=_params("parallel", "arbitrary"),
        name="matmul_res",
    )(*args)


def _merge_body(ol_ref, od_ref, op_ref, gl_ref, gd_ref, gp_ref, wl_ref, wd_ref, wp_ref, o_ref):
    def branch(o_ref_, g_ref_, w_ref_):
        return jax.nn.sigmoid(g_ref_[...]) * jnp.dot(o_ref_[...], w_ref_[...], preferred_element_type=F32)

    m = branch(ol_ref, gl_ref, wl_ref) + branch(od_ref, gd_ref, wd_ref) + branch(op_ref, gp_ref, wp_ref)
    o_ref[...] = m.astype(o_ref.dtype)


def _merge(o_lru, o_dn, o_pool, proj, gate_col0, w_lru, w_dn, w_pool, l, *, tm, tn):
    m, half = o_lru.shape
    d = w_lru.shape[-1]
    nj = d // tn
    g0 = gate_col0 // tn

    def o_spec():
        return pl.BlockSpec((tm, half), lambda i, j: (i, 0))

    def g_spec(b):
        return pl.BlockSpec((tm, tn), lambda i, j: (i, g0 + b * nj + j))

    def w_spec():
        return pl.BlockSpec((None, half, tn), lambda i, j: (l, 0, j))

    return pl.pallas_call(
        _merge_body,
        grid=(m // tm, nj),
        in_specs=[o_spec(), o_spec(), o_spec(), g_spec(0), g_spec(1), g_spec(2), w_spec(), w_spec(), w_spec()],
        out_specs=pl.BlockSpec((tm, tn), lambda i, j: (i, j)),
        out_shape=jax.ShapeDtypeStruct((m, d), BF16),
        compiler_params=_params("parallel", "arbitrary"),
        name="merge",
    )(o_lru, o_dn, o_pool, proj, proj, proj, w_lru, w_dn, w_pool)


def _rmsnorm_body(x_ref, g_ref, o_ref):
    x = x_ref[...]
    inv = lax.rsqrt(jnp.mean(x * x, axis=-1, keepdims=True) + EPS)
    o_ref[...] = x * inv * g_ref[...]


def _final_norm(x, gain, row0, rows, *, tm):
    d = x.shape[-1]
    b0 = row0 // tm
    return pl.pallas_call(
        _rmsnorm_body,
        grid=(rows // tm,),
        in_specs=[pl.BlockSpec((tm, d), lambda i: (b0 + i, 0)), _full((1, d))],
        out_specs=pl.BlockSpec((tm, d), lambda i: (i, 0)),
        out_shape=jax.ShapeDtypeStruct((rows, d), F32),
        compiler_params=_params("parallel"),
        name="final_norm",
    )(x, gain)


def _softplus(x):
    return jnp.maximum(x, 0.0) + jnp.log1p(jnp.exp(-jnp.abs(x)))


def _shift_rows(cur, prev, k):
    ax = cur.ndim - 2
    row = lax.broadcasted_iota(jnp.int32, cur.shape, ax)
    return jnp.where(row >= k, pltpu.roll(cur, k, ax), pltpu.roll(prev, k, ax))


def _prev_groups(x, first):
    return jnp.concatenate([first[None], x[:-1]], axis=0)


def _causal_conv(x, prev, cw):
    y = cw[CONV_WIDTH - 1:CONV_WIDTH] * x
    for k in range(1, CONV_WIDTH):
        y = y + cw[CONV_WIDTH - 1 - k:CONV_WIDTH - k] * _shift_rows(x, prev, k)
    return y


def _lru_body(*refs, long_seq):
    (x_ref, gate_ref, prev_ref, h0_ref, cw_ref, cb_ref, wa_ref, ba_ref, wx_ref, bx_ref, lam_ref) = refs[:11]
    refs = refs[11:]
    if not long_seq:
        refs = refs[1:]
    o_ref, hl_ref, nb_ref, a_scr, b_scr, h_scr, cx_scr, ch_scr = refs
    g_n, _, c = x_ref.shape
    rows = g_n * SUBLANES
    x = x_ref[...]
    if long_seq:
        @pl.when(pl.program_id(1) == 0)
        def _():
            cx_scr[...] = prev_ref[0]
            ch_scr[...] = h0_ref[0]

        prev = _prev_groups(x, cx_scr[...])
    else:
        prev = prev_ref[...]
    xc = _causal_conv(x, prev, cw_ref[...]) + cb_ref[...]
    if long_seq:
        cx_scr[...] = x[g_n - 1]
        nb_ref[0] = x[g_n - 1]
    else:
        nb_ref[...] = x

    xc2 = xc.reshape(rows, c)
    xb = xc2.astype(BF16)
    nblk = c // GATE_TILE

    def gate(w_ref, b_ref):
        parts = [jnp.dot(xb[:, i * GATE_TILE:(i + 1) * GATE_TILE], w_ref[i], preferred_element_type=F32)
                 for i in range(nblk)]
        return jax.nn.sigmoid(jnp.concatenate(parts, axis=1) + b_ref[...])

    r = gate(wa_ref, ba_ref)
    i_g = gate(wx_ref, bx_ref)
    log_a = (-LRU_C) * r * _softplus(-lam_ref[...])
    a = jnp.exp(log_a)
    one_minus_a2 = -jnp.tanh(log_a) * (a * a + 1.0)
    b = jnp.sqrt(one_minus_a2) * (i_g * xc2)

    a3 = a.reshape(g_n, SUBLANES, c)
    b3 = b.reshape(g_n, SUBLANES, c)
    row = lax.broadcasted_iota(jnp.int32, a3.shape, 1)
    s = 1
    while s < SUBLANES:
        m = row >= s
        a_sh = pltpu.roll(a3, s, 1)
        b_sh = pltpu.roll(b3, s, 1)
        b3 = jnp.where(m, a3 * b_sh + b3, b3)
        a3 = jnp.where(m, a3 * a_sh, a3)
        s *= 2
    a_scr[...] = a3
    b_scr[...] = b3

    def chain(g, h_in):
        if not long_seq:
            h_in = h0_ref[g]
        hg = a_scr[g] * h_in + b_scr[g]
        h_scr[g] = hg
        h_out = hg[SUBLANES - 1:SUBLANES]
        if not long_seq:
            hl_ref[g] = h_out
        return h_out

    h_init = ch_scr[...] if long_seq else jnp.zeros((1, c), F32)
    h_fin = lax.fori_loop(0, g_n, chain, h_init)
    if long_seq:
        ch_scr[...] = h_fin
        hl_ref[0] = h_fin
    h = h_scr[...].reshape(rows, c)
    gl = gate_ref[...].reshape(rows, c)
    o_ref[...] = (h * jax.nn.gelu(gl)).astype(o_ref.dtype)


def _lru_call(proj3, col0, o_prev, prev, h0, weights, l, *, long_seq, n_outer, n_inner, g_n, group0, state_layer):
    cw, cb, wa, ba, wx, bx, lam = weights
    c = cw.shape[-1]
    m = proj3.shape[0] * SUBLANES
    nseq_blk = 1 if long_seq else g_n
    nseq = n_outer * nseq_blk
    gb0 = group0 // g_n

    def x_spec(col):
        return pl.BlockSpec((g_n, SUBLANES, c), lambda i, t: (gb0 + i * n_inner + t, 0, col))

    def state_spec(r):
        if state_layer is None:
            return pl.BlockSpec((nseq_blk, r, c), lambda i, t: (i, 0, 0))
        return pl.BlockSpec((None, nseq_blk, r, c), lambda i, t: (state_layer, i, 0, 0))

    ng = c // GATE_TILE
    in_specs = [x_spec(col0), x_spec(col0 + 1), state_spec(SUBLANES), state_spec(1),
                _layer(l, (CONV_WIDTH, c)), _layer(l, (1, c)),
                _layer(l, (ng, GATE_TILE, GATE_TILE)), _layer(l, (1, c)),
                _layer(l, (ng, GATE_TILE, GATE_TILE)), _layer(l, (1, c)), _layer(l, (1, c))]
    args = [proj3, proj3, prev, h0, cw, cb, wa, ba, wx, bx, lam]
    aliases = {}
    if o_prev is not None:
        in_specs.append(pl.BlockSpec(memory_space=pl.ANY))
        args.append(o_prev)
        aliases = {len(args) - 1: 0}
    out_specs = [pl.BlockSpec((g_n * SUBLANES, c), lambda i, t: (gb0 + i * n_inner + t, 0)),
                 pl.BlockSpec((nseq_blk, 1, c), lambda i, t: (i, 0, 0)),
                 pl.BlockSpec((nseq_blk, SUBLANES, c), lambda i, t: (i, 0, 0))]
    out_shape = [jax.ShapeDtypeStruct((m, c), BF16),
                 jax.ShapeDtypeStruct((nseq, 1, c), F32),
                 jax.ShapeDtypeStruct((nseq, SUBLANES, c), F32)]
    grp = (g_n, SUBLANES, c)
    return pl.pallas_call(
        functools.partial(_lru_body, long_seq=long_seq),
        grid=(n_outer, n_inner),
        in_specs=in_specs, out_specs=out_specs, out_shape=out_shape,
        scratch_shapes=[pltpu.VMEM(grp, F32), pltpu.VMEM(grp, F32), pltpu.VMEM(grp, F32),
                        pltpu.VMEM((SUBLANES, c), F32), pltpu.VMEM((1, c), F32)],
        input_output_aliases=aliases,
        compiler_params=_params("parallel", "arbitrary"),
        name="lru_long" if long_seq else "lru_short",
    )(*args)


def _pool_body(*refs, long_seq, past_len):
    u_ref, hist_ref, pw_ref, ps_ref = refs[:4]
    refs = refs[4:]
    if not long_seq:
        refs = refs[1:]
    o_ref, nh_ref, c_scr = refs
    g_n, _, c = u_ref.shape
    rows = g_n * SUBLANES
    x = u_ref[...]
    t = pl.program_id(1)
    if long_seq:
        @pl.when(t == 0)
        def _():
            c_scr[0] = hist_ref[0, 0:SUBLANES]
            c_scr[1] = hist_ref[0, SUBLANES:2 * SUBLANES]

        ext = jnp.concatenate([c_scr[...], x], axis=0)
        gs = 1
    else:
        hist = hist_ref[...]
        ext = jnp.concatenate([hist[:, 0:SUBLANES], hist[:, SUBLANES:2 * SUBLANES], x], axis=0)
        gs = g_n
    nh_ref[:, 0:SUBLANES] = ext[-2 * gs:-gs]
    nh_ref[:, SUBLANES:2 * SUBLANES] = ext[-gs:]
    if long_seq:
        c_scr[0] = ext[g_n]
        c_scr[1] = ext[g_n + 1]

    def prev(a):
        return jnp.concatenate([a[:gs], a[:-gs]], axis=0)

    n_grp = len(POOL_WINDOWS)
    pg = c // n_grp
    i0 = lax.broadcasted_iota(jnp.int32, (g_n, SUBLANES, pg), 0)
    i1 = lax.broadcasted_iota(jnp.int32, (g_n, SUBLANES, pg), 1)
    t_abs = (t * g_n + i0) * SUBLANES + i1 if long_seq else i1
    outs = []
    for gi, w in enumerate(POOL_WINDOWS):
        eg = ext[:, :, gi * pg:(gi + 1) * pg]
        s = eg
        k = 1
        while k < w:
            s = s + (_shift_rows(s, prev(s), k) if k < SUBLANES else prev(s))
            k *= 2
        cnt = jnp.minimum(t_abs + (1 + past_len), w).astype(F32)
        d = s[2 * gs:] / cnt - eg[2 * gs:]
        d2 = d.reshape(rows, pg).astype(BF16)
        outs.append(jnp.dot(d2, pw_ref[gi], preferred_element_type=F32))
    y = jnp.concatenate(outs, axis=1) * ps_ref[...]
    o_ref[...] = y.astype(o_ref.dtype)


def _pool_call(proj3, col, o_prev, hist, pool_w, pool_scale, l, *, long_seq, past_len, n_outer, n_inner, g_n,
               group0, state_layer):
    c = pool_scale.shape[-1]
    m = proj3.shape[0] * SUBLANES
    nseq_blk = 1 if long_seq else g_n
    nseq = n_outer * nseq_blk
    gb0 = group0 // g_n
    pg = c // len(POOL_WINDOWS)
    hrows = 2 * SUBLANES
    if state_layer is None:
        h_spec = pl.BlockSpec((nseq_blk, hrows, c), lambda i, t: (i, 0, 0))
    else:
        h_spec = pl.BlockSpec((None, nseq_blk, hrows, c), lambda i, t: (state_layer, i, 0, 0))
    in_specs = [pl.BlockSpec((g_n, SUBLANES, c), lambda i, t: (gb0 + i * n_inner + t, 0, col)), h_spec,
                _layer(l, (len(POOL_WINDOWS), pg, pg)), _layer(l, (1, c))]
    args = [proj3, hist, pool_w, pool_scale]
    aliases = {}
    if o_prev is not None:
        in_specs.append(pl.BlockSpec(memory_space=pl.ANY))
        args.append(o_prev)
        aliases = {len(args) - 1: 0}
    return pl.pallas_call(
        functools.partial(_pool_body, long_seq=long_seq, past_len=past_len),
        grid=(n_outer, n_inner),
        in_specs=in_specs,
        out_specs=[pl.BlockSpec((g_n * SUBLANES, c), lambda i, t: (gb0 + i * n_inner + t, 0)),
                   pl.BlockSpec((nseq_blk, hrows, c), lambda i, t: (i, 0, 0))],
        out_shape=[jax.ShapeDtypeStruct((m, c), BF16), jax.ShapeDtypeStruct((nseq, hrows, c), F32)],
        scratch_shapes=[pltpu.VMEM((2, SUBLANES, c), F32)],
        input_output_aliases=aliases,
        compiler_params=_params("parallel", "arbitrary"),
        name="pool_long" if long_seq else "pool_short",
    )(*args)


SOLVE_PASSES = 1


def _mm(a, b):
    a_hi, b_hi = a.astype(BF16), b.astype(BF16)
    out = jnp.dot(a_hi, b_hi, preferred_element_type=F32)
    if SOLVE_PASSES == 3:
        a_lo = (a - a_hi.astype(F32)).astype(BF16)
        b_lo = (b - b_hi.astype(F32)).astype(BF16)
        out = out + jnp.dot(a_hi, b_lo, preferred_element_type=F32) + jnp.dot(a_lo, b_hi, preferred_element_type=F32)
    return out


def _delta_body(*refs, long_seq, tseq, n_aliased):
    (q_ref, k_ref, v_ref, z_ref, ba_ref, prev_ref, s0_ref, cw_ref, alog_ref, dtb_ref, nw_ref) = refs[:11]
    o_ref, s_ref, nb_ref, cx_scr = refs[11 + n_aliased:]
    g_n, _, c = q_ref.shape
    rows = g_n * SUBLANES
    nseq_blk = rows // tseq
    heads = c // HEAD_DIM
    first = pl.program_id(1) == 0

    if long_seq:
        @pl.when(first)
        def _():
            cx_scr[...] = prev_ref[0]
            s_ref[...] = s0_ref[...]

    cw = cw_ref[...]
    streams = []
    for idx, x_ref in enumerate((q_ref, k_ref, v_ref)):
        x = x_ref[...]
        lo, hi = idx * c, (idx + 1) * c
        if long_seq:
            prev = _prev_groups(x, cx_scr[:, lo:hi])
            cx_scr[:, lo:hi] = x[g_n - 1]
            nb_ref[0, :, lo:hi] = x[g_n - 1]
        else:
            prev = prev_ref[:, :, lo:hi]
            nb_ref[:, :, lo:hi] = x
        y = _causal_conv(x, prev, cw[:, lo:hi])
        streams.append((y * jax.nn.sigmoid(y)).reshape(rows, c))
    q_all, k_all, v_all = streams
    z_all = z_ref[...].reshape(rows, c)

    ba = ba_ref[...]
    beta_all = jax.nn.sigmoid(ba[:, 0:HEAD_DIM])
    g_all = -jnp.exp(alog_ref[...]) * _softplus(ba[:, HEAD_DIM:2 * HEAD_DIM] + dtb_ref[...])

    ri = lax.broadcasted_iota(jnp.int32, (rows, rows), 0)
    ci = lax.broadcasted_iota(jnp.int32, (rows, rows), 1)
    same = (ri // tseq) == (ci // tseq)
    incl = same & (ci <= ri)
    strict = same & (ci < ri)
    gamma_all = jnp.dot(incl.astype(F32), g_all, precision=_HI, preferred_element_type=F32)
    gtot_all = jnp.dot(same.astype(F32), g_all, precision=_HI, preferred_element_type=F32)
    gamma_t = gamma_all.T
    eye = (ri == ci).astype(F32)

    s_src = s_ref if long_seq else s0_ref
    hrange = range(heads)
    q_l, k_l, v_l, beta_l, gcol_l, gtot_l, decay_l, qb_l, kb_l, a_l = ([] for _ in range(10))
    for h in hrange:
        hs = slice(h * HEAD_DIM, (h + 1) * HEAD_DIM)
        q_h, k_h = q_all[:, hs], k_all[:, hs]
        q_h = q_h * lax.rsqrt(jnp.sum(q_h * q_h, axis=-1, keepdims=True) + EPS) * (HEAD_DIM ** -0.5)
        k_h = k_h * lax.rsqrt(jnp.sum(k_h * k_h, axis=-1, keepdims=True) + EPS)
        gcol = gamma_all[:, h:h + 1]
        decay = jnp.where(incl, jnp.exp(gcol - gamma_t[h:h + 1, :]), 0.0)
        qb, kb = q_h.astype(BF16), k_h.astype(BF16)
        beta = beta_all[:, h:h + 1]
        kk = lax.dot_general(kb, kb, _NT, preferred_element_type=F32)
        q_l.append(q_h)
        k_l.append(k_h)
        v_l.append(v_all[:, hs])
        beta_l.append(beta)
        gcol_l.append(gcol)
        gtot_l.append(gtot_all[:, h:h + 1])
        decay_l.append(decay)
        qb_l.append(qb)
        kb_l.append(kb)
        a_l.append(jnp.where(strict, beta * decay * kk, 0.0))

    t_l = [eye - jnp.where(ri // 2 == ci // 2, a, 0.0) for a in a_l]
    s = 2
    while s < tseq:
        off_mask = (ri // (2 * s) == ci // (2 * s)) & (ri % (2 * s) >= s) & (ci % (2 * s) < s)
        t_off = [_mm(t, jnp.where(off_mask, a, 0.0)) for t, a in zip(t_l, a_l)]
        t_l = [t - _mm(to, t) for t, to in zip(t_l, t_off)]
        s *= 2

    eg_l = [jnp.exp(g) for g in gcol_l]
    sol_l = [_mm(t, jnp.concatenate([b * v, (b * e) * k], axis=1))
             for t, b, v, e, k in zip(t_l, beta_l, v_l, eg_l, k_l)]
    qk_l = [(lax.dot_general(qb, kb, _NT, preferred_element_type=F32) * d).astype(BF16)
            for qb, kb, d in zip(qb_l, kb_l, decay_l)]
    qg_l = [(q * e).astype(BF16) for q, e in zip(q_l, eg_l)]
    kd_l = [(k * jnp.exp(gt - g)).astype(BF16) for k, gt, g in zip(k_l, gtot_l, gcol_l)]

    w_l, o_l, s_old = [], [], []
    for h in hrange:
        u_c, wkb = sol_l[h][:, :HEAD_DIM], sol_l[h][:, HEAD_DIM:].astype(BF16)
        w_parts, o_parts = [], []
        for sq in range(nseq_blk):
            rs = slice(sq * tseq, (sq + 1) * tseq)
            s_mat = s_src[sq, h]
            sb = s_mat.astype(BF16)
            s_old.append(s_mat)
            w_parts.append(u_c[rs] - jnp.dot(wkb[rs], sb, preferred_element_type=F32))
            o_parts.append(jnp.dot(qg_l[h][rs], sb, preferred_element_type=F32))
        w_l.append((jnp.concatenate(w_parts, axis=0) if nseq_blk > 1 else w_parts[0]).astype(BF16))
        o_l.append(jnp.concatenate(o_parts, axis=0) if nseq_blk > 1 else o_parts[0])
    for h in hrange:
        hs = slice(h * HEAD_DIM, (h + 1) * HEAD_DIM)
        for sq in range(nseq_blk):
            rs = slice(sq * tseq, (sq + 1) * tseq)
            g_last = jnp.exp(gtot_l[h][sq * tseq:sq * tseq + 1])
            s_ref[sq, h] = g_last * s_old[h * nseq_blk + sq] + lax.dot_general(
                kd_l[h][rs], w_l[h][rs], _TN, preferred_element_type=F32)
        o = o_l[h] + jnp.dot(qk_l[h], w_l[h], preferred_element_type=F32)
        o = o * lax.rsqrt(jnp.mean(o * o, axis=-1, keepdims=True) + EPS) * nw_ref[...]
        z_h = z_all[:, hs]
        o_ref[:, hs] = (o * (z_h * jax.nn.sigmoid(z_h))).astype(o_ref.dtype)


def _delta_call(proj3, col0, ba, o_prev, prev, s0, conv_w, alog, dtb, norm_w, l, *, long_seq, tseq, n_outer, n_inner,
                group0, state_layer, s_stack=None):
    c = conv_w.shape[-1] // 3
    heads = c // HEAD_DIM
    g_n = DN_CHUNK // SUBLANES
    m = proj3.shape[0] * SUBLANES
    nseq_blk = 1 if long_seq else DN_CHUNK // tseq
    nseq = n_outer * nseq_blk
    gb0 = group0 // g_n

    def x_spec(col):
        return pl.BlockSpec((g_n, SUBLANES, c), lambda i, t: (gb0 + i * n_inner + t, 0, col))

    if state_layer is None:
        p_spec = pl.BlockSpec((nseq_blk, SUBLANES, 3 * c), lambda i, t: (i, 0, 0))
        s_spec = pl.BlockSpec((nseq_blk, heads, HEAD_DIM, HEAD_DIM), lambda i, t: (i, 0, 0, 0))
    else:
        p_spec = pl.BlockSpec((None, nseq_blk, SUBLANES, 3 * c), lambda i, t: (state_layer, i, 0, 0))
        s_spec = pl.BlockSpec((None, nseq_blk, heads, HEAD_DIM, HEAD_DIM), lambda i, t: (state_layer, i, 0, 0, 0))
    in_specs = [x_spec(col0), x_spec(col0 + 1), x_spec(col0 + 2), x_spec(col0 + 3),
                pl.BlockSpec((DN_CHUNK, 2 * HEAD_DIM), lambda i, t: (gb0 + i * n_inner + t, 0)),
                p_spec, s_spec,
                _layer(l, (CONV_WIDTH, 3 * c)), _layer(l, (1, HEAD_DIM)), _layer(l, (1, HEAD_DIM)),
                _layer(l, (1, HEAD_DIM))]
    args = [proj3, proj3, proj3, proj3, ba, prev, s0, conv_w, alog, dtb, norm_w]
    aliases = {}
    if o_prev is not None:
        in_specs.append(pl.BlockSpec(memory_space=pl.ANY))
        args.append(o_prev)
        aliases = {len(args) - 1: 0}
    s_block = (nseq_blk, heads, HEAD_DIM, HEAD_DIM)
    if s_stack is None:
        s_out_spec = pl.BlockSpec(s_block, lambda i, t: (i, 0, 0, 0))
        s_out_shape = jax.ShapeDtypeStruct((nseq,) + s_block[1:], F32)
    else:
        depth, stacked = s_stack
        s_out_spec = pl.BlockSpec((None,) + s_block, lambda i, t: (l, i, 0, 0, 0))
        s_out_shape = jax.ShapeDtypeStruct((depth, nseq) + s_block[1:], F32)
        if stacked is not None:
            in_specs.append(pl.BlockSpec(memory_space=pl.ANY))
            args.append(stacked)
            aliases[len(args) - 1] = 1
    return pl.pallas_call(
        functools.partial(_delta_body, long_seq=long_seq, tseq=tseq, n_aliased=len(aliases)),
        grid=(n_outer, n_inner),
        in_specs=in_specs,
        out_specs=[pl.BlockSpec((DN_CHUNK, c), lambda i, t: (gb0 + i * n_inner + t, 0)),
                   s_out_spec,
                   pl.BlockSpec((nseq_blk, SUBLANES, 3 * c), lambda i, t: (i, 0, 0))],
        out_shape=[jax.ShapeDtypeStruct((m, c), BF16),
                   s_out_shape,
                   jax.ShapeDtypeStruct((nseq, SUBLANES, 3 * c), F32)],
        scratch_shapes=[pltpu.VMEM((SUBLANES, 3 * c), F32)],
        input_output_aliases=aliases,
        compiler_params=_params("parallel", "arbitrary"),
        name="delta_long" if long_seq else "delta_short",
    )(*args)


def _pad_front(a, rows):
    pad = [(0, 0)] * a.ndim
    pad[-2] = (rows - a.shape[-2], 0)
    return jnp.pad(a, pad)


def _block_diag_tiles(w):
    depth, nb, b, _ = w.shape
    per = GATE_TILE // b
    w5 = w.reshape(depth, nb // per, per, b, b)
    eye = jnp.eye(per, dtype=w.dtype)
    t = jnp.einsum('lijcd,jk->lijckd', w5, eye)
    return t.reshape(depth, nb // per, GATE_TILE, GATE_TILE).astype(BF16)


def kernel(x_prompt, x_sample, state_lru_h, state_lru_conv, state_dn_s, state_dn_conv, state_pool, norm_mix, w_in, lru_conv_w, lru_conv_b, lru_gate_a_w, lru_gate_a_b, lru_gate_x_w, lru_gate_x_b, lru_lambda, dn_conv_w, dn_a_log, dn_dt_bias, dn_norm_w, pool_w, pool_scale, w_br_lru, w_br_dn, w_br_pool, w_out, norm_mlp, w_up, w_down, norm_final):
    bp, tp, d = x_prompt.shape
    bs, ts, _ = x_sample.shape
    depth = w_in.shape[0]
    half = d // 2
    heads = half // HEAD_DIM
    mp, ms = bp * tp, bs * ts
    m = mp + ms
    assert ts == SUBLANES and tp % DN_CHUNK == 0 and bs % (DN_CHUNK // ts) == 0
    assert lru_gate_a_w.shape[-1] == LRU_BLOCK and half % GATE_TILE == 0
    assert state_pool.shape[-2] == POOL_BUF

    n_pre = 6 * half
    ba0 = n_pre
    pool0 = ba0 + 2 * heads
    gates0 = pool0 + half
    w_in_t = jnp.swapaxes(w_in, 1, 2)
    n_gate_tiles, n_stream_tiles = 3 * d // half, n_pre // half
    n_proj_tiles = n_gate_tiles + n_stream_tiles + 1

    assert gates0 % SUBLANES == 0 and pool0 % SUBLANES == 0

    def w_in_row0(j):
        r = jnp.where(j < n_gate_tiles, gates0 + j * half,
                      jnp.where(j < n_gate_tiles + n_stream_tiles, (j - n_gate_tiles) * half, pool0))
        return pl.multiple_of(r, SUBLANES)

    lane_fill = ((0, 0), (0, 0), (0, HEAD_DIM - heads))
    w_ba = jnp.concatenate([jnp.pad(w_in[:, :, ba0:ba0 + heads], lane_fill),
                            jnp.pad(w_in[:, :, ba0 + heads:ba0 + 2 * heads], lane_fill)], axis=2).astype(BF16)
    col0 = n_gate_tiles
    gate_col0 = 0

    def row(a):
        return a.reshape(depth, 1, a.shape[-1])

    def lane_pad(a):
        return jnp.pad(a, ((0, 0), (0, HEAD_DIM - a.shape[-1]))).reshape(depth, 1, HEAD_DIM)

    lru_w = (lru_conv_w, row(lru_conv_b), _block_diag_tiles(lru_gate_a_w), row(lru_gate_a_b),
             _block_diag_tiles(lru_gate_x_w), row(lru_gate_x_b), row(lru_lambda))
    alog, dtb, dn_nw = lane_pad(dn_a_log), lane_pad(dn_dt_bias), row(dn_norm_w)
    pool_wb, pool_sc = pool_w.astype(BF16), row(pool_scale)
    w_br_lru_b, w_br_dn_b, w_br_pool_b = w_br_lru.astype(BF16), w_br_dn.astype(BF16), w_br_pool.astype(BF16)
    w_out_b, w_down_b = w_out.astype(BF16), w_down.astype(BF16)
    g_mix, g_mlp = row(norm_mix), row(norm_mlp)

    s_lru_h0 = state_lru_h.reshape(depth, bs, 1, half)
    s_lru_prev = _pad_front(state_lru_conv, SUBLANES)
    s_dn_prev = _pad_front(state_dn_conv, SUBLANES)
    s_pool_hist = _pad_front(state_pool, 2 * SUBLANES)
    p_lru_h0 = jnp.zeros((bp, 1, half), F32)
    p_lru_prev = jnp.zeros((bp, SUBLANES, half), F32)
    p_dn_prev = jnp.zeros((bp, SUBLANES, 3 * half), F32)
    p_dn_s0 = jnp.zeros((bp, heads, HEAD_DIM, HEAD_DIM), F32)
    p_pool_hist = jnp.zeros((bp, 2 * SUBLANES, half), F32)

    x = jnp.concatenate([x_prompt.reshape(mp, d), x_sample.reshape(ms, d)], axis=0)

    tm = 512 if (mp % 512 == 0 and ms % 512 == 0) else 256
    tm_big = 1024 if (mp % 1024 == 0 and ms % 1024 == 0) else tm
    lru_g = 32
    assert (tp // SUBLANES) % lru_g == 0 and bs % lru_g == 0 and m % tm == 0 and mp % tm == 0
    n_t = tp // SUBLANES // lru_g
    long_kw = dict(long_seq=True, n_outer=bp, n_inner=n_t, g_n=lru_g, group0=0, state_layer=None)
    short_kw = dict(long_seq=False, n_outer=bs // lru_g, n_inner=1, g_n=lru_g, group0=mp // SUBLANES)
    seq_per_blk = DN_CHUNK // ts

    p_states, s_states = [], []
    s_dn_stack = None
    xn, ba = _norm_cast(x, g_mix, w_ba, 0, tm=tm)
    for l in range(depth):
        w_in_spec = pl.BlockSpec((None, pl.Element(half), pl.Element(d)), lambda j, i, l=l: (l, w_in_row0(j), 0))
        proj = _stream_matmul(xn, w_in_t, w_in_spec, (half, d), n_proj_tiles, tm=tm_big, tn=half, out_dtype=F32,
                              w_rows_are_outputs=True)
        proj3 = proj.reshape(m // SUBLANES, SUBLANES, proj.shape[-1])

        o_lru, p_h, p_cb = _lru_call(proj3, col0, None, p_lru_prev, p_lru_h0, lru_w, l, **long_kw)
        o_lru, s_h, s_cb = _lru_call(proj3, col0, o_lru, s_lru_prev, s_lru_h0, lru_w, l, state_layer=l, **short_kw)

        o_dn, p_s, p_db = _delta_call(proj3, col0 + 2, ba, None, p_dn_prev, p_dn_s0, dn_conv_w, alog, dtb, dn_nw, l,
                                      long_seq=True, tseq=DN_CHUNK, n_outer=bp, n_inner=tp // DN_CHUNK, group0=0,
                                      state_layer=None)
        o_dn, s_dn_stack, s_db = _delta_call(proj3, col0 + 2, ba, o_dn, s_dn_prev, state_dn_s, dn_conv_w, alog, dtb,
                                             dn_nw, l, long_seq=False, tseq=ts, n_outer=bs // seq_per_blk, n_inner=1,
                                             group0=mp // SUBLANES, state_layer=l, s_stack=(depth, s_dn_stack))

        o_pool, p_ph = _pool_call(proj3, col0 + 6, None, p_pool_hist, pool_wb, pool_sc, l, past_len=0, **long_kw)
        o_pool, s_ph = _pool_call(proj3, col0 + 6, o_pool, s_pool_hist, pool_wb, pool_sc, l, past_len=PAST_LEN,
                                  state_layer=l, **short_kw)

        merged = _merge(o_lru, o_dn, o_pool, proj, gate_col0, w_br_lru_b, w_br_dn_b, w_br_pool_b, l, tm=256, tn=d)
        x, xn = _matmul_res(merged, w_out_b, l, x, tm=tm, tk=d, gain=g_mlp, gain_layer=l)
        w_up_spec = pl.BlockSpec((None, d, half), lambda j, i, l=l: (l, 0, j))
        hm = _stream_matmul(xn, w_up, w_up_spec, (d, half), w_up.shape[-1] // half, tm=tm_big, tn=half,
                            out_dtype=BF16, w_rows_are_outputs=False, relu2=True)
        if l + 1 < depth:
            x, xn, ba = _matmul_res(hm, w_down_b, l, x, tm=tm, tk=d, gain=g_mix, gain_layer=l + 1, w_small=w_ba)
        else:
            x, = _matmul_res(hm, w_down_b, l, x, tm=tm, tk=d)
        p_states.append((p_h, p_cb, p_s, p_db, p_ph))
        s_states.append((s_h, s_cb, None, s_db, s_ph))

    y_prompt = _final_norm(x, norm_final.reshape(1, d), 0, mp, tm=tm).reshape(bp, tp, d)
    y_sample = _final_norm(x, norm_final.reshape(1, d), mp, ms, tm=tm).reshape(bs, ts, d)

    def collect(states, nseq, sm=None):
        tail = CONV_WIDTH - 1
        h = jnp.stack([s[0] for s in states]).reshape(depth, nseq, half)
        cb = jnp.stack([s[1] for s in states])[:, :, SUBLANES - tail:]
        if sm is None:
            sm = jnp.stack([s[2] for s in states])
        db = jnp.stack([s[3] for s in states])[:, :, SUBLANES - tail:]
        ph = jnp.stack([s[4] for s in states])[:, :, 2 * SUBLANES - POOL_BUF:]
        return h, cb, sm, db, ph

    return (y_prompt, y_sample) + collect(p_states, bp) + collect(s_states, bs, s_dn_stack)
```

```python
import functools

import jax
import jax.numpy as jnp
from jax import lax
from jax.experimental import pallas as pl
from jax.experimental.pallas import tpu as pltpu

F32 = jnp.float32
BF16 = jnp.bfloat16

EPS = 1e-6
LRU_C = 8.0
CONV_WIDTH = 4
HEAD_DIM = 128
LRU_BLOCK = 64
POOL_WINDOWS = (2, 4, 8, 16)
POOL_BUF = 15
DN_CHUNK = 64
PAST_LEN = 16384
SUBLANES = 8
GATE_TILE = 256
VMEM_LIMIT = 56 * 1024 * 1024

_NT = (((1,), (1,)), ((), ()))
_TN = (((0,), (0,)), ((), ()))
_HI = lax.Precision.HIGHEST


def _params(*sem):
    return pltpu.CompilerParams(dimension_semantics=sem, vmem_limit_bytes=VMEM_LIMIT)


def _full(shape):
    return pl.BlockSpec(shape, lambda *_: (0,) * len(shape))


def _layer(l, shape):
    return pl.BlockSpec((None,) + shape, lambda *_: (l,) + (0,) * len(shape))


def _rms(x, gain):
    return x * lax.rsqrt(jnp.mean(x * x, axis=-1, keepdims=True) + EPS) * gain


def _norm_cast_body(x_ref, g_ref, ws_ref, xn_ref, os_ref):
    xn = _rms(x_ref[...], g_ref[...]).astype(BF16)
    xn_ref[...] = xn
    os_ref[...] = lax.dot_general(xn, ws_ref[...], _NT, preferred_element_type=F32)


def _norm_cast(x, gain, w_small, l, *, tm):
    m, d = x.shape
    ns = w_small.shape[-2]
    return pl.pallas_call(
        _norm_cast_body,
        grid=(m // tm,),
        in_specs=[pl.BlockSpec((tm, d), lambda i: (i, 0)), _layer(l, (1, d)), _layer(l, (ns, d))],
        out_specs=[pl.BlockSpec((tm, d), lambda i: (i, 0)), pl.BlockSpec((tm, ns), lambda i: (i, 0))],
        out_shape=[jax.ShapeDtypeStruct((m, d), BF16), jax.ShapeDtypeStruct((m, ns), F32)],
        compiler_params=_params("parallel"),
        name="norm_cast",
    )(x, gain, w_small)


def _stream_matmul_body(x_ref, w_ref, o_ref, wb_ref, *, w_rows_are_outputs, act):
    @pl.when(pl.program_id(1) == 0)
    def _():
        wb_ref[...] = w_ref[...].astype(BF16)

    dims = _NT if w_rows_are_outputs else (((1,), (0,)), ((), ()))
    acc = lax.dot_general(x_ref[...], wb_ref[...], dims, preferred_element_type=F32)
    if act == "relu2":
        acc = jnp.square(jnp.maximum(acc, 0.0))
    elif act == "sigmoid":
        acc = jax.nn.sigmoid(acc)
    o_ref[...] = acc.astype(o_ref.dtype)


def _stream_matmul(x, w, w_spec, w_block, n_tiles, *, tm, tn, out_dtype, w_rows_are_outputs, act=None):
    m, d = x.shape
    return pl.pallas_call(
        functools.partial(_stream_matmul_body, w_rows_are_outputs=w_rows_are_outputs, act=act),
        grid=(n_tiles, m // tm),
        in_specs=[pl.BlockSpec((tm, d), lambda j, i: (i, 0)), w_spec],
        out_specs=pl.BlockSpec((tm, tn), lambda j, i: (i, j)),
        out_shape=jax.ShapeDtypeStruct((m, n_tiles * tn), out_dtype),
        scratch_shapes=[pltpu.VMEM(w_block, BF16)],
        compiler_params=_params("parallel", "arbitrary"),
        name="stream_matmul_t" if w_rows_are_outputs else "stream_matmul",
    )(x, w)


def _matmul_res_body(*refs, n_k, emit_norm, emit_small):
    a_ref, w_ref, r_ref = refs[:3]
    refs = refs[3:]
    if emit_norm:
        g_ref = refs[0]
        refs = refs[1:]
    if emit_small:
        ws_ref = refs[0]
        refs = refs[1:]
    o_ref = refs[0]
    k = pl.program_id(1)

    @pl.when(k == 0)
    def _():
        o_ref[...] = r_ref[...]

    o_ref[...] += jnp.dot(a_ref[...], w_ref[...], preferred_element_type=F32)

    if emit_norm:
        @pl.when(k == n_k - 1)
        def _():
            xn = _rms(o_ref[...], g_ref[...]).astype(BF16)
            refs[1][...] = xn
            if emit_small:
                refs[2][...] = lax.dot_general(xn, ws_ref[...], _NT, preferred_element_type=F32)


def _matmul_res(a, w, l, res, *, tm, tk, gain=None, gain_layer=0, w_small=None):
    m, k = a.shape
    d = w.shape[-1]
    n_k = k // tk
    emit_norm = gain is not None
    emit_small = w_small is not None
    in_specs = [pl.BlockSpec((tm, tk), lambda i, kk: (i, kk)),
                pl.BlockSpec((None, tk, d), lambda i, kk: (l, kk, 0)),
                pl.BlockSpec((tm, d), lambda i, kk: (i, 0))]
    out_specs = [pl.BlockSpec((tm, d), lambda i, kk: (i, 0))]
    out_shape = [jax.ShapeDtypeStruct((m, d), F32)]
    args = [a, w, res]
    if emit_norm:
        in_specs.append(_layer(gain_layer, (1, d)))
        args.append(gain)
        out_specs.append(pl.BlockSpec((tm, d), lambda i, kk: (i, 0)))
        out_shape.append(jax.ShapeDtypeStruct((m, d), BF16))
    if emit_small:
        ns = w_small.shape[-2]
        in_specs.append(_layer(gain_layer, (ns, d)))
        args.append(w_small)
        out_specs.append(pl.BlockSpec((tm, ns), lambda i, kk: (i, 0)))
        out_shape.append(jax.ShapeDtypeStruct((m, ns), F32))
    return pl.pallas_call(
        functools.partial(_matmul_res_body, n_k=n_k, emit_norm=emit_norm, emit_small=emit_small),
        grid=(m // tm, n_k),
        in_specs=in_specs, out_specs=out_specs, out_shape=out_shape,
        compiler_params=_params("parallel", "arbitrary"),
        name="matmul_res",
    )(*args)


def _merge_body(ol_ref, od_ref, op_ref, gl_ref, gd_ref, gp_ref, wl_ref, wd_ref, wp_ref, o_ref):
    def branch(o_ref_, g_ref_, w_ref_):
        return g_ref_[...].astype(F32) * jnp.dot(o_ref_[...], w_ref_[...], preferred_element_type=F32)

    m = branch(ol_ref, gl_ref, wl_ref) + branch(od_ref, gd_ref, wd_ref) + branch(op_ref, gp_ref, wp_ref)
    o_ref[...] = m.astype(o_ref.dtype)


def _merge(o_lru, o_dn, o_pool, gates, w_lru, w_dn, w_pool, l, *, tm, tn):
    m, half = o_lru.shape
    d = w_lru.shape[-1]
    nj = d // tn
    g0 = 0

    def o_spec():
        return pl.BlockSpec((tm, half), lambda i, j: (i, 0))

    def g_spec(b):
        return pl.BlockSpec((tm, tn), lambda i, j: (i, g0 + b * nj + j))

    def w_spec():
        return pl.BlockSpec((None, half, tn), lambda i, j: (l, 0, j))

    return pl.pallas_call(
        _merge_body,
        grid=(m // tm, nj),
        in_specs=[o_spec(), o_spec(), o_spec(), g_spec(0), g_spec(1), g_spec(2), w_spec(), w_spec(), w_spec()],
        out_specs=pl.BlockSpec((tm, tn), lambda i, j: (i, j)),
        out_shape=jax.ShapeDtypeStruct((m, d), BF16),
        compiler_params=_params("parallel", "arbitrary"),
        name="merge",
    )(o_lru, o_dn, o_pool, gates, gates, gates, w_lru, w_dn, w_pool)


def _rmsnorm_body(x_ref, g_ref, o_ref):
    x = x_ref[...]
    inv = lax.rsqrt(jnp.mean(x * x, axis=-1, keepdims=True) + EPS)
    o_ref[...] = x * inv * g_ref[...]


def _final_norm(x, gain, row0, rows, *, tm):
    d = x.shape[-1]
    b0 = row0 // tm
    return pl.pallas_call(
        _rmsnorm_body,
        grid=(rows // tm,),
        in_specs=[pl.BlockSpec((tm, d), lambda i: (b0 + i, 0)), _full((1, d))],
        out_specs=pl.BlockSpec((tm, d), lambda i: (i, 0)),
        out_shape=jax.ShapeDtypeStruct((rows, d), F32),
        compiler_params=_params("parallel"),
        name="final_norm",
    )(x, gain)


def _softplus(x):
    return jnp.maximum(x, 0.0) + jnp.log1p(jnp.exp(-jnp.abs(x)))


def _shift_rows(cur, prev, k, chained=False):
    ax = cur.ndim - 2
    row = lax.broadcasted_iota(jnp.int32, cur.shape, ax)
    if chained:
        rolled = pltpu.roll(jnp.concatenate([prev[None], cur], axis=0), k, ax)
        return jnp.where(row >= k, rolled[1:], rolled[:-1])
    return jnp.where(row >= k, pltpu.roll(cur, k, ax), pltpu.roll(prev, k, ax))


def _causal_conv(x, prev, cw, chained):
    y = cw[CONV_WIDTH - 1:CONV_WIDTH] * x
    for k in range(1, CONV_WIDTH):
        y = y + cw[CONV_WIDTH - 1 - k:CONV_WIDTH - k] * _shift_rows(x, prev, k, chained)
    return y


def _lru_body(*refs, long_seq):
    (x_ref, gate_ref, prev_ref, h0_ref, cw_ref, cb_ref, wa_ref, ba_ref, wx_ref, bx_ref, lam_ref) = refs[:11]
    refs = refs[11:]
    if not long_seq:
        refs = refs[1:]
    o_ref, hl_ref, nb_ref, a_scr, b_scr, h_scr, cx_scr, ch_scr = refs
    g_n, _, c = x_ref.shape
    rows = g_n * SUBLANES
    x = x_ref[...]
    if long_seq:
        @pl.when(pl.program_id(1) == 0)
        def _():
            cx_scr[...] = prev_ref[0]
            ch_scr[...] = h0_ref[0]

        prev = cx_scr[...]
    else:
        prev = prev_ref[...]
    xc = _causal_conv(x, prev, cw_ref[...], long_seq) + cb_ref[...]
    if long_seq:
        cx_scr[...] = x[g_n - 1]
        nb_ref[0] = x[g_n - 1]
    else:
        nb_ref[...] = x

    xc2 = xc.reshape(rows, c)
    xb = xc2.astype(BF16)
    nblk = c // GATE_TILE

    def gate(w_ref, b_ref):
        parts = [jnp.dot(xb[:, i * GATE_TILE:(i + 1) * GATE_TILE], w_ref[i], preferred_element_type=F32)
                 for i in range(nblk)]
        return jax.nn.sigmoid(jnp.concatenate(parts, axis=1) + b_ref[...])

    r = gate(wa_ref, ba_ref)
    i_g = gate(wx_ref, bx_ref)
    log_a = (-LRU_C) * r * _softplus(-lam_ref[...])
    a = jnp.exp(log_a)
    one_minus_a2 = -jnp.tanh(log_a) * (a * a + 1.0)
    b = jnp.sqrt(one_minus_a2) * (i_g * xc2)

    a3 = a.reshape(g_n, SUBLANES, c)
    b3 = b.reshape(g_n, SUBLANES, c)
    row = lax.broadcasted_iota(jnp.int32, a3.shape, 1)
    s = 1
    while s < SUBLANES:
        m = row >= s
        a_sh = pltpu.roll(a3, s, 1)
        b_sh = pltpu.roll(b3, s, 1)
        b3 = jnp.where(m, a3 * b_sh + b3, b3)
        a3 = jnp.where(m, a3 * a_sh, a3)
        s *= 2
    a_scr[...] = a3
    b_scr[...] = b3

    def chain(g, h_in):
        if not long_seq:
            h_in = h0_ref[g]
        hg = a_scr[g] * h_in + b_scr[g]
        h_scr[g] = hg
        h_out = hg[SUBLANES - 1:SUBLANES]
        if not long_seq:
            hl_ref[g] = h_out
        return h_out

    h_init = ch_scr[...] if long_seq else jnp.zeros((1, c), F32)
    h_fin = lax.fori_loop(0, g_n, chain, h_init)
    if long_seq:
        ch_scr[...] = h_fin
        hl_ref[0] = h_fin
    h = h_scr[...].reshape(rows, c)
    gl = gate_ref[...].reshape(rows, c)
    o_ref[...] = (h * jax.nn.gelu(gl)).astype(o_ref.dtype)


def _lru_call(proj3, col0, o_prev, prev, h0, weights, l, *, long_seq, n_outer, n_inner, g_n, group0, state_layer):
    cw, cb, wa, ba, wx, bx, lam = weights
    c = cw.shape[-1]
    m = proj3.shape[0] * SUBLANES
    nseq_blk = 1 if long_seq else g_n
    nseq = n_outer * nseq_blk
    gb0 = group0 // g_n

    def x_spec(col):
        return pl.BlockSpec((g_n, SUBLANES, c), lambda i, t: (gb0 + i * n_inner + t, 0, col))

    def state_spec(r):
        if state_layer is None:
            return pl.BlockSpec((nseq_blk, r, c), lambda i, t: (i, 0, 0))
        return pl.BlockSpec((None, nseq_blk, r, c), lambda i, t: (state_layer, i, 0, 0))

    ng = c // GATE_TILE
    in_specs = [x_spec(col0), x_spec(col0 + 1), state_spec(SUBLANES), state_spec(1),
                _layer(l, (CONV_WIDTH, c)), _layer(l, (1, c)),
                _layer(l, (ng, GATE_TILE, GATE_TILE)), _layer(l, (1, c)),
                _layer(l, (ng, GATE_TILE, GATE_TILE)), _layer(l, (1, c)), _layer(l, (1, c))]
    args = [proj3, proj3, prev, h0, cw, cb, wa, ba, wx, bx, lam]
    aliases = {}
    if o_prev is not None:
        in_specs.append(pl.BlockSpec(memory_space=pl.ANY))
        args.append(o_prev)
        aliases = {len(args) - 1: 0}
    out_specs = [pl.BlockSpec((g_n * SUBLANES, c), lambda i, t: (gb0 + i * n_inner + t, 0)),
                 pl.BlockSpec((nseq_blk, 1, c), lambda i, t: (i, 0, 0)),
                 pl.BlockSpec((nseq_blk, SUBLANES, c), lambda i, t: (i, 0, 0))]
    out_shape = [jax.ShapeDtypeStruct((m, c), BF16),
                 jax.ShapeDtypeStruct((nseq, 1, c), F32),
                 jax.ShapeDtypeStruct((nseq, SUBLANES, c), F32)]
    grp = (g_n, SUBLANES, c)
    return pl.pallas_call(
        functools.partial(_lru_body, long_seq=long_seq),
        grid=(n_outer, n_inner),
        in_specs=in_specs, out_specs=out_specs, out_shape=out_shape,
        scratch_shapes=[pltpu.VMEM(grp, F32), pltpu.VMEM(grp, F32), pltpu.VMEM(grp, F32),
                        pltpu.VMEM((SUBLANES, c), F32), pltpu.VMEM((1, c), F32)],
        input_output_aliases=aliases,
        compiler_params=_params("parallel", "arbitrary"),
        name="lru_long" if long_seq else "lru_short",
    )(*args)


def _pool_body(*refs, long_seq, past_len):
    u_ref, hist_ref, pw_ref, ps_ref = refs[:4]
    refs = refs[4:]
    if not long_seq:
        refs = refs[1:]
    o_ref, nh_ref, c_scr = refs
    g_n, _, c = u_ref.shape
    rows = g_n * SUBLANES
    x = u_ref[...]
    t = pl.program_id(1)
    if long_seq:
        @pl.when(t == 0)
        def _():
            c_scr[0] = hist_ref[0, 0:SUBLANES]
            c_scr[1] = hist_ref[0, SUBLANES:2 * SUBLANES]

        ext = jnp.concatenate([c_scr[...], x], axis=0)
        gs = 1
    else:
        hist = hist_ref[...]
        ext = jnp.concatenate([hist[:, 0:SUBLANES], hist[:, SUBLANES:2 * SUBLANES], x], axis=0)
        gs = g_n
    nh_ref[:, 0:SUBLANES] = ext[-2 * gs:-gs]
    nh_ref[:, SUBLANES:2 * SUBLANES] = ext[-gs:]
    if long_seq:
        c_scr[0] = ext[g_n]
        c_scr[1] = ext[g_n + 1]

    def prev(a):
        return jnp.concatenate([a[:gs], a[:-gs]], axis=0)

    n_grp = len(POOL_WINDOWS)
    pg = c // n_grp
    i0 = lax.broadcasted_iota(jnp.int32, (g_n, SUBLANES, pg), 0)
    i1 = lax.broadcasted_iota(jnp.int32, (g_n, SUBLANES, pg), 1)
    t_abs = (t * g_n + i0) * SUBLANES + i1 if long_seq else i1
    outs = []
    for gi, w in enumerate(POOL_WINDOWS):
        eg = ext[:, :, gi * pg:(gi + 1) * pg]
        s = eg
        k = 1
        while k < w:
            if k < SUBLANES:
                rolled = pltpu.roll(s, k, 1)
                row = lax.broadcasted_iota(jnp.int32, s.shape, 1)
                s = s + jnp.where(row >= k, rolled, prev(rolled))
            else:
                s = s + prev(s)
            k *= 2
        cnt = jnp.minimum(t_abs + (1 + past_len), w).astype(F32)
        d = s[2 * gs:] / cnt - eg[2 * gs:]
        d2 = d.reshape(rows, pg).astype(BF16)
        outs.append(jnp.dot(d2, pw_ref[gi], preferred_element_type=F32))
    y = jnp.concatenate(outs, axis=1) * ps_ref[...]
    o_ref[...] = y.astype(o_ref.dtype)


def _pool_call(proj3, col, o_prev, hist, pool_w, pool_scale, l, *, long_seq, past_len, n_outer, n_inner, g_n,
               group0, state_layer):
    c = pool_scale.shape[-1]
    m = proj3.shape[0] * SUBLANES
    nseq_blk = 1 if long_seq else g_n
    nseq = n_outer * nseq_blk
    gb0 = group0 // g_n
    pg = c // len(POOL_WINDOWS)
    hrows = 2 * SUBLANES
    if state_layer is None:
        h_spec = pl.BlockSpec((nseq_blk, hrows, c), lambda i, t: (i, 0, 0))
    else:
        h_spec = pl.BlockSpec((None, nseq_blk, hrows, c), lambda i, t: (state_layer, i, 0, 0))
    in_specs = [pl.BlockSpec((g_n, SUBLANES, c), lambda i, t: (gb0 + i * n_inner + t, 0, col)), h_spec,
                _layer(l, (len(POOL_WINDOWS), pg, pg)), _layer(l, (1, c))]
    args = [proj3, hist, pool_w, pool_scale]
    aliases = {}
    if o_prev is not None:
        in_specs.append(pl.BlockSpec(memory_space=pl.ANY))
        args.append(o_prev)
        aliases = {len(args) - 1: 0}
    return pl.pallas_call(
        functools.partial(_pool_body, long_seq=long_seq, past_len=past_len),
        grid=(n_outer, n_inner),
        in_specs=in_specs,
        out_specs=[pl.BlockSpec((g_n * SUBLANES, c), lambda i, t: (gb0 + i * n_inner + t, 0)),
                   pl.BlockSpec((nseq_blk, hrows, c), lambda i, t: (i, 0, 0))],
        out_shape=[jax.ShapeDtypeStruct((m, c), BF16), jax.ShapeDtypeStruct((nseq, hrows, c), F32)],
        scratch_shapes=[pltpu.VMEM((2, SUBLANES, c), F32)],
        input_output_aliases=aliases,
        compiler_params=_params("parallel", "arbitrary"),
        name="pool_long" if long_seq else "pool_short",
    )(*args)


SOLVE_PASSES = 1


def _mm(a, b):
    a_hi, b_hi = a.astype(BF16), b.astype(BF16)
    out = jnp.dot(a_hi, b_hi, preferred_element_type=F32)
    if SOLVE_PASSES == 3:
        a_lo = (a - a_hi.astype(F32)).astype(BF16)
        b_lo = (b - b_hi.astype(F32)).astype(BF16)
        out = out + jnp.dot(a_hi, b_lo, preferred_element_type=F32) + jnp.dot(a_lo, b_hi, preferred_element_type=F32)
    return out


def _delta_body(*refs, long_seq, tseq, n_aliased):
    (q_ref, k_ref, v_ref, z_ref, ba_ref, prev_ref, s0_ref, cw_ref, alog_ref, dtb_ref, nw_ref) = refs[:11]
    o_ref, s_ref, nb_ref, cx_scr = refs[11 + n_aliased:]
    g_n, _, c = q_ref.shape
    rows = g_n * SUBLANES
    nseq_blk = rows // tseq
    heads = c // HEAD_DIM
    first = pl.program_id(1) == 0

    if long_seq:
        @pl.when(first)
        def _():
            cx_scr[...] = prev_ref[0]
            s_ref[...] = s0_ref[...]

    cw = cw_ref[...]
    streams = []
    for idx, x_ref in enumerate((q_ref, k_ref, v_ref)):
        x = x_ref[...]
        lo, hi = idx * c, (idx + 1) * c
        if long_seq:
            prev = cx_scr[:, lo:hi]
            cx_scr[:, lo:hi] = x[g_n - 1]
            nb_ref[0, :, lo:hi] = x[g_n - 1]
        else:
            prev = prev_ref[:, :, lo:hi]
            nb_ref[:, :, lo:hi] = x
        y = _causal_conv(x, prev, cw[:, lo:hi], long_seq)
        streams.append((y * jax.nn.sigmoid(y)).reshape(rows, c))
    q_all, k_all, v_all = streams
    z_all = z_ref[...].reshape(rows, c)

    ba = ba_ref[...]
    beta_all = jax.nn.sigmoid(ba[:, 0:HEAD_DIM])
    g_all = -jnp.exp(alog_ref[...]) * _softplus(ba[:, HEAD_DIM:2 * HEAD_DIM] + dtb_ref[...])

    ri = lax.broadcasted_iota(jnp.int32, (rows, rows), 0)
    ci = lax.broadcasted_iota(jnp.int32, (rows, rows), 1)
    same = (ri // tseq) == (ci // tseq)
    incl = same & (ci <= ri)
    strict = same & (ci < ri)
    gamma_all = jnp.dot(incl.astype(F32), g_all, precision=_HI, preferred_element_type=F32)
    gtot_all = jnp.dot(same.astype(F32), g_all, precision=_HI, preferred_element_type=F32)
    gamma_t = gamma_all.T
    eye = (ri == ci).astype(F32)

    s_src = s_ref if long_seq else s0_ref
    hrange = range(heads)
    q_l, k_l, v_l, beta_l, gcol_l, gtot_l, decay_l, qb_l, kb_l, a_l = ([] for _ in range(10))
    for h in hrange:
        hs = slice(h * HEAD_DIM, (h + 1) * HEAD_DIM)
        q_h, k_h = q_all[:, hs], k_all[:, hs]
        q_h = q_h * lax.rsqrt(jnp.sum(q_h * q_h, axis=-1, keepdims=True) + EPS) * (HEAD_DIM ** -0.5)
        k_h = k_h * lax.rsqrt(jnp.sum(k_h * k_h, axis=-1, keepdims=True) + EPS)
        gcol = gamma_all[:, h:h + 1]
        decay = jnp.where(incl, jnp.exp(gcol - gamma_t[h:h + 1, :]), 0.0)
        qb, kb = q_h.astype(BF16), k_h.astype(BF16)
        beta = beta_all[:, h:h + 1]
        kk = lax.dot_general(kb, kb, _NT, preferred_element_type=F32)
        q_l.append(q_h)
        k_l.append(k_h)
        v_l.append(v_all[:, hs])
        beta_l.append(beta)
        gcol_l.append(gcol)
        gtot_l.append(gtot_all[:, h:h + 1])
        decay_l.append(decay)
        qb_l.append(qb)
        kb_l.append(kb)
        a_l.append(jnp.where(strict, beta * decay * kk, 0.0))

    t_l = [eye - jnp.where(ri // 2 == ci // 2, a, 0.0) for a in a_l]
    s = 2
    while s < tseq:
        off_mask = (ri // (2 * s) == ci // (2 * s)) & (ri % (2 * s) >= s) & (ci % (2 * s) < s)
        t_off = [_mm(t, jnp.where(off_mask, a, 0.0)) for t, a in zip(t_l, a_l)]
        t_l = [t - _mm(to, t) for t, to in zip(t_l, t_off)]
        s *= 2

    eg_l = [jnp.exp(g) for g in gcol_l]
    sol_l = [_mm(t, jnp.concatenate([b * v, (b * e) * k], axis=1))
             for t, b, v, e, k in zip(t_l, beta_l, v_l, eg_l, k_l)]
    qk_l = [(lax.dot_general(qb, kb, _NT, preferred_element_type=F32) * d).astype(BF16)
            for qb, kb, d in zip(qb_l, kb_l, decay_l)]
    qg_l = [(q * e).astype(BF16) for q, e in zip(q_l, eg_l)]
    kd_l = [(k * jnp.exp(gt - g)).astype(BF16) for k, gt, g in zip(k_l, gtot_l, gcol_l)]

    w_l, o_l, s_old = [], [], []
    for h in hrange:
        u_c, wkb = sol_l[h][:, :HEAD_DIM], sol_l[h][:, HEAD_DIM:].astype(BF16)
        w_parts, o_parts = [], []
        for sq in range(nseq_blk):
            rs = slice(sq * tseq, (sq + 1) * tseq)
            s_mat = s_src[sq, h]
            sb = s_mat.astype(BF16)
            s_old.append(s_mat)
            w_parts.append(u_c[rs] - jnp.dot(wkb[rs], sb, preferred_element_type=F32))
            o_parts.append(jnp.dot(qg_l[h][rs], sb, preferred_element_type=F32))
        w_l.append((jnp.concatenate(w_parts, axis=0) if nseq_blk > 1 else w_parts[0]).astype(BF16))
        o_l.append(jnp.concatenate(o_parts, axis=0) if nseq_blk > 1 else o_parts[0])
    for h in hrange:
        hs = slice(h * HEAD_DIM, (h + 1) * HEAD_DIM)
        for sq in range(nseq_blk):
            rs = slice(sq * tseq, (sq + 1) * tseq)
            g_last = jnp.exp(gtot_l[h][sq * tseq:sq * tseq + 1])
            s_ref[sq, h] = g_last * s_old[h * nseq_blk + sq] + lax.dot_general(
                kd_l[h][rs], w_l[h][rs], _TN, preferred_element_type=F32)
        o = o_l[h] + jnp.dot(qk_l[h], w_l[h], preferred_element_type=F32)
        o = o * lax.rsqrt(jnp.mean(o * o, axis=-1, keepdims=True) + EPS) * nw_ref[...]
        z_h = z_all[:, hs]
        o_ref[:, hs] = (o * (z_h * jax.nn.sigmoid(z_h))).astype(o_ref.dtype)


def _delta_call(proj3, col0, ba, o_prev, prev, s0, conv_w, alog, dtb, norm_w, l, *, long_seq, tseq, n_outer, n_inner,
                group0, state_layer, s_stack=None):
    c = conv_w.shape[-1] // 3
    heads = c // HEAD_DIM
    g_n = DN_CHUNK // SUBLANES
    m = proj3.shape[0] * SUBLANES
    nseq_blk = 1 if long_seq else DN_CHUNK // tseq
    nseq = n_outer * nseq_blk
    gb0 = group0 // g_n

    def x_spec(col):
        return pl.BlockSpec((g_n, SUBLANES, c), lambda i, t: (gb0 + i * n_inner + t, 0, col))

    if state_layer is None:
        p_spec = pl.BlockSpec((nseq_blk, SUBLANES, 3 * c), lambda i, t: (i, 0, 0))
        s_spec = pl.BlockSpec((nseq_blk, heads, HEAD_DIM, HEAD_DIM), lambda i, t: (i, 0, 0, 0))
    else:
        p_spec = pl.BlockSpec((None, nseq_blk, SUBLANES, 3 * c), lambda i, t: (state_layer, i, 0, 0))
        s_spec = pl.BlockSpec((None, nseq_blk, heads, HEAD_DIM, HEAD_DIM), lambda i, t: (state_layer, i, 0, 0, 0))
    in_specs = [x_spec(col0), x_spec(col0 + 1), x_spec(col0 + 2), x_spec(col0 + 3),
                pl.BlockSpec((DN_CHUNK, 2 * HEAD_DIM), lambda i, t: (gb0 + i * n_inner + t, 0)),
                p_spec, s_spec,
                _layer(l, (CONV_WIDTH, 3 * c)), _layer(l, (1, HEAD_DIM)), _layer(l, (1, HEAD_DIM)),
                _layer(l, (1, HEAD_DIM))]
    args = [proj3, proj3, proj3, proj3, ba, prev, s0, conv_w, alog, dtb, norm_w]
    aliases = {}
    if o_prev is not None:
        in_specs.append(pl.BlockSpec(memory_space=pl.ANY))
        args.append(o_prev)
        aliases = {len(args) - 1: 0}
    s_block = (nseq_blk, heads, HEAD_DIM, HEAD_DIM)
    if s_stack is None:
        s_out_spec = pl.BlockSpec(s_block, lambda i, t: (i, 0, 0, 0))
        s_out_shape = jax.ShapeDtypeStruct((nseq,) + s_block[1:], F32)
    else:
        depth, stacked = s_stack
        s_out_spec = pl.BlockSpec((None,) + s_block, lambda i, t: (l, i, 0, 0, 0))
        s_out_shape = jax.ShapeDtypeStruct((depth, nseq) + s_block[1:], F32)
        if stacked is not None:
            in_specs.append(pl.BlockSpec(memory_space=pl.ANY))
            args.append(stacked)
            aliases[len(args) - 1] = 1
    return pl.pallas_call(
        functools.partial(_delta_body, long_seq=long_seq, tseq=tseq, n_aliased=len(aliases)),
        grid=(n_outer, n_inner),
        in_specs=in_specs,
        out_specs=[pl.BlockSpec((DN_CHUNK, c), lambda i, t: (gb0 + i * n_inner + t, 0)),
                   s_out_spec,
                   pl.BlockSpec((nseq_blk, SUBLANES, 3 * c), lambda i, t: (i, 0, 0))],
        out_shape=[jax.ShapeDtypeStruct((m, c), BF16),
                   s_out_shape,
                   jax.ShapeDtypeStruct((nseq, SUBLANES, 3 * c), F32)],
        scratch_shapes=[pltpu.VMEM((SUBLANES, 3 * c), F32)],
        input_output_aliases=aliases,
        compiler_params=_params("parallel", "arbitrary"),
        name="delta_long" if long_seq else "delta_short",
    )(*args)


def _pad_front(a, rows):
    pad = [(0, 0)] * a.ndim
    pad[-2] = (rows - a.shape[-2], 0)
    return jnp.pad(a, pad)


def _block_diag_tiles(w):
    depth, nb, b, _ = w.shape
    per = GATE_TILE // b
    w5 = w.reshape(depth, nb // per, per, b, b)
    eye = jnp.eye(per, dtype=w.dtype)
    t = jnp.einsum('lijcd,jk->lijckd', w5, eye)
    return t.reshape(depth, nb // per, GATE_TILE, GATE_TILE).astype(BF16)


def kernel(x_prompt, x_sample, state_lru_h, state_lru_conv, state_dn_s, state_dn_conv, state_pool, norm_mix, w_in, lru_conv_w, lru_conv_b, lru_gate_a_w, lru_gate_a_b, lru_gate_x_w, lru_gate_x_b, lru_lambda, dn_conv_w, dn_a_log, dn_dt_bias, dn_norm_w, pool_w, pool_scale, w_br_lru, w_br_dn, w_br_pool, w_out, norm_mlp, w_up, w_down, norm_final):
    bp, tp, d = x_prompt.shape
    bs, ts, _ = x_sample.shape
    depth = w_in.shape[0]
    half = d // 2
    heads = half // HEAD_DIM
    mp, ms = bp * tp, bs * ts
    m = mp + ms
    assert ts == SUBLANES and tp % DN_CHUNK == 0 and bs % (DN_CHUNK // ts) == 0
    assert lru_gate_a_w.shape[-1] == LRU_BLOCK and half % GATE_TILE == 0
    assert state_pool.shape[-2] == POOL_BUF

    n_pre = 6 * half
    ba0 = n_pre
    pool0 = ba0 + 2 * heads
    gates0 = pool0 + half
    w_in_t = jnp.swapaxes(w_in, 1, 2)
    n_gate_tiles, n_stream_tiles = 3 * d // half, n_pre // half + 1
    assert gates0 % SUBLANES == 0 and pool0 % SUBLANES == 0

    def gate_row0(j):
        return pl.multiple_of(gates0 + j * half, SUBLANES)

    def stream_row0(j):
        return pl.multiple_of(jnp.where(j < n_stream_tiles - 1, j * half, pool0), SUBLANES)

    row_fill = ((0, 0), (0, HEAD_DIM - heads), (0, 0))
    w_ba = jnp.concatenate([jnp.pad(w_in_t[:, ba0:ba0 + heads], row_fill),
                            jnp.pad(w_in_t[:, ba0 + heads:ba0 + 2 * heads], row_fill)], axis=1).astype(BF16)
    col0 = 0

    def row(a):
        return a.reshape(depth, 1, a.shape[-1])

    def lane_pad(a):
        return jnp.pad(a, ((0, 0), (0, HEAD_DIM - a.shape[-1]))).reshape(depth, 1, HEAD_DIM)

    lru_w = (lru_conv_w, row(lru_conv_b), _block_diag_tiles(lru_gate_a_w), row(lru_gate_a_b),
             _block_diag_tiles(lru_gate_x_w), row(lru_gate_x_b), row(lru_lambda))
    alog, dtb, dn_nw = lane_pad(dn_a_log), lane_pad(dn_dt_bias), row(dn_norm_w)
    pool_wb, pool_sc = pool_w.astype(BF16), row(pool_scale)
    w_br_lru_b, w_br_dn_b, w_br_pool_b = w_br_lru.astype(BF16), w_br_dn.astype(BF16), w_br_pool.astype(BF16)
    w_out_b, w_down_b = w_out.astype(BF16), w_down.astype(BF16)
    g_mix, g_mlp = row(norm_mix), row(norm_mlp)

    s_lru_h0 = state_lru_h.reshape(depth, bs, 1, half)
    s_lru_prev = _pad_front(state_lru_conv, SUBLANES)
    s_dn_prev = _pad_front(state_dn_conv, SUBLANES)
    s_pool_hist = _pad_front(state_pool, 2 * SUBLANES)
    p_lru_h0 = jnp.zeros((bp, 1, half), F32)
    p_lru_prev = jnp.zeros((bp, SUBLANES, half), F32)
    p_dn_prev = jnp.zeros((bp, SUBLANES, 3 * half), F32)
    p_dn_s0 = jnp.zeros((bp, heads, HEAD_DIM, HEAD_DIM), F32)
    p_pool_hist = jnp.zeros((bp, 2 * SUBLANES, half), F32)

    x = jnp.concatenate([x_prompt.reshape(mp, d), x_sample.reshape(ms, d)], axis=0)

    tm = 512 if (mp % 512 == 0 and ms % 512 == 0) else 256
    tm_big = 1024 if (mp % 1024 == 0 and ms % 1024 == 0) else tm
    lru_g = 32
    assert (tp // SUBLANES) % lru_g == 0 and bs % lru_g == 0 and m % tm == 0 and mp % tm == 0
    n_t = tp // SUBLANES // lru_g
    long_kw = dict(long_seq=True, n_outer=bp, n_inner=n_t, g_n=lru_g, group0=0, state_layer=None)
    short_kw = dict(long_seq=False, n_outer=bs // lru_g, n_inner=1, g_n=lru_g, group0=mp // SUBLANES)
    seq_per_blk = DN_CHUNK // ts

    p_states, s_states = [], []
    s_dn_stack = None
    xn, ba = _norm_cast(x, g_mix, w_ba, 0, tm=tm)
    for l in range(depth):
        w_rows = (None, pl.Element(half), pl.Element(d))
        gates = _stream_matmul(xn, w_in_t, pl.BlockSpec(w_rows, lambda j, i, l=l: (l, gate_row0(j), 0)), (half, d),
                               n_gate_tiles, tm=tm_big, tn=half, out_dtype=BF16, w_rows_are_outputs=True,
                               act="sigmoid")
        proj = _stream_matmul(xn, w_in_t, pl.BlockSpec(w_rows, lambda j, i, l=l: (l, stream_row0(j), 0)), (half, d),
                              n_stream_tiles, tm=tm_big, tn=half, out_dtype=F32, w_rows_are_outputs=True)
        proj3 = proj.reshape(m // SUBLANES, SUBLANES, proj.shape[-1])

        o_lru, p_h, p_cb = _lru_call(proj3, col0, None, p_lru_prev, p_lru_h0, lru_w, l, **long_kw)
        o_lru, s_h, s_cb = _lru_call(proj3, col0, o_lru, s_lru_prev, s_lru_h0, lru_w, l, state_layer=l, **short_kw)

        o_dn, p_s, p_db = _delta_call(proj3, col0 + 2, ba, None, p_dn_prev, p_dn_s0, dn_conv_w, alog, dtb, dn_nw, l,
                                      long_seq=True, tseq=DN_CHUNK, n_outer=bp, n_inner=tp // DN_CHUNK, group0=0,
                                      state_layer=None)
        o_dn, s_dn_stack, s_db = _delta_call(proj3, col0 + 2, ba, o_dn, s_dn_prev, state_dn_s, dn_conv_w, alog, dtb,
                                             dn_nw, l, long_seq=False, tseq=ts, n_outer=bs // seq_per_blk, n_inner=1,
                                             group0=mp // SUBLANES, state_layer=l, s_stack=(depth, s_dn_stack))

        o_pool, p_ph = _pool_call(proj3, col0 + 6, None, p_pool_hist, pool_wb, pool_sc, l, past_len=0, **long_kw)
        o_pool, s_ph = _pool_call(proj3, col0 + 6, o_pool, s_pool_hist, pool_wb, pool_sc, l, past_len=PAST_LEN,
                                  state_layer=l, **short_kw)

        merged = _merge(o_lru, o_dn, o_pool, gates, w_br_lru_b, w_br_dn_b, w_br_pool_b, l, tm=512, tn=d)
        x, xn = _matmul_res(merged, w_out_b, l, x, tm=tm, tk=d, gain=g_mlp, gain_layer=l)
        w_up_spec = pl.BlockSpec((None, d, half), lambda j, i, l=l: (l, 0, j))
        hm = _stream_matmul(xn, w_up, w_up_spec, (d, half), w_up.shape[-1] // half, tm=tm_big, tn=half,
                            out_dtype=BF16, w_rows_are_outputs=False, act="relu2")
        if l + 1 < depth:
            x, xn, ba = _matmul_res(hm, w_down_b, l, x, tm=tm, tk=d, gain=g_mix, gain_layer=l + 1, w_small=w_ba)
        else:
            x, = _matmul_res(hm, w_down_b, l, x, tm=tm, tk=d)
        p_states.append((p_h, p_cb, p_s, p_db, p_ph))
        s_states.append((s_h, s_cb, None, s_db, s_ph))

    y_prompt = _final_norm(x, norm_final.reshape(1, d), 0, mp, tm=tm).reshape(bp, tp, d)
    y_sample = _final_norm(x, norm_final.reshape(1, d), mp, ms, tm=tm).reshape(bs, ts, d)

    def collect(states, nseq, sm=None):
        tail = CONV_WIDTH - 1
        h = jnp.stack([s[0] for s in states]).reshape(depth, nseq, half)
        cb = jnp.stack([s[1] for s in states])[:, :, SUBLANES - tail:]
        if sm is None:
            sm = jnp.stack([s[2] for s in states])
        db = jnp.stack([s[3] for s in states])[:, :, SUBLANES - tail:]
        ph = jnp.stack([s[4] for s in states])[:, :, 2 * SUBLANES - POOL_BUF:]
        return h, cb, sm, db, ph

    return (y_prompt, y_sample) + collect(p_states, bp) + collect(s_states, bs, s_dn_stack)
```

```python
import functools

import jax
import jax.numpy as jnp
from jax import lax
from jax.experimental import pallas as pl
from jax.experimental.pallas import tpu as pltpu

F32 = jnp.float32
BF16 = jnp.bfloat16

EPS = 1e-6
LRU_C = 8.0
CONV_WIDTH = 4
HEAD_DIM = 128
LRU_BLOCK = 64
POOL_WINDOWS = (2, 4, 8, 16)
POOL_BUF = 15
DN_CHUNK = 64
PAST_LEN = 16384
SUBLANES = 8
GATE_TILE = 256
VMEM_LIMIT = 56 * 1024 * 1024

_NT = (((1,), (1,)), ((), ()))
_TN = (((0,), (0,)), ((), ()))
_HI = lax.Precision.HIGHEST


def _params(*sem):
    return pltpu.CompilerParams(dimension_semantics=sem, vmem_limit_bytes=VMEM_LIMIT)


def _full(shape):
    return pl.BlockSpec(shape, lambda *_: (0,) * len(shape))


def _layer(l, shape):
    return pl.BlockSpec((None,) + shape, lambda *_: (l,) + (0,) * len(shape))


def _rms(x, gain):
    return x * lax.rsqrt(jnp.mean(x * x, axis=-1, keepdims=True) + EPS) * gain


def _norm_cast_body(x_ref, g_ref, ws_ref, xn_ref, os_ref):
    xn = _rms(x_ref[...], g_ref[...]).astype(BF16)
    xn_ref[...] = xn
    os_ref[...] = lax.dot_general(xn, ws_ref[...].astype(BF16), _NT, preferred_element_type=F32)


def _norm_cast(x, gain, w_small, l, *, tm):
    m, d = x.shape
    ns = w_small.shape[-2]
    return pl.pallas_call(
        _norm_cast_body,
        grid=(m // tm,),
        in_specs=[pl.BlockSpec((tm, d), lambda i: (i, 0)), _layer(l, (1, d)), _layer(l, (ns, d))],
        out_specs=[pl.BlockSpec((tm, d), lambda i: (i, 0)), pl.BlockSpec((tm, ns), lambda i: (i, 0))],
        out_shape=[jax.ShapeDtypeStruct((m, d), BF16), jax.ShapeDtypeStruct((m, ns), F32)],
        compiler_params=_params("parallel"),
        name="norm_cast",
    )(x, gain, w_small)


def _stream_matmul_body(x_ref, w_ref, o_ref, wb_ref, *, w_rows_are_outputs, act):
    @pl.when(pl.program_id(1) == 0)
    def _():
        wb_ref[...] = w_ref[...].astype(BF16)

    dims = _NT if w_rows_are_outputs else (((1,), (0,)), ((), ()))
    acc = lax.dot_general(x_ref[...], wb_ref[...], dims, preferred_element_type=F32)
    if act == "relu2":
        acc = jnp.square(jnp.maximum(acc, 0.0))
    elif act == "sigmoid":
        acc = jax.nn.sigmoid(acc)
    o_ref[...] = acc.astype(o_ref.dtype)


def _stream_matmul(x, w, w_spec, w_block, n_tiles, *, tm, tn, out_dtype, w_rows_are_outputs, act=None):
    m, d = x.shape
    return pl.pallas_call(
        functools.partial(_stream_matmul_body, w_rows_are_outputs=w_rows_are_outputs, act=act),
        grid=(n_tiles, m // tm),
        in_specs=[pl.BlockSpec((tm, d), lambda j, i: (i, 0)), w_spec],
        out_specs=pl.BlockSpec((tm, tn), lambda j, i: (i, j)),
        out_shape=jax.ShapeDtypeStruct((m, n_tiles * tn), out_dtype),
        scratch_shapes=[pltpu.VMEM(w_block, BF16)],
        compiler_params=_params("parallel", "arbitrary"),
        name="stream_matmul_t" if w_rows_are_outputs else "stream_matmul",
    )(x, w)


def _matmul_res_body(*refs, n_k, emit_norm, emit_small):
    a_ref, w_ref, r_ref = refs[:3]
    refs = refs[3:]
    if emit_norm:
        g_ref = refs[0]
        refs = refs[1:]
    if emit_small:
        ws_ref = refs[0]
        refs = refs[1:]
    o_ref = refs[0]
    k = pl.program_id(1)

    @pl.when(k == 0)
    def _():
        o_ref[...] = r_ref[...]

    o_ref[...] += jnp.dot(a_ref[...], w_ref[...], preferred_element_type=F32)

    if emit_norm:
        @pl.when(k == n_k - 1)
        def _():
            xn = _rms(o_ref[...], g_ref[...]).astype(BF16)
            refs[1][...] = xn
            if emit_small:
                refs[2][...] = lax.dot_general(xn, ws_ref[...].astype(BF16), _NT, preferred_element_type=F32)


def _matmul_res(a, w, l, res, *, tm, tk, gain=None, gain_layer=0, w_small=None):
    m, k = a.shape
    d = w.shape[-1]
    n_k = k // tk
    emit_norm = gain is not None
    emit_small = w_small is not None
    in_specs = [pl.BlockSpec((tm, tk), lambda i, kk: (i, kk)),
                pl.BlockSpec((None, tk, d), lambda i, kk: (l, kk, 0)),
                pl.BlockSpec((tm, d), lambda i, kk: (i, 0))]
    out_specs = [pl.BlockSpec((tm, d), lambda i, kk: (i, 0))]
    out_shape = [jax.ShapeDtypeStruct((m, d), F32)]
    args = [a, w, res]
    if emit_norm:
        in_specs.append(_layer(gain_layer, (1, d)))
        args.append(gain)
        out_specs.append(pl.BlockSpec((tm, d), lambda i, kk: (i, 0)))
        out_shape.append(jax.ShapeDtypeStruct((m, d), BF16))
    if emit_small:
        ns = w_small.shape[-2]
        in_specs.append(_layer(gain_layer, (ns, d)))
        args.append(w_small)
        out_specs.append(pl.BlockSpec((tm, ns), lambda i, kk: (i, 0)))
        out_shape.append(jax.ShapeDtypeStruct((m, ns), F32))
    return pl.pallas_call(
        functools.partial(_matmul_res_body, n_k=n_k, emit_norm=emit_norm, emit_small=emit_small),
        grid=(m // tm, n_k),
        in_specs=in_specs, out_specs=out_specs, out_shape=out_shape,
        compiler_params=_params("parallel", "arbitrary"),
        name="matmul_res",
    )(*args)


def _merge_body(ol_ref, od_ref, op_ref, gl_ref, gd_ref, gp_ref, wl_ref, wd_ref, wp_ref, o_ref):
    def branch(o_ref_, g_ref_, w_ref_):
        return g_ref_[...].astype(F32) * jnp.dot(o_ref_[...], w_ref_[...], preferred_element_type=F32)

    m = branch(ol_ref, gl_ref, wl_ref) + branch(od_ref, gd_ref, wd_ref) + branch(op_ref, gp_ref, wp_ref)
    o_ref[...] = m.astype(o_ref.dtype)


def _merge(o_lru, o_dn, o_pool, gates, w_lru, w_dn, w_pool, l, *, tm, tn):
    m, half = o_lru.shape
    d = w_lru.shape[-1]
    nj = d // tn
    g0 = 0

    def o_spec():
        return pl.BlockSpec((tm, half), lambda i, j: (i, 0))

    def g_spec(b):
        return pl.BlockSpec((tm, tn), lambda i, j: (i, g0 + b * nj + j))

    def w_spec():
        return pl.BlockSpec((None, half, tn), lambda i, j: (l, 0, j))

    return pl.pallas_call(
        _merge_body,
        grid=(m // tm, nj),
        in_specs=[o_spec(), o_spec(), o_spec(), g_spec(0), g_spec(1), g_spec(2), w_spec(), w_spec(), w_spec()],
        out_specs=pl.BlockSpec((tm, tn), lambda i, j: (i, j)),
        out_shape=jax.ShapeDtypeStruct((m, d), BF16),
        compiler_params=_params("parallel", "arbitrary"),
        name="merge",
    )(o_lru, o_dn, o_pool, gates, gates, gates, w_lru, w_dn, w_pool)


def _rmsnorm_body(x_ref, g_ref, o_ref):
    x = x_ref[...]
    inv = lax.rsqrt(jnp.mean(x * x, axis=-1, keepdims=True) + EPS)
    o_ref[...] = x * inv * g_ref[...]


def _final_norm(x, gain, row0, rows, *, tm):
    d = x.shape[-1]
    b0 = row0 // tm
    return pl.pallas_call(
        _rmsnorm_body,
        grid=(rows // tm,),
        in_specs=[pl.BlockSpec((tm, d), lambda i: (b0 + i, 0)), _full((1, d))],
        out_specs=pl.BlockSpec((tm, d), lambda i: (i, 0)),
        out_shape=jax.ShapeDtypeStruct((rows, d), F32),
        compiler_params=_params("parallel"),
        name="final_norm",
    )(x, gain)


def _softplus(x):
    return jnp.maximum(x, 0.0) + jnp.log1p(jnp.exp(-jnp.abs(x)))


def _shift_rows(cur, prev, k, chained=False):
    ax = cur.ndim - 2
    row = lax.broadcasted_iota(jnp.int32, cur.shape, ax)
    if chained:
        rolled = pltpu.roll(jnp.concatenate([prev[None], cur], axis=0), k, ax)
        return jnp.where(row >= k, rolled[1:], rolled[:-1])
    return jnp.where(row >= k, pltpu.roll(cur, k, ax), pltpu.roll(prev, k, ax))


def _causal_conv(x, prev, cw, chained):
    y = cw[CONV_WIDTH - 1:CONV_WIDTH] * x
    for k in range(1, CONV_WIDTH):
        y = y + cw[CONV_WIDTH - 1 - k:CONV_WIDTH - k] * _shift_rows(x, prev, k, chained)
    return y


def _lru_body(*refs, long_seq):
    (x_ref, gate_ref, prev_ref, h0_ref, cw_ref, cb_ref, wa_ref, ba_ref, wx_ref, bx_ref, lam_ref) = refs[:11]
    refs = refs[11:]
    if not long_seq:
        refs = refs[1:]
    o_ref, hl_ref, nb_ref, a_scr, b_scr, h_scr, cx_scr, ch_scr = refs
    g_n, _, c = x_ref.shape
    rows = g_n * SUBLANES
    x = x_ref[...]
    if long_seq:
        @pl.when(pl.program_id(1) == 0)
        def _():
            cx_scr[...] = prev_ref[0]
            ch_scr[...] = h0_ref[0]

        prev = cx_scr[...]
    else:
        prev = prev_ref[...]
    xc = _causal_conv(x, prev, cw_ref[...], long_seq) + cb_ref[...]
    if long_seq:
        cx_scr[...] = x[g_n - 1]
        nb_ref[0] = x[g_n - 1]
    else:
        nb_ref[...] = x

    xc2 = xc.reshape(rows, c)
    xb = xc2.astype(BF16)
    nblk = c // GATE_TILE

    def gate(w_ref, b_ref):
        parts = [jnp.dot(xb[:, i * GATE_TILE:(i + 1) * GATE_TILE], w_ref[i], preferred_element_type=F32)
                 for i in range(nblk)]
        return jax.nn.sigmoid(jnp.concatenate(parts, axis=1) + b_ref[...])

    r = gate(wa_ref, ba_ref)
    i_g = gate(wx_ref, bx_ref)
    log_a = (-LRU_C) * r * _softplus(-lam_ref[...])
    a = jnp.exp(log_a)
    one_minus_a2 = -jnp.tanh(log_a) * (a * a + 1.0)
    b = jnp.sqrt(one_minus_a2) * (i_g * xc2)

    a3 = a.reshape(g_n, SUBLANES, c)
    b3 = b.reshape(g_n, SUBLANES, c)
    row = lax.broadcasted_iota(jnp.int32, a3.shape, 1)
    s = 1
    while s < SUBLANES:
        m = row >= s
        a_sh = pltpu.roll(a3, s, 1)
        b_sh = pltpu.roll(b3, s, 1)
        b3 = jnp.where(m, a3 * b_sh + b3, b3)
        a3 = jnp.where(m, a3 * a_sh, a3)
        s *= 2
    a_scr[...] = a3
    b_scr[...] = b3

    def chain(g, h_in):
        if not long_seq:
            h_in = h0_ref[g]
        hg = a_scr[g] * h_in + b_scr[g]
        h_scr[g] = hg
        h_out = hg[SUBLANES - 1:SUBLANES]
        if not long_seq:
            hl_ref[g] = h_out
        return h_out

    h_init = ch_scr[...] if long_seq else jnp.zeros((1, c), F32)
    h_fin = lax.fori_loop(0, g_n, chain, h_init)
    if long_seq:
        ch_scr[...] = h_fin
        hl_ref[0] = h_fin
    h = h_scr[...].reshape(rows, c)
    gl = gate_ref[...].reshape(rows, c)
    o_ref[...] = (h * jax.nn.gelu(gl)).astype(o_ref.dtype)


def _lru_call(proj3, col0, o_prev, prev, h0, weights, l, *, long_seq, n_outer, n_inner, g_n, group0, state_layer):
    cw, cb, wa, ba, wx, bx, lam = weights
    c = cw.shape[-1]
    m = proj3.shape[0] * SUBLANES
    nseq_blk = 1 if long_seq else g_n
    nseq = n_outer * nseq_blk
    gb0 = group0 // g_n

    def x_spec(col):
        return pl.BlockSpec((g_n, SUBLANES, c), lambda i, t: (gb0 + i * n_inner + t, 0, col))

    def state_spec(r):
        if state_layer is None:
            return pl.BlockSpec((nseq_blk, r, c), lambda i, t: (i, 0, 0))
        return pl.BlockSpec((None, nseq_blk, r, c), lambda i, t: (state_layer, i, 0, 0))

    ng = c // GATE_TILE
    in_specs = [x_spec(col0), x_spec(col0 + 1), state_spec(SUBLANES), state_spec(1),
                _layer(l, (CONV_WIDTH, c)), _layer(l, (1, c)),
                _layer(l, (ng, GATE_TILE, GATE_TILE)), _layer(l, (1, c)),
                _layer(l, (ng, GATE_TILE, GATE_TILE)), _layer(l, (1, c)), _layer(l, (1, c))]
    args = [proj3, proj3, prev, h0, cw, cb, wa, ba, wx, bx, lam]
    aliases = {}
    if o_prev is not None:
        in_specs.append(pl.BlockSpec(memory_space=pl.ANY))
        args.append(o_prev)
        aliases = {len(args) - 1: 0}
    out_specs = [pl.BlockSpec((g_n * SUBLANES, c), lambda i, t: (gb0 + i * n_inner + t, 0)),
                 pl.BlockSpec((nseq_blk, 1, c), lambda i, t: (i, 0, 0)),
                 pl.BlockSpec((nseq_blk, SUBLANES, c), lambda i, t: (i, 0, 0))]
    out_shape = [jax.ShapeDtypeStruct((m, c), BF16),
                 jax.ShapeDtypeStruct((nseq, 1, c), F32),
                 jax.ShapeDtypeStruct((nseq, SUBLANES, c), F32)]
    grp = (g_n, SUBLANES, c)
    return pl.pallas_call(
        functools.partial(_lru_body, long_seq=long_seq),
        grid=(n_outer, n_inner),
        in_specs=in_specs, out_specs=out_specs, out_shape=out_shape,
        scratch_shapes=[pltpu.VMEM(grp, F32), pltpu.VMEM(grp, F32), pltpu.VMEM(grp, F32),
                        pltpu.VMEM((SUBLANES, c), F32), pltpu.VMEM((1, c), F32)],
        input_output_aliases=aliases,
        compiler_params=_params("parallel", "arbitrary"),
        name="lru_long" if long_seq else "lru_short",
    )(*args)


def _pool_body(*refs, long_seq, past_len):
    u_ref, hist_ref, pw_ref, ps_ref = refs[:4]
    refs = refs[4:]
    if not long_seq:
        refs = refs[1:]
    o_ref, nh_ref, c_scr = refs
    g_n, _, c = u_ref.shape
    rows = g_n * SUBLANES
    x = u_ref[...]
    t = pl.program_id(1)
    if long_seq:
        @pl.when(t == 0)
        def _():
            c_scr[0] = hist_ref[0, 0:SUBLANES]
            c_scr[1] = hist_ref[0, SUBLANES:2 * SUBLANES]

        ext = jnp.concatenate([c_scr[...], x], axis=0)
        gs = 1
    else:
        hist = hist_ref[...]
        ext = jnp.concatenate([hist[:, 0:SUBLANES], hist[:, SUBLANES:2 * SUBLANES], x], axis=0)
        gs = g_n
    nh_ref[:, 0:SUBLANES] = ext[-2 * gs:-gs]
    nh_ref[:, SUBLANES:2 * SUBLANES] = ext[-gs:]
    if long_seq:
        c_scr[0] = ext[g_n]
        c_scr[1] = ext[g_n + 1]

    def prev(a):
        return jnp.concatenate([a[:gs], a[:-gs]], axis=0)

    n_grp = len(POOL_WINDOWS)
    pg = c // n_grp
    i0 = lax.broadcasted_iota(jnp.int32, (g_n, SUBLANES, pg), 0)
    i1 = lax.broadcasted_iota(jnp.int32, (g_n, SUBLANES, pg), 1)
    t_abs = (t * g_n + i0) * SUBLANES + i1 if long_seq else i1
    outs = []
    for gi, w in enumerate(POOL_WINDOWS):
        eg = ext[:, :, gi * pg:(gi + 1) * pg]
        s = eg
        k = 1
        while k < w:
            if k < SUBLANES:
                rolled = pltpu.roll(s, k, 1)
                row = lax.broadcasted_iota(jnp.int32, s.shape, 1)
                s = s + jnp.where(row >= k, rolled, prev(rolled))
            else:
                s = s + prev(s)
            k *= 2
        cnt = jnp.minimum(t_abs + (1 + past_len), w).astype(F32)
        d = s[2 * gs:] / cnt - eg[2 * gs:]
        d2 = d.reshape(rows, pg).astype(BF16)
        outs.append(jnp.dot(d2, pw_ref[gi], preferred_element_type=F32))
    y = jnp.concatenate(outs, axis=1) * ps_ref[...]
    o_ref[...] = y.astype(o_ref.dtype)


def _pool_call(proj3, col, o_prev, hist, pool_w, pool_scale, l, *, long_seq, past_len, n_outer, n_inner, g_n,
               group0, state_layer):
    c = pool_scale.shape[-1]
    m = proj3.shape[0] * SUBLANES
    nseq_blk = 1 if long_seq else g_n
    nseq = n_outer * nseq_blk
    gb0 = group0 // g_n
    pg = c // len(POOL_WINDOWS)
    hrows = 2 * SUBLANES
    if state_layer is None:
        h_spec = pl.BlockSpec((nseq_blk, hrows, c), lambda i, t: (i, 0, 0))
    else:
        h_spec = pl.BlockSpec((None, nseq_blk, hrows, c), lambda i, t: (state_layer, i, 0, 0))
    in_specs = [pl.BlockSpec((g_n, SUBLANES, c), lambda i, t: (gb0 + i * n_inner + t, 0, col)), h_spec,
                _layer(l, (len(POOL_WINDOWS), pg, pg)), _layer(l, (1, c))]
    args = [proj3, hist, pool_w, pool_scale]
    aliases = {}
    if o_prev is not None:
        in_specs.append(pl.BlockSpec(memory_space=pl.ANY))
        args.append(o_prev)
        aliases = {len(args) - 1: 0}
    return pl.pallas_call(
        functools.partial(_pool_body, long_seq=long_seq, past_len=past_len),
        grid=(n_outer, n_inner),
        in_specs=in_specs,
        out_specs=[pl.BlockSpec((g_n * SUBLANES, c), lambda i, t: (gb0 + i * n_inner + t, 0)),
                   pl.BlockSpec((nseq_blk, hrows, c), lambda i, t: (i, 0, 0))],
        out_shape=[jax.ShapeDtypeStruct((m, c), BF16), jax.ShapeDtypeStruct((nseq, hrows, c), F32)],
        scratch_shapes=[pltpu.VMEM((2, SUBLANES, c), F32)],
        input_output_aliases=aliases,
        compiler_params=_params("parallel", "arbitrary"),
        name="pool_long" if long_seq else "pool_short",
    )(*args)


CHUNK_STAGE_LAG = 3
SOLVE_PASSES = 1


def _mm(a, b):
    a_hi, b_hi = a.astype(BF16), b.astype(BF16)
    out = jnp.dot(a_hi, b_hi, preferred_element_type=F32)
    if SOLVE_PASSES == 3:
        a_lo = (a - a_hi.astype(F32)).astype(BF16)
        b_lo = (b - b_hi.astype(F32)).astype(BF16)
        out = out + jnp.dot(a_hi, b_lo, preferred_element_type=F32) + jnp.dot(a_lo, b_hi, preferred_element_type=F32)
    return out


def _delta_body(*refs, long_seq, tseq, n_aliased):
    (q_ref, k_ref, v_ref, z_ref, ba_ref, prev_ref, s0_ref, cw_ref, alog_ref, dtb_ref, nw_ref) = refs[:11]
    o_ref, s_ref, nb_ref, cx_scr = refs[11 + n_aliased:]
    g_n, _, c = q_ref.shape
    rows = g_n * SUBLANES
    n_sub = rows // DN_CHUNK
    nseq_chunk = DN_CHUNK // tseq
    heads = c // HEAD_DIM

    if long_seq:
        @pl.when(pl.program_id(1) == 0)
        def _():
            cx_scr[...] = prev_ref[0]
            s_ref[...] = s0_ref[...]

    cw = cw_ref[...]
    streams = []
    for idx, x_ref in enumerate((q_ref, k_ref, v_ref)):
        x = x_ref[...]
        lo, hi = idx * c, (idx + 1) * c
        if long_seq:
            prev = cx_scr[:, lo:hi]
            cx_scr[:, lo:hi] = x[g_n - 1]
            nb_ref[0, :, lo:hi] = x[g_n - 1]
        else:
            prev = prev_ref[:, :, lo:hi]
            nb_ref[:, :, lo:hi] = x
        y = _causal_conv(x, prev, cw[:, lo:hi], long_seq)
        streams.append((y * jax.nn.sigmoid(y)).reshape(rows, c))
    q_all, k_all, v_all = streams
    z_all = z_ref[...].reshape(rows, c)

    ba = ba_ref[...]
    beta_all = jax.nn.sigmoid(ba[:, 0:HEAD_DIM])
    g_all = -jnp.exp(alog_ref[...]) * _softplus(ba[:, HEAD_DIM:2 * HEAD_DIM] + dtb_ref[...])

    ri = lax.broadcasted_iota(jnp.int32, (DN_CHUNK, DN_CHUNK), 0)
    ci = lax.broadcasted_iota(jnp.int32, (DN_CHUNK, DN_CHUNK), 1)
    same = (ri // tseq) == (ci // tseq)
    incl = same & (ci <= ri)
    strict = same & (ci < ri)
    eye = (ri == ci).astype(F32)
    incl_f, same_f = incl.astype(F32), same.astype(F32)

    state = [s_ref[0, h] for h in range(heads)] if long_seq else None
    masks = (ri, ci, incl, strict, eye, incl_f, same_f)
    waiting = [_delta_chunk([(sub, h) for h in range(heads)], state, masks,
                            (q_all, k_all, v_all, z_all, beta_all, g_all), (s0_ref, s_ref, nw_ref, o_ref),
                            long_seq, tseq) for sub in range(n_sub)]
    running, rnd = [], 0
    while waiting or running:
        if waiting and rnd % CHUNK_STAGE_LAG == 0:
            running.append(waiting.pop(0))
        for gen in list(running):
            if next(gen, "done") == "done":
                running.remove(gen)
        rnd += 1
    if long_seq:
        for h in range(heads):
            s_ref[0, h] = state[h]


def _delta_chunk(pairs, state, masks, streams, refs, long_seq, tseq):
    ri, ci, incl, strict, eye, incl_f, same_f = masks
    q_all, k_all, v_all, z_all, beta_all, g_all = streams
    s0_ref, s_ref, nw_ref, o_ref = refs
    nseq_chunk = DN_CHUNK // tseq
    gamma_s, gtot_s, gamma_ts = {}, {}, {}
    for sub in sorted({sub for sub, _ in pairs}):
        g_sub = g_all[sub * DN_CHUNK:(sub + 1) * DN_CHUNK]
        gamma = jnp.dot(incl_f, g_sub, precision=_HI, preferred_element_type=F32)
        gamma_s[sub] = gamma
        gtot_s[sub] = jnp.dot(same_f, g_sub, precision=_HI, preferred_element_type=F32)
        gamma_ts[sub] = gamma.T

    q_l, k_l, v_l, beta_l, gcol_l, gtot_l, decay_l, qb_l, kb_l, a_l = ([] for _ in range(10))
    for sub, h in pairs:
        rs = slice(sub * DN_CHUNK, (sub + 1) * DN_CHUNK)
        hs = slice(h * HEAD_DIM, (h + 1) * HEAD_DIM)
        q_h, k_h = q_all[rs, hs], k_all[rs, hs]
        q_h = q_h * lax.rsqrt(jnp.sum(q_h * q_h, axis=-1, keepdims=True) + EPS) * (HEAD_DIM ** -0.5)
        k_h = k_h * lax.rsqrt(jnp.sum(k_h * k_h, axis=-1, keepdims=True) + EPS)
        gcol = gamma_s[sub][:, h:h + 1]
        decay = jnp.where(incl, jnp.exp(gcol - gamma_ts[sub][h:h + 1, :]), 0.0)
        qb, kb = q_h.astype(BF16), k_h.astype(BF16)
        beta = beta_all[rs, h:h + 1]
        kk = lax.dot_general(kb, kb, _NT, preferred_element_type=F32)
        q_l.append(q_h)
        k_l.append(k_h)
        v_l.append(v_all[rs, hs])
        beta_l.append(beta)
        gcol_l.append(gcol)
        gtot_l.append(gtot_s[sub][:, h:h + 1])
        decay_l.append(decay)
        qb_l.append(qb)
        kb_l.append(kb)
        a_l.append(jnp.where(strict, beta * decay * kk, 0.0))
    yield

    t_l = [eye - jnp.where(ri // 2 == ci // 2, a, 0.0) for a in a_l]
    s = 2
    while s < tseq:
        off_mask = (ri // (2 * s) == ci // (2 * s)) & (ri % (2 * s) >= s) & (ci % (2 * s) < s)
        t_off = [_mm(t, jnp.where(off_mask, a, 0.0)) for t, a in zip(t_l, a_l)]
        t_l = [t - _mm(to, t) for t, to in zip(t_l, t_off)]
        s *= 2
        yield

    eg_l = [jnp.exp(g) for g in gcol_l]
    sol_l = [_mm(t, jnp.concatenate([b * v, (b * e) * k], axis=1))
             for t, b, v, e, k in zip(t_l, beta_l, v_l, eg_l, k_l)]
    qk_l = [(lax.dot_general(qb, kb, _NT, preferred_element_type=F32) * d).astype(BF16)
            for qb, kb, d in zip(qb_l, kb_l, decay_l)]
    qg_l = [(q * e).astype(BF16) for q, e in zip(q_l, eg_l)]
    kd_l = [(k * jnp.exp(gt - g)).astype(BF16) for k, gt, g in zip(k_l, gtot_l, gcol_l)]
    yield

    w_l, o_l, s_old = [], [], []
    for i, (sub, h) in enumerate(pairs):
        u_c, wkb = sol_l[i][:, :HEAD_DIM], sol_l[i][:, HEAD_DIM:].astype(BF16)
        w_parts, o_parts, s_mats = [], [], []
        for sq in range(nseq_chunk):
            rq = slice(sq * tseq, (sq + 1) * tseq)
            s_mat = state[h] if long_seq else s0_ref[sub * nseq_chunk + sq, h]
            sb = s_mat.astype(BF16)
            s_mats.append(s_mat)
            w_parts.append(u_c[rq] - jnp.dot(wkb[rq], sb, preferred_element_type=F32))
            o_parts.append(jnp.dot(qg_l[i][rq], sb, preferred_element_type=F32))
        w_l.append((jnp.concatenate(w_parts, axis=0) if nseq_chunk > 1 else w_parts[0]).astype(BF16))
        o_l.append(jnp.concatenate(o_parts, axis=0) if nseq_chunk > 1 else o_parts[0])
        s_old.append(s_mats)
    yield

    for i, (sub, h) in enumerate(pairs):
        w = w_l[i]
        for sq in range(nseq_chunk):
            rq = slice(sq * tseq, (sq + 1) * tseq)
            g_last = jnp.exp(gtot_l[i][sq * tseq:sq * tseq + 1])
            s_upd = g_last * s_old[i][sq] + lax.dot_general(kd_l[i][rq], w[rq], _TN, preferred_element_type=F32)
            if long_seq:
                state[h] = s_upd
            else:
                s_ref[sub * nseq_chunk + sq, h] = s_upd
        o = o_l[i] + jnp.dot(qk_l[i], w, preferred_element_type=F32)
        o = o * lax.rsqrt(jnp.mean(o * o, axis=-1, keepdims=True) + EPS) * nw_ref[...]
        rs = slice(sub * DN_CHUNK, (sub + 1) * DN_CHUNK)
        hs = slice(h * HEAD_DIM, (h + 1) * HEAD_DIM)
        z_h = z_all[rs, hs]
        o_ref[rs, hs] = (o * (z_h * jax.nn.sigmoid(z_h))).astype(o_ref.dtype)


def _delta_call(proj3, col0, ba, o_prev, prev, s0, conv_w, alog, dtb, norm_w, l, *, long_seq, tseq, n_outer, n_inner,
                group0, state_layer, n_sub, s_stack=None):
    c = conv_w.shape[-1] // 3
    heads = c // HEAD_DIM
    blk_rows = n_sub * DN_CHUNK
    g_n = blk_rows // SUBLANES
    m = proj3.shape[0] * SUBLANES
    nseq_blk = 1 if long_seq else blk_rows // tseq
    nseq = n_outer * nseq_blk
    gb0 = group0 // g_n

    def x_spec(col):
        return pl.BlockSpec((g_n, SUBLANES, c), lambda i, t: (gb0 + i * n_inner + t, 0, col))

    if state_layer is None:
        p_spec = pl.BlockSpec((nseq_blk, SUBLANES, 3 * c), lambda i, t: (i, 0, 0))
        s_spec = pl.BlockSpec((nseq_blk, heads, HEAD_DIM, HEAD_DIM), lambda i, t: (i, 0, 0, 0))
    else:
        p_spec = pl.BlockSpec((None, nseq_blk, SUBLANES, 3 * c), lambda i, t: (state_layer, i, 0, 0))
        s_spec = pl.BlockSpec((None, nseq_blk, heads, HEAD_DIM, HEAD_DIM), lambda i, t: (state_layer, i, 0, 0, 0))
    in_specs = [x_spec(col0), x_spec(col0 + 1), x_spec(col0 + 2), x_spec(col0 + 3),
                pl.BlockSpec((blk_rows, 2 * HEAD_DIM), lambda i, t: (gb0 + i * n_inner + t, 0)),
                p_spec, s_spec,
                _layer(l, (CONV_WIDTH, 3 * c)), _layer(l, (1, HEAD_DIM)), _layer(l, (1, HEAD_DIM)),
                _layer(l, (1, HEAD_DIM))]
    args = [proj3, proj3, proj3, proj3, ba, prev, s0, conv_w, alog, dtb, norm_w]
    aliases = {}
    if o_prev is not None:
        in_specs.append(pl.BlockSpec(memory_space=pl.ANY))
        args.append(o_prev)
        aliases = {len(args) - 1: 0}
    s_block = (nseq_blk, heads, HEAD_DIM, HEAD_DIM)
    if s_stack is None:
        s_out_spec = pl.BlockSpec(s_block, lambda i, t: (i, 0, 0, 0))
        s_out_shape = jax.ShapeDtypeStruct((nseq,) + s_block[1:], F32)
    else:
        depth, stacked = s_stack
        s_out_spec = pl.BlockSpec((None,) + s_block, lambda i, t: (l, i, 0, 0, 0))
        s_out_shape = jax.ShapeDtypeStruct((depth, nseq) + s_block[1:], F32)
        if stacked is not None:
            in_specs.append(pl.BlockSpec(memory_space=pl.ANY))
            args.append(stacked)
            aliases[len(args) - 1] = 1
    return pl.pallas_call(
        functools.partial(_delta_body, long_seq=long_seq, tseq=tseq, n_aliased=len(aliases)),
        grid=(n_outer, n_inner),
        in_specs=in_specs,
        out_specs=[pl.BlockSpec((blk_rows, c), lambda i, t: (gb0 + i * n_inner + t, 0)),
                   s_out_spec,
                   pl.BlockSpec((nseq_blk, SUBLANES, 3 * c), lambda i, t: (i, 0, 0))],
        out_shape=[jax.ShapeDtypeStruct((m, c), BF16),
                   s_out_shape,
                   jax.ShapeDtypeStruct((nseq, SUBLANES, 3 * c), F32)],
        scratch_shapes=[pltpu.VMEM((SUBLANES, 3 * c), F32)],
        input_output_aliases=aliases,
        compiler_params=_params("parallel", "arbitrary"),
        name="delta_long" if long_seq else "delta_short",
    )(*args)


def _pad_front(a, rows):
    pad = [(0, 0)] * a.ndim
    pad[-2] = (rows - a.shape[-2], 0)
    return jnp.pad(a, pad)


def _block_diag_tiles(w):
    depth, nb, b, _ = w.shape
    per = GATE_TILE // b
    w5 = w.reshape(depth, nb // per, per, b, b)
    eye = jnp.eye(per, dtype=w.dtype)
    t = jnp.einsum('lijcd,jk->lijckd', w5, eye)
    return t.reshape(depth, nb // per, GATE_TILE, GATE_TILE).astype(BF16)


def kernel(x_prompt, x_sample, state_lru_h, state_lru_conv, state_dn_s, state_dn_conv, state_pool, norm_mix, w_in, lru_conv_w, lru_conv_b, lru_gate_a_w, lru_gate_a_b, lru_gate_x_w, lru_gate_x_b, lru_lambda, dn_conv_w, dn_a_log, dn_dt_bias, dn_norm_w, pool_w, pool_scale, w_br_lru, w_br_dn, w_br_pool, w_out, norm_mlp, w_up, w_down, norm_final):
    bp, tp, d = x_prompt.shape
    bs, ts, _ = x_sample.shape
    depth = w_in.shape[0]
    half = d // 2
    heads = half // HEAD_DIM
    mp, ms = bp * tp, bs * ts
    m = mp + ms
    assert ts == SUBLANES and tp % DN_CHUNK == 0 and bs % (DN_CHUNK // ts) == 0
    assert lru_gate_a_w.shape[-1] == LRU_BLOCK and half % GATE_TILE == 0
    assert state_pool.shape[-2] == POOL_BUF

    n_pre = 6 * half
    ba0 = n_pre
    pool0 = ba0 + 2 * heads
    gates0 = pool0 + half
    w_in_t = jnp.swapaxes(w_in, 1, 2)
    n_gate_tiles, n_stream_tiles = 3 * d // half, n_pre // half + 1
    assert gates0 % SUBLANES == 0 and pool0 % SUBLANES == 0

    def gate_row0(j):
        return pl.multiple_of(gates0 + j * half, SUBLANES)

    def stream_row0(j):
        return pl.multiple_of(jnp.where(j < n_stream_tiles - 1, j * half, pool0), SUBLANES)

    row_fill = ((0, 0), (0, HEAD_DIM - heads), (0, 0))
    w_ba = jnp.concatenate([jnp.pad(w_in_t[:, ba0:ba0 + heads], row_fill),
                            jnp.pad(w_in_t[:, ba0 + heads:ba0 + 2 * heads], row_fill)], axis=1)
    col0 = 0

    def row(a):
        return a.reshape(depth, 1, a.shape[-1])

    def lane_pad(a):
        return jnp.pad(a, ((0, 0), (0, HEAD_DIM - a.shape[-1]))).reshape(depth, 1, HEAD_DIM)

    lru_w = (lru_conv_w, row(lru_conv_b), _block_diag_tiles(lru_gate_a_w), row(lru_gate_a_b),
             _block_diag_tiles(lru_gate_x_w), row(lru_gate_x_b), row(lru_lambda))
    alog, dtb, dn_nw = lane_pad(dn_a_log), lane_pad(dn_dt_bias), row(dn_norm_w)
    pool_wb, pool_sc = pool_w.astype(BF16), row(pool_scale)
    w_br_lru_b, w_br_dn_b, w_br_pool_b = w_br_lru.astype(BF16), w_br_dn.astype(BF16), w_br_pool.astype(BF16)
    w_out_b, w_down_b = w_out.astype(BF16), w_down.astype(BF16)
    g_mix, g_mlp = row(norm_mix), row(norm_mlp)

    s_lru_h0 = state_lru_h.reshape(depth, bs, 1, half)
    s_lru_prev = _pad_front(state_lru_conv, SUBLANES)
    s_dn_prev = _pad_front(state_dn_conv, SUBLANES)
    s_pool_hist = _pad_front(state_pool, 2 * SUBLANES)
    p_lru_h0 = jnp.zeros((bp, 1, half), F32)
    p_lru_prev = jnp.zeros((bp, SUBLANES, half), F32)
    p_dn_prev = jnp.zeros((bp, SUBLANES, 3 * half), F32)
    p_dn_s0 = jnp.zeros((bp, heads, HEAD_DIM, HEAD_DIM), F32)
    p_pool_hist = jnp.zeros((bp, 2 * SUBLANES, half), F32)

    x = jnp.concatenate([x_prompt.reshape(mp, d), x_sample.reshape(ms, d)], axis=0)

    tm = 512 if (mp % 512 == 0 and ms % 512 == 0) else 256
    tm_big = 1024 if (mp % 1024 == 0 and ms % 1024 == 0) else tm
    lru_g = 32
    assert (tp // SUBLANES) % lru_g == 0 and bs % lru_g == 0 and m % tm == 0 and mp % tm == 0
    n_t = tp // SUBLANES // lru_g
    long_kw = dict(long_seq=True, n_outer=bp, n_inner=n_t, g_n=lru_g, group0=0, state_layer=None)
    short_kw = dict(long_seq=False, n_outer=bs // lru_g, n_inner=1, g_n=lru_g, group0=mp // SUBLANES)
    seq_per_blk = DN_CHUNK // ts
    dn_sub, dn_sub_s = 4, 2
    assert tp % (dn_sub * DN_CHUNK) == 0 and bs % (dn_sub_s * seq_per_blk) == 0

    p_states, s_states = [], []
    s_dn_stack = None
    xn, ba = _norm_cast(x, g_mix, w_ba, 0, tm=tm)
    for l in range(depth):
        w_rows = (None, pl.Element(half), pl.Element(d))
        gates = _stream_matmul(xn, w_in_t, pl.BlockSpec(w_rows, lambda j, i, l=l: (l, gate_row0(j), 0)), (half, d),
                               n_gate_tiles, tm=tm_big, tn=half, out_dtype=BF16, w_rows_are_outputs=True,
                               act="sigmoid")
        proj = _stream_matmul(xn, w_in_t, pl.BlockSpec(w_rows, lambda j, i, l=l: (l, stream_row0(j), 0)), (half, d),
                              n_stream_tiles, tm=tm_big, tn=half, out_dtype=F32, w_rows_are_outputs=True)
        proj3 = proj.reshape(m // SUBLANES, SUBLANES, proj.shape[-1])

        o_lru, p_h, p_cb = _lru_call(proj3, col0, None, p_lru_prev, p_lru_h0, lru_w, l, **long_kw)
        o_lru, s_h, s_cb = _lru_call(proj3, col0, o_lru, s_lru_prev, s_lru_h0, lru_w, l, state_layer=l, **short_kw)

        o_dn, p_s, p_db = _delta_call(proj3, col0 + 2, ba, None, p_dn_prev, p_dn_s0, dn_conv_w, alog, dtb, dn_nw, l,
                                      long_seq=True, tseq=DN_CHUNK, n_outer=bp, n_inner=tp // (dn_sub * DN_CHUNK),
                                      group0=0, state_layer=None, n_sub=dn_sub)
        o_dn, s_dn_stack, s_db = _delta_call(proj3, col0 + 2, ba, o_dn, s_dn_prev, state_dn_s, dn_conv_w, alog, dtb,
                                             dn_nw, l, long_seq=False, tseq=ts, n_outer=bs // (dn_sub_s * seq_per_blk),
                                             n_inner=1, group0=mp // SUBLANES, state_layer=l, n_sub=dn_sub_s,
                                             s_stack=(depth, s_dn_stack))

        o_pool, p_ph = _pool_call(proj3, col0 + 6, None, p_pool_hist, pool_wb, pool_sc, l, past_len=0, **long_kw)
        o_pool, s_ph = _pool_call(proj3, col0 + 6, o_pool, s_pool_hist, pool_wb, pool_sc, l, past_len=PAST_LEN,
                                  state_layer=l, **short_kw)

        merged = _merge(o_lru, o_dn, o_pool, gates, w_br_lru_b, w_br_dn_b, w_br_pool_b, l, tm=512, tn=d)
        x, xn = _matmul_res(merged, w_out_b, l, x, tm=tm, tk=d, gain=g_mlp, gain_layer=l)
        w_up_spec = pl.BlockSpec((None, d, half), lambda j, i, l=l: (l, 0, j))
        hm = _stream_matmul(xn, w_up, w_up_spec, (d, half), w_up.shape[-1] // half, tm=tm_big, tn=half,
                            out_dtype=BF16, w_rows_are_outputs=False, act="relu2")
        if l + 1 < depth:
            x, xn, ba = _matmul_res(hm, w_down_b, l, x, tm=tm, tk=d, gain=g_mix, gain_layer=l + 1, w_small=w_ba)
        else:
            x, = _matmul_res(hm, w_down_b, l, x, tm=tm, tk=d)
        p_states.append((p_h, p_cb, p_s, p_db, p_ph))
        s_states.append((s_h, s_cb, None, s_db, s_ph))

    y_prompt = _final_norm(x, norm_final.reshape(1, d), 0, mp, tm=tm).reshape(bp, tp, d)
    y_sample = _final_norm(x, norm_final.reshape(1, d), mp, ms, tm=tm).reshape(bs, ts, d)

    def collect(states, nseq, sm=None):
        tail = CONV_WIDTH - 1
        h = jnp.stack([s[0] for s in states]).reshape(depth, nseq, half)
        cb = jnp.stack([s[1] for s in states])[:, :, SUBLANES - tail:]
        if sm is None:
            sm = jnp.stack([s[2] for s in states])
        db = jnp.stack([s[3] for s in states])[:, :, SUBLANES - tail:]
        ph = jnp.stack([s[4] for s in states])[:, :, 2 * SUBLANES - POOL_BUF:]
        return h, cb, sm, db, ph

    return (y_prompt, y_sample) + collect(p_states, bp) + collect(s_states, bs, s_dn_stack)
```

```python
import functools

import jax
import jax.numpy as jnp
from jax import lax
from jax.experimental import pallas as pl
from jax.experimental.pallas import tpu as pltpu

F32 = jnp.float32
BF16 = jnp.bfloat16

EPS = 1e-6
LRU_C = 8.0
CONV_WIDTH = 4
HEAD_DIM = 128
LRU_BLOCK = 64
POOL_WINDOWS = (2, 4, 8, 16)
POOL_BUF = 15
DN_CHUNK = 64
PAST_LEN = 16384
SUBLANES = 8
GATE_TILE = 256
VMEM_LIMIT = 56 * 1024 * 1024

_NT = (((1,), (1,)), ((), ()))
_TN = (((0,), (0,)), ((), ()))
_HI = lax.Precision.HIGHEST


def _params(*sem):
    return pltpu.CompilerParams(dimension_semantics=sem, vmem_limit_bytes=VMEM_LIMIT)


def _full(shape):
    return pl.BlockSpec(shape, lambda *_: (0,) * len(shape))


def _layer(l, shape):
    return pl.BlockSpec((None,) + shape, lambda *_: (l,) + (0,) * len(shape))


def _rms(x, gain):
    return x * lax.rsqrt(jnp.mean(x * x, axis=-1, keepdims=True) + EPS) * gain


def _norm_cast_body(x_ref, g_ref, ws_ref, xn_ref, os_ref):
    xn = _rms(x_ref[...], g_ref[...]).astype(BF16)
    xn_ref[...] = xn
    os_ref[...] = lax.dot_general(xn, ws_ref[...].astype(BF16), _NT, preferred_element_type=F32)


def _norm_cast(x, gain, w_small, l, *, tm):
    m, d = x.shape
    ns = w_small.shape[-2]
    return pl.pallas_call(
        _norm_cast_body,
        grid=(m // tm,),
        in_specs=[pl.BlockSpec((tm, d), lambda i: (i, 0)), _layer(l, (1, d)), _layer(l, (ns, d))],
        out_specs=[pl.BlockSpec((tm, d), lambda i: (i, 0)), pl.BlockSpec((tm, ns), lambda i: (i, 0))],
        out_shape=[jax.ShapeDtypeStruct((m, d), BF16), jax.ShapeDtypeStruct((m, ns), F32)],
        compiler_params=_params("parallel"),
        name="norm_cast",
    )(x, gain, w_small)


def _stream_matmul_body(*refs, w_rows_are_outputs, act, cast_blocks):
    n_cast = len(cast_blocks)
    x_ref, w_ref = refs[:2]
    src_refs = refs[2:2 + n_cast]
    o_ref = refs[2 + n_cast]
    dst_refs = refs[3 + n_cast:3 + 2 * n_cast]
    wb_ref = refs[3 + 2 * n_cast]

    @pl.when(pl.program_id(1) == 0)
    def _():
        wb_ref[...] = w_ref[...].astype(BF16)

    step = pl.program_id(0) * pl.num_programs(1) + pl.program_id(1)
    for src, dst, nblk in zip(src_refs, dst_refs, cast_blocks):
        @pl.when(step < nblk)
        def _(src=src, dst=dst):
            dst[...] = src[...].astype(BF16)

    dims = _NT if w_rows_are_outputs else (((1,), (0,)), ((), ()))
    acc = lax.dot_general(x_ref[...], wb_ref[...], dims, preferred_element_type=F32)
    if act == "relu2":
        acc = jnp.square(jnp.maximum(acc, 0.0))
    elif act == "sigmoid":
        acc = jax.nn.sigmoid(acc)
    o_ref[...] = acc.astype(o_ref.dtype)


def _stream_matmul(x, w, w_spec, w_block, n_tiles, *, tm, tn, out_dtype, w_rows_are_outputs, act=None, casts=()):
    m, d = x.shape
    m_tiles = m // tm
    in_specs = [pl.BlockSpec((tm, d), lambda j, i: (i, 0)), w_spec]
    out_specs = [pl.BlockSpec((tm, tn), lambda j, i: (i, j))]
    out_shape = [jax.ShapeDtypeStruct((m, n_tiles * tn), out_dtype)]
    cast_blocks = []
    for src, layer, rps in casts:
        _, rows, cols = src.shape
        nblk = rows // rps
        assert rows % rps == 0 and nblk <= n_tiles * m_tiles
        cast_blocks.append(nblk)
        in_specs.append(pl.BlockSpec((None, rps, cols),
                                     lambda j, i, layer=layer, nblk=nblk: (layer, jnp.minimum(j * m_tiles + i, nblk - 1), 0)))
        out_specs.append(pl.BlockSpec((None, rps, cols),
                                      lambda j, i, nblk=nblk: (0, jnp.minimum(j * m_tiles + i, nblk - 1), 0)))
        out_shape.append(jax.ShapeDtypeStruct((1, rows, cols), BF16))
    res = pl.pallas_call(
        functools.partial(_stream_matmul_body, w_rows_are_outputs=w_rows_are_outputs, act=act,
                          cast_blocks=tuple(cast_blocks)),
        grid=(n_tiles, m_tiles),
        in_specs=in_specs, out_specs=out_specs, out_shape=out_shape,
        scratch_shapes=[pltpu.VMEM(w_block, BF16)],
        compiler_params=_params("arbitrary", "arbitrary"),
        name="stream_matmul_t" if w_rows_are_outputs else "stream_matmul",
    )(x, w, *[c[0] for c in casts])
    return res if casts else res[0]


def _matmul_res_body(*refs, n_k, emit_norm, emit_small):
    a_ref, w_ref, r_ref = refs[:3]
    refs = refs[3:]
    if emit_norm:
        g_ref = refs[0]
        refs = refs[1:]
    if emit_small:
        ws_ref = refs[0]
        refs = refs[1:]
    o_ref = refs[0]
    k = pl.program_id(1)

    def step(first, last):
        acc = (r_ref[...] if first else o_ref[...]) + jnp.dot(a_ref[...], w_ref[...], preferred_element_type=F32)
        o_ref[...] = acc
        if last and emit_norm:
            xn = _rms(acc, g_ref[...]).astype(BF16)
            refs[1][...] = xn
            if emit_small:
                refs[2][...] = lax.dot_general(xn, ws_ref[...].astype(BF16), _NT, preferred_element_type=F32)

    if n_k == 1:
        step(True, True)
    else:
        pl.when(k == 0)(lambda: step(True, False))
        pl.when(k == n_k - 1)(lambda: step(False, True))
        if n_k > 2:
            pl.when((k > 0) & (k < n_k - 1))(lambda: step(False, False))


def _matmul_res(a, w, l, res, *, tm, tk, gain=None, gain_layer=0, w_small=None):
    m, k = a.shape
    d = w.shape[-1]
    n_k = k // tk
    emit_norm = gain is not None
    emit_small = w_small is not None
    in_specs = [pl.BlockSpec((tm, tk), lambda i, kk: (i, kk)),
                pl.BlockSpec((None, tk, d), lambda i, kk: (l, kk, 0)),
                pl.BlockSpec((tm, d), lambda i, kk: (i, 0))]
    out_specs = [pl.BlockSpec((tm, d), lambda i, kk: (i, 0))]
    out_shape = [jax.ShapeDtypeStruct((m, d), F32)]
    args = [a, w, res]
    if emit_norm:
        in_specs.append(_layer(gain_layer, (1, d)))
        args.append(gain)
        out_specs.append(pl.BlockSpec((tm, d), lambda i, kk: (i, 0)))
        out_shape.append(jax.ShapeDtypeStruct((m, d), BF16))
    if emit_small:
        ns = w_small.shape[-2]
        in_specs.append(_layer(gain_layer, (ns, d)))
        args.append(w_small)
        out_specs.append(pl.BlockSpec((tm, ns), lambda i, kk: (i, 0)))
        out_shape.append(jax.ShapeDtypeStruct((m, ns), F32))
    return pl.pallas_call(
        functools.partial(_matmul_res_body, n_k=n_k, emit_norm=emit_norm, emit_small=emit_small),
        grid=(m // tm, n_k),
        in_specs=in_specs, out_specs=out_specs, out_shape=out_shape,
        compiler_params=_params("parallel", "arbitrary"),
        name="matmul_res",
    )(*args)


def _merge_body(ol_ref, od_ref, op_ref, gl_ref, gd_ref, gp_ref, wl_ref, wd_ref, wp_ref, o_ref):
    def branch(o_ref_, g_ref_, w_ref_):
        return g_ref_[...].astype(F32) * jnp.dot(o_ref_[...], w_ref_[...], preferred_element_type=F32)

    m = branch(ol_ref, gl_ref, wl_ref) + branch(od_ref, gd_ref, wd_ref) + branch(op_ref, gp_ref, wp_ref)
    o_ref[...] = m.astype(o_ref.dtype)


def _merge(o_lru, o_dn, o_pool, gates, w_lru, w_dn, w_pool, l, *, tm, tn):
    m, half = o_lru.shape
    d = w_lru.shape[-1]
    nj = d // tn
    g0 = 0

    def o_spec():
        return pl.BlockSpec((tm, half), lambda i, j: (i, 0))

    def g_spec(b):
        return pl.BlockSpec((tm, tn), lambda i, j: (i, g0 + b * nj + j))

    def w_spec():
        return pl.BlockSpec((None, half, tn), lambda i, j: (l, 0, j))

    return pl.pallas_call(
        _merge_body,
        grid=(m // tm, nj),
        in_specs=[o_spec(), o_spec(), o_spec(), g_spec(0), g_spec(1), g_spec(2), w_spec(), w_spec(), w_spec()],
        out_specs=pl.BlockSpec((tm, tn), lambda i, j: (i, j)),
        out_shape=jax.ShapeDtypeStruct((m, d), BF16),
        compiler_params=_params("parallel", "arbitrary"),
        name="merge",
    )(o_lru, o_dn, o_pool, gates, gates, gates, w_lru, w_dn, w_pool)


def _rmsnorm_body(x_ref, g_ref, o_ref):
    x = x_ref[...]
    inv = lax.rsqrt(jnp.mean(x * x, axis=-1, keepdims=True) + EPS)
    o_ref[...] = x * inv * g_ref[...]


def _final_norm(x, gain, row0, rows, *, tm):
    d = x.shape[-1]
    b0 = row0 // tm
    return pl.pallas_call(
        _rmsnorm_body,
        grid=(rows // tm,),
        in_specs=[pl.BlockSpec((tm, d), lambda i: (b0 + i, 0)), _full((1, d))],
        out_specs=pl.BlockSpec((tm, d), lambda i: (i, 0)),
        out_shape=jax.ShapeDtypeStruct((rows, d), F32),
        compiler_params=_params("parallel"),
        name="final_norm",
    )(x, gain)


def _softplus(x):
    return jnp.maximum(x, 0.0) + jnp.log1p(jnp.exp(-jnp.abs(x)))


def _shift_rows(cur, prev, k, chained=False):
    ax = cur.ndim - 2
    row = lax.broadcasted_iota(jnp.int32, cur.shape, ax)
    if chained:
        rolled = pltpu.roll(jnp.concatenate([prev[None], cur], axis=0), k, ax)
        return jnp.where(row >= k, rolled[1:], rolled[:-1])
    return jnp.where(row >= k, pltpu.roll(cur, k, ax), pltpu.roll(prev, k, ax))


def _causal_conv(x, prev, cw, chained):
    y = cw[CONV_WIDTH - 1:CONV_WIDTH] * x
    for k in range(1, CONV_WIDTH):
        y = y + cw[CONV_WIDTH - 1 - k:CONV_WIDTH - k] * _shift_rows(x, prev, k, chained)
    return y


def _lru_body(*refs, long_seq):
    (x_ref, gate_ref, prev_ref, h0_ref, cw_ref, cb_ref, wa_ref, ba_ref, wx_ref, bx_ref, lam_ref) = refs[:11]
    refs = refs[11:]
    if not long_seq:
        refs = refs[1:]
    o_ref, hl_ref, nb_ref, a_scr, b_scr, h_scr, cx_scr, ch_scr = refs
    g_n, _, c = x_ref.shape
    rows = g_n * SUBLANES
    x = x_ref[...]
    if long_seq:
        @pl.when(pl.program_id(1) == 0)
        def _():
            cx_scr[...] = prev_ref[0]
            ch_scr[...] = h0_ref[0]

        prev = cx_scr[...]
    else:
        prev = prev_ref[...]
    xc = _causal_conv(x, prev, cw_ref[...], long_seq) + cb_ref[...]
    if long_seq:
        cx_scr[...] = x[g_n - 1]
        nb_ref[0] = x[g_n - 1]
    else:
        nb_ref[...] = x

    xc2 = xc.reshape(rows, c)
    xb = xc2.astype(BF16)
    nblk = c // GATE_TILE

    def gate(w_ref, b_ref):
        parts = [jnp.dot(xb[:, i * GATE_TILE:(i + 1) * GATE_TILE], w_ref[i], preferred_element_type=F32)
                 for i in range(nblk)]
        return jax.nn.sigmoid(jnp.concatenate(parts, axis=1) + b_ref[...])

    r = gate(wa_ref, ba_ref)
    i_g = gate(wx_ref, bx_ref)
    log_a = (-LRU_C) * r * _softplus(-lam_ref[...])
    a = jnp.exp(log_a)
    one_minus_a2 = -jnp.tanh(log_a) * (a * a + 1.0)
    b = jnp.sqrt(one_minus_a2) * (i_g * xc2)

    a3 = a.reshape(g_n, SUBLANES, c)
    b3 = b.reshape(g_n, SUBLANES, c)
    row = lax.broadcasted_iota(jnp.int32, a3.shape, 1)
    s = 1
    while s < SUBLANES:
        m = row >= s
        a_sh = pltpu.roll(a3, s, 1)
        b_sh = pltpu.roll(b3, s, 1)
        b3 = jnp.where(m, a3 * b_sh + b3, b3)
        a3 = jnp.where(m, a3 * a_sh, a3)
        s *= 2
    a_scr[...] = a3
    b_scr[...] = b3

    def chain(g, h_in):
        if not long_seq:
            h_in = h0_ref[g]
        hg = a_scr[g] * h_in + b_scr[g]
        h_scr[g] = hg
        h_out = hg[SUBLANES - 1:SUBLANES]
        if not long_seq:
            hl_ref[g] = h_out
        return h_out

    h_init = ch_scr[...] if long_seq else jnp.zeros((1, c), F32)
    h_fin = lax.fori_loop(0, g_n, chain, h_init)
    if long_seq:
        ch_scr[...] = h_fin
        hl_ref[0] = h_fin
    h = h_scr[...].reshape(rows, c)
    gl = gate_ref[...].reshape(rows, c)
    o_ref[...] = (h * jax.nn.gelu(gl)).astype(o_ref.dtype)


def _lru_call(proj3, col0, o_prev, prev, h0, weights, l, *, long_seq, n_outer, n_inner, g_n, group0, state_layer):
    cw, cb, wa, ba, wx, bx, lam = weights
    c = cw.shape[-1]
    m = proj3.shape[0] * SUBLANES
    nseq_blk = 1 if long_seq else g_n
    nseq = n_outer * nseq_blk
    gb0 = group0 // g_n

    def x_spec(col):
        return pl.BlockSpec((g_n, SUBLANES, c), lambda i, t: (gb0 + i * n_inner + t, 0, col))

    def state_spec(r):
        if state_layer is None:
            return pl.BlockSpec((nseq_blk, r, c), lambda i, t: (i, 0, 0))
        return pl.BlockSpec((None, nseq_blk, r, c), lambda i, t: (state_layer, i, 0, 0))

    ng = c // GATE_TILE
    in_specs = [x_spec(col0), x_spec(col0 + 1), state_spec(SUBLANES), state_spec(1),
                _layer(l, (CONV_WIDTH, c)), _layer(l, (1, c)),
                _layer(l, (ng, GATE_TILE, GATE_TILE)), _layer(l, (1, c)),
                _layer(l, (ng, GATE_TILE, GATE_TILE)), _layer(l, (1, c)), _layer(l, (1, c))]
    args = [proj3, proj3, prev, h0, cw, cb, wa, ba, wx, bx, lam]
    aliases = {}
    if o_prev is not None:
        in_specs.append(pl.BlockSpec(memory_space=pl.ANY))
        args.append(o_prev)
        aliases = {len(args) - 1: 0}
    out_specs = [pl.BlockSpec((g_n * SUBLANES, c), lambda i, t: (gb0 + i * n_inner + t, 0)),
                 pl.BlockSpec((nseq_blk, 1, c), lambda i, t: (i, 0, 0)),
                 pl.BlockSpec((nseq_blk, SUBLANES, c), lambda i, t: (i, 0, 0))]
    out_shape = [jax.ShapeDtypeStruct((m, c), BF16),
                 jax.ShapeDtypeStruct((nseq, 1, c), F32),
                 jax.ShapeDtypeStruct((nseq, SUBLANES, c), F32)]
    grp = (g_n, SUBLANES, c)
    return pl.pallas_call(
        functools.partial(_lru_body, long_seq=long_seq),
        grid=(n_outer, n_inner),
        in_specs=in_specs, out_specs=out_specs, out_shape=out_shape,
        scratch_shapes=[pltpu.VMEM(grp, F32), pltpu.VMEM(grp, F32), pltpu.VMEM(grp, F32),
                        pltpu.VMEM((SUBLANES, c), F32), pltpu.VMEM((1, c), F32)],
        input_output_aliases=aliases,
        compiler_params=_params("parallel", "arbitrary"),
        name="lru_long" if long_seq else "lru_short",
    )(*args)


def _pool_body(*refs, long_seq, past_len):
    u_ref, hist_ref, pw_ref, ps_ref = refs[:4]
    refs = refs[4:]
    if not long_seq:
        refs = refs[1:]
    o_ref, nh_ref, c_scr = refs
    g_n, _, c = u_ref.shape
    rows = g_n * SUBLANES
    x = u_ref[...]
    t = pl.program_id(1)
    if long_seq:
        @pl.when(t == 0)
        def _():
            c_scr[0] = hist_ref[0, 0:SUBLANES]
            c_scr[1] = hist_ref[0, SUBLANES:2 * SUBLANES]

        ext = jnp.concatenate([c_scr[...], x], axis=0)
        gs = 1
    else:
        hist = hist_ref[...]
        ext = jnp.concatenate([hist[:, 0:SUBLANES], hist[:, SUBLANES:2 * SUBLANES], x], axis=0)
        gs = g_n
    nh_ref[:, 0:SUBLANES] = ext[-2 * gs:-gs]
    nh_ref[:, SUBLANES:2 * SUBLANES] = ext[-gs:]
    if long_seq:
        c_scr[0] = ext[g_n]
        c_scr[1] = ext[g_n + 1]

    def prev(a):
        return jnp.concatenate([a[:gs], a[:-gs]], axis=0)

    n_grp = len(POOL_WINDOWS)
    pg = c // n_grp
    i0 = lax.broadcasted_iota(jnp.int32, (g_n, SUBLANES, pg), 0)
    i1 = lax.broadcasted_iota(jnp.int32, (g_n, SUBLANES, pg), 1)
    t_abs = (t * g_n + i0) * SUBLANES + i1 if long_seq else i1
    outs = []
    for gi, w in enumerate(POOL_WINDOWS):
        eg = ext[:, :, gi * pg:(gi + 1) * pg]
        s = eg
        k = 1
        while k < w:
            if k < SUBLANES:
                rolled = pltpu.roll(s, k, 1)
                row = lax.broadcasted_iota(jnp.int32, s.shape, 1)
                s = s + jnp.where(row >= k, rolled, prev(rolled))
            else:
                s = s + prev(s)
            k *= 2
        cnt = jnp.minimum(t_abs + (1 + past_len), w).astype(F32)
        d = s[2 * gs:] / cnt - eg[2 * gs:]
        d2 = d.reshape(rows, pg).astype(BF16)
        outs.append(jnp.dot(d2, pw_ref[gi], preferred_element_type=F32))
    y = jnp.concatenate(outs, axis=1) * ps_ref[...]
    o_ref[...] = y.astype(o_ref.dtype)


def _pool_call(proj3, col, o_prev, hist, pool_w, pool_scale, l, *, long_seq, past_len, n_outer, n_inner, g_n,
               group0, state_layer):
    c = pool_scale.shape[-1]
    m = proj3.shape[0] * SUBLANES
    nseq_blk = 1 if long_seq else g_n
    nseq = n_outer * nseq_blk
    gb0 = group0 // g_n
    pg = c // len(POOL_WINDOWS)
    hrows = 2 * SUBLANES
    if state_layer is None:
        h_spec = pl.BlockSpec((nseq_blk, hrows, c), lambda i, t: (i, 0, 0))
    else:
        h_spec = pl.BlockSpec((None, nseq_blk, hrows, c), lambda i, t: (state_layer, i, 0, 0))
    in_specs = [pl.BlockSpec((g_n, SUBLANES, c), lambda i, t: (gb0 + i * n_inner + t, 0, col)), h_spec,
                _layer(l, (len(POOL_WINDOWS), pg, pg)), _layer(l, (1, c))]
    args = [proj3, hist, pool_w, pool_scale]
    aliases = {}
    if o_prev is not None:
        in_specs.append(pl.BlockSpec(memory_space=pl.ANY))
        args.append(o_prev)
        aliases = {len(args) - 1: 0}
    return pl.pallas_call(
        functools.partial(_pool_body, long_seq=long_seq, past_len=past_len),
        grid=(n_outer, n_inner),
        in_specs=in_specs,
        out_specs=[pl.BlockSpec((g_n * SUBLANES, c), lambda i, t: (gb0 + i * n_inner + t, 0)),
                   pl.BlockSpec((nseq_blk, hrows, c), lambda i, t: (i, 0, 0))],
        out_shape=[jax.ShapeDtypeStruct((m, c), BF16), jax.ShapeDtypeStruct((nseq, hrows, c), F32)],
        scratch_shapes=[pltpu.VMEM((2, SUBLANES, c), F32)],
        input_output_aliases=aliases,
        compiler_params=_params("parallel", "arbitrary"),
        name="pool_long" if long_seq else "pool_short",
    )(*args)


CHUNK_STAGE_LAG = 3
SOLVE_PASSES = 1


def _mm(a, b):
    a_hi, b_hi = a.astype(BF16), b.astype(BF16)
    out = jnp.dot(a_hi, b_hi, preferred_element_type=F32)
    if SOLVE_PASSES == 3:
        a_lo = (a - a_hi.astype(F32)).astype(BF16)
        b_lo = (b - b_hi.astype(F32)).astype(BF16)
        out = out + jnp.dot(a_hi, b_lo, preferred_element_type=F32) + jnp.dot(a_lo, b_hi, preferred_element_type=F32)
    return out


def _delta_body(*refs, long_seq, tseq, n_aliased):
    (q_ref, k_ref, v_ref, z_ref, ba_ref, prev_ref, s0_ref, cw_ref, alog_ref, dtb_ref, nw_ref) = refs[:11]
    o_ref, s_ref, nb_ref, cx_scr = refs[11 + n_aliased:]
    g_n, _, c = q_ref.shape
    rows = g_n * SUBLANES
    n_sub = rows // DN_CHUNK
    nseq_chunk = DN_CHUNK // tseq
    heads = c // HEAD_DIM

    if long_seq:
        @pl.when(pl.program_id(1) == 0)
        def _():
            cx_scr[...] = prev_ref[0]
            s_ref[...] = s0_ref[...]

    cw = cw_ref[...]
    streams = []
    for idx, x_ref in enumerate((q_ref, k_ref, v_ref)):
        x = x_ref[...]
        lo, hi = idx * c, (idx + 1) * c
        if long_seq:
            prev = cx_scr[:, lo:hi]
            cx_scr[:, lo:hi] = x[g_n - 1]
            nb_ref[0, :, lo:hi] = x[g_n - 1]
        else:
            prev = prev_ref[:, :, lo:hi]
            nb_ref[:, :, lo:hi] = x
        y = _causal_conv(x, prev, cw[:, lo:hi], long_seq)
        streams.append((y * jax.nn.sigmoid(y)).reshape(rows, c))
    q_all, k_all, v_all = streams
    z_all = z_ref[...].reshape(rows, c)

    ba = ba_ref[...]
    beta_all = jax.nn.sigmoid(ba[:, 0:HEAD_DIM])
    g_all = -jnp.exp(alog_ref[...]) * _softplus(ba[:, HEAD_DIM:2 * HEAD_DIM] + dtb_ref[...])

    ri = lax.broadcasted_iota(jnp.int32, (DN_CHUNK, DN_CHUNK), 0)
    ci = lax.broadcasted_iota(jnp.int32, (DN_CHUNK, DN_CHUNK), 1)
    same = (ri // tseq) == (ci // tseq)
    incl = same & (ci <= ri)
    strict = same & (ci < ri)
    eye = (ri == ci).astype(F32)
    incl_f, same_f = incl.astype(F32), same.astype(F32)

    state = [s_ref[0, h] for h in range(heads)] if long_seq else None
    masks = (ri, ci, incl, strict, eye, incl_f, same_f)
    waiting = [_delta_chunk([(sub, h) for h in range(heads)], state, masks,
                            (q_all, k_all, v_all, z_all, beta_all, g_all), (s0_ref, s_ref, nw_ref, o_ref),
                            long_seq, tseq) for sub in range(n_sub)]
    running, rnd = [], 0
    while waiting or running:
        if waiting and rnd % CHUNK_STAGE_LAG == 0:
            running.append(waiting.pop(0))
        for gen in list(running):
            if next(gen, "done") == "done":
                running.remove(gen)
        rnd += 1
    if long_seq:
        for h in range(heads):
            s_ref[0, h] = state[h]


def _delta_chunk(pairs, state, masks, streams, refs, long_seq, tseq):
    ri, ci, incl, strict, eye, incl_f, same_f = masks
    q_all, k_all, v_all, z_all, beta_all, g_all = streams
    s0_ref, s_ref, nw_ref, o_ref = refs
    nseq_chunk = DN_CHUNK // tseq
    gamma_s, gtot_s, gamma_ts = {}, {}, {}
    for sub in sorted({sub for sub, _ in pairs}):
        g_sub = g_all[sub * DN_CHUNK:(sub + 1) * DN_CHUNK]
        gamma = jnp.dot(incl_f, g_sub, precision=_HI, preferred_element_type=F32)
        gamma_s[sub] = gamma
        gtot_s[sub] = jnp.dot(same_f, g_sub, precision=_HI, preferred_element_type=F32)
        gamma_ts[sub] = gamma.T

    q_l, k_l, v_l, beta_l, gcol_l, gtot_l, decay_l, qb_l, kb_l, a_l = ([] for _ in range(10))
    for sub, h in pairs:
        rs = slice(sub * DN_CHUNK, (sub + 1) * DN_CHUNK)
        hs = slice(h * HEAD_DIM, (h + 1) * HEAD_DIM)
        q_h, k_h = q_all[rs, hs], k_all[rs, hs]
        q_h = q_h * lax.rsqrt(jnp.sum(q_h * q_h, axis=-1, keepdims=True) + EPS) * (HEAD_DIM ** -0.5)
        k_h = k_h * lax.rsqrt(jnp.sum(k_h * k_h, axis=-1, keepdims=True) + EPS)
        gcol = gamma_s[sub][:, h:h + 1]
        decay = jnp.where(incl, jnp.exp(gcol - gamma_ts[sub][h:h + 1, :]), 0.0)
        qb, kb = q_h.astype(BF16), k_h.astype(BF16)
        beta = beta_all[rs, h:h + 1]
        kk = lax.dot_general(kb, kb, _NT, preferred_element_type=F32)
        q_l.append(q_h)
        k_l.append(k_h)
        v_l.append(v_all[rs, hs])
        beta_l.append(beta)
        gcol_l.append(gcol)
        gtot_l.append(gtot_s[sub][:, h:h + 1])
        decay_l.append(decay)
        qb_l.append(qb)
        kb_l.append(kb)
        a_l.append(jnp.where(strict, beta * decay * kk, 0.0))
    yield

    t_l = [eye - jnp.where(ri // 2 == ci // 2, a, 0.0) for a in a_l]
    s = 2
    while s < tseq:
        off_mask = (ri // (2 * s) == ci // (2 * s)) & (ri % (2 * s) >= s) & (ci % (2 * s) < s)
        t_off = [_mm(t, jnp.where(off_mask, a, 0.0)) for t, a in zip(t_l, a_l)]
        t_l = [t - _mm(to, t) for t, to in zip(t_l, t_off)]
        s *= 2
        yield

    eg_l = [jnp.exp(g) for g in gcol_l]
    sol_l = [_mm(t, jnp.concatenate([b * v, (b * e) * k], axis=1))
             for t, b, v, e, k in zip(t_l, beta_l, v_l, eg_l, k_l)]
    qk_l = [(lax.dot_general(qb, kb, _NT, preferred_element_type=F32) * d).astype(BF16)
            for qb, kb, d in zip(qb_l, kb_l, decay_l)]
    qg_l = [(q * e).astype(BF16) for q, e in zip(q_l, eg_l)]
    kd_l = [(k * jnp.exp(gt - g)).astype(BF16) for k, gt, g in zip(k_l, gtot_l, gcol_l)]
    yield

    w_l, o_l, s_old = [], [], []
    for i, (sub, h) in enumerate(pairs):
        u_c, wkb = sol_l[i][:, :HEAD_DIM], sol_l[i][:, HEAD_DIM:].astype(BF16)
        w_parts, o_parts, s_mats = [], [], []
        for sq in range(nseq_chunk):
            rq = slice(sq * tseq, (sq + 1) * tseq)
            s_mat = state[h] if long_seq else s0_ref[sub * nseq_chunk + sq, h]
            sb = s_mat.astype(BF16)
            s_mats.append(s_mat)
            w_parts.append(u_c[rq] - jnp.dot(wkb[rq], sb, preferred_element_type=F32))
            o_parts.append(jnp.dot(qg_l[i][rq], sb, preferred_element_type=F32))
        w_l.append((jnp.concatenate(w_parts, axis=0) if nseq_chunk > 1 else w_parts[0]).astype(BF16))
        o_l.append(jnp.concatenate(o_parts, axis=0) if nseq_chunk > 1 else o_parts[0])
        s_old.append(s_mats)
    yield

    for i, (sub, h) in enumerate(pairs):
        w = w_l[i]
        for sq in range(nseq_chunk):
            rq = slice(sq * tseq, (sq + 1) * tseq)
            g_last = jnp.exp(gtot_l[i][sq * tseq:sq * tseq + 1])
            s_upd = g_last * s_old[i][sq] + lax.dot_general(kd_l[i][rq], w[rq], _TN, preferred_element_type=F32)
            if long_seq:
                state[h] = s_upd
            else:
                s_ref[sub * nseq_chunk + sq, h] = s_upd
        o = o_l[i] + jnp.dot(qk_l[i], w, preferred_element_type=F32)
        o = o * lax.rsqrt(jnp.mean(o * o, axis=-1, keepdims=True) + EPS) * nw_ref[...]
        rs = slice(sub * DN_CHUNK, (sub + 1) * DN_CHUNK)
        hs = slice(h * HEAD_DIM, (h + 1) * HEAD_DIM)
        z_h = z_all[rs, hs]
        o_ref[rs, hs] = (o * (z_h * jax.nn.sigmoid(z_h))).astype(o_ref.dtype)


def _delta_call(proj3, col0, ba, o_prev, prev, s0, conv_w, alog, dtb, norm_w, l, *, long_seq, tseq, n_outer, n_inner,
                group0, state_layer, n_sub, s_stack=None):
    c = conv_w.shape[-1] // 3
    heads = c // HEAD_DIM
    blk_rows = n_sub * DN_CHUNK
    g_n = blk_rows // SUBLANES
    m = proj3.shape[0] * SUBLANES
    nseq_blk = 1 if long_seq else blk_rows // tseq
    nseq = n_outer * nseq_blk
    gb0 = group0 // g_n

    def x_spec(col):
        return pl.BlockSpec((g_n, SUBLANES, c), lambda i, t: (gb0 + i * n_inner + t, 0, col))

    if state_layer is None:
        p_spec = pl.BlockSpec((nseq_blk, SUBLANES, 3 * c), lambda i, t: (i, 0, 0))
        s_spec = pl.BlockSpec((nseq_blk, heads, HEAD_DIM, HEAD_DIM), lambda i, t: (i, 0, 0, 0))
    else:
        p_spec = pl.BlockSpec((None, nseq_blk, SUBLANES, 3 * c), lambda i, t: (state_layer, i, 0, 0))
        s_spec = pl.BlockSpec((None, nseq_blk, heads, HEAD_DIM, HEAD_DIM), lambda i, t: (state_layer, i, 0, 0, 0))
    in_specs = [x_spec(col0), x_spec(col0 + 1), x_spec(col0 + 2), x_spec(col0 + 3),
                pl.BlockSpec((blk_rows, 2 * HEAD_DIM), lambda i, t: (gb0 + i * n_inner + t, 0)),
                p_spec, s_spec,
                _layer(l, (CONV_WIDTH, 3 * c)), _layer(l, (1, HEAD_DIM)), _layer(l, (1, HEAD_DIM)),
                _layer(l, (1, HEAD_DIM))]
    args = [proj3, proj3, proj3, proj3, ba, prev, s0, conv_w, alog, dtb, norm_w]
    aliases = {}
    if o_prev is not None:
        in_specs.append(pl.BlockSpec(memory_space=pl.ANY))
        args.append(o_prev)
        aliases = {len(args) - 1: 0}
    s_block = (nseq_blk, heads, HEAD_DIM, HEAD_DIM)
    if s_stack is None:
        s_out_spec = pl.BlockSpec(s_block, lambda i, t: (i, 0, 0, 0))
        s_out_shape = jax.ShapeDtypeStruct((nseq,) + s_block[1:], F32)
    else:
        depth, stacked = s_stack
        s_out_spec = pl.BlockSpec((None,) + s_block, lambda i, t: (l, i, 0, 0, 0))
        s_out_shape = jax.ShapeDtypeStruct((depth, nseq) + s_block[1:], F32)
        if stacked is not None:
            in_specs.append(pl.BlockSpec(memory_space=pl.ANY))
            args.append(stacked)
            aliases[len(args) - 1] = 1
    return pl.pallas_call(
        functools.partial(_delta_body, long_seq=long_seq, tseq=tseq, n_aliased=len(aliases)),
        grid=(n_outer, n_inner),
        in_specs=in_specs,
        out_specs=[pl.BlockSpec((blk_rows, c), lambda i, t: (gb0 + i * n_inner + t, 0)),
                   s_out_spec,
                   pl.BlockSpec((nseq_blk, SUBLANES, 3 * c), lambda i, t: (i, 0, 0))],
        out_shape=[jax.ShapeDtypeStruct((m, c), BF16),
                   s_out_shape,
                   jax.ShapeDtypeStruct((nseq, SUBLANES, 3 * c), F32)],
        scratch_shapes=[pltpu.VMEM((SUBLANES, 3 * c), F32)],
        input_output_aliases=aliases,
        compiler_params=_params("parallel", "arbitrary"),
        name="delta_long" if long_seq else "delta_short",
    )(*args)


def _cast_rows(w, steps):
    rows = w.shape[1]
    nblk = 1
    while nblk * 2 <= steps and rows % (nblk * 2) == 0 and rows // (nblk * 2) >= 2 * SUBLANES:
        nblk *= 2
    return rows // nblk


def _pad_front(a, rows):
    pad = [(0, 0)] * a.ndim
    pad[-2] = (rows - a.shape[-2], 0)
    return jnp.pad(a, pad)


def _block_diag_tiles(w):
    depth, nb, b, _ = w.shape
    per = GATE_TILE // b
    w5 = w.reshape(depth, nb // per, per, b, b)
    eye = jnp.eye(per, dtype=w.dtype)
    t = jnp.einsum('lijcd,jk->lijckd', w5, eye)
    return t.reshape(depth, nb // per, GATE_TILE, GATE_TILE).astype(BF16)


def kernel(x_prompt, x_sample, state_lru_h, state_lru_conv, state_dn_s, state_dn_conv, state_pool, norm_mix, w_in, lru_conv_w, lru_conv_b, lru_gate_a_w, lru_gate_a_b, lru_gate_x_w, lru_gate_x_b, lru_lambda, dn_conv_w, dn_a_log, dn_dt_bias, dn_norm_w, pool_w, pool_scale, w_br_lru, w_br_dn, w_br_pool, w_out, norm_mlp, w_up, w_down, norm_final):
    bp, tp, d = x_prompt.shape
    bs, ts, _ = x_sample.shape
    depth = w_in.shape[0]
    half = d // 2
    heads = half // HEAD_DIM
    mp, ms = bp * tp, bs * ts
    m = mp + ms
    assert ts == SUBLANES and tp % DN_CHUNK == 0 and bs % (DN_CHUNK // ts) == 0
    assert lru_gate_a_w.shape[-1] == LRU_BLOCK and half % GATE_TILE == 0
    assert state_pool.shape[-2] == POOL_BUF

    n_pre = 6 * half
    ba0 = n_pre
    pool0 = ba0 + 2 * heads
    gates0 = pool0 + half
    w_in_t = jnp.swapaxes(w_in, 1, 2)
    n_gate_tiles, n_stream_tiles = 3 * d // half, n_pre // half + 1
    assert gates0 % SUBLANES == 0 and pool0 % SUBLANES == 0

    def gate_row0(j):
        return pl.multiple_of(gates0 + j * half, SUBLANES)

    def stream_row0(j):
        return pl.multiple_of(jnp.where(j < n_stream_tiles - 1, j * half, pool0), SUBLANES)

    row_fill = ((0, 0), (0, HEAD_DIM - heads), (0, 0))
    w_ba = jnp.concatenate([jnp.pad(w_in_t[:, ba0:ba0 + heads], row_fill),
                            jnp.pad(w_in_t[:, ba0 + heads:ba0 + 2 * heads], row_fill)], axis=1)
    col0 = 0

    def row(a):
        return a.reshape(depth, 1, a.shape[-1])

    def lane_pad(a):
        return jnp.pad(a, ((0, 0), (0, HEAD_DIM - a.shape[-1]))).reshape(depth, 1, HEAD_DIM)

    lru_w = (lru_conv_w, row(lru_conv_b), _block_diag_tiles(lru_gate_a_w), row(lru_gate_a_b),
             _block_diag_tiles(lru_gate_x_w), row(lru_gate_x_b), row(lru_lambda))
    alog, dtb, dn_nw = lane_pad(dn_a_log), lane_pad(dn_dt_bias), row(dn_norm_w)
    pool_wb, pool_sc = pool_w.astype(BF16), row(pool_scale)
    g_mix, g_mlp = row(norm_mix), row(norm_mlp)

    s_lru_h0 = state_lru_h.reshape(depth, bs, 1, half)
    s_lru_prev = _pad_front(state_lru_conv, SUBLANES)
    s_dn_prev = _pad_front(state_dn_conv, SUBLANES)
    s_pool_hist = _pad_front(state_pool, 2 * SUBLANES)
    p_lru_h0 = jnp.zeros((bp, 1, half), F32)
    p_lru_prev = jnp.zeros((bp, SUBLANES, half), F32)
    p_dn_prev = jnp.zeros((bp, SUBLANES, 3 * half), F32)
    p_dn_s0 = jnp.zeros((bp, heads, HEAD_DIM, HEAD_DIM), F32)
    p_pool_hist = jnp.zeros((bp, 2 * SUBLANES, half), F32)

    x = jnp.concatenate([x_prompt.reshape(mp, d), x_sample.reshape(ms, d)], axis=0)

    tm = 512 if (mp % 512 == 0 and ms % 512 == 0) else 256
    tm_big = 1024 if (mp % 1024 == 0 and ms % 1024 == 0) else tm
    lru_g = 32
    assert (tp // SUBLANES) % lru_g == 0 and bs % lru_g == 0 and m % tm == 0 and mp % tm == 0
    n_t = tp // SUBLANES // lru_g
    long_kw = dict(long_seq=True, n_outer=bp, n_inner=n_t, g_n=lru_g, group0=0, state_layer=None)
    short_kw = dict(long_seq=False, n_outer=bs // lru_g, n_inner=1, g_n=lru_g, group0=mp // SUBLANES)
    seq_per_blk = DN_CHUNK // ts
    dn_sub, dn_sub_s = 4, 2
    assert tp % (dn_sub * DN_CHUNK) == 0 and bs % (dn_sub_s * seq_per_blk) == 0

    p_states, s_states = [], []
    s_dn_stack = None
    xn, ba = _norm_cast(x, g_mix, w_ba, 0, tm=tm)
    for l in range(depth):
        w_rows = (None, pl.Element(half), pl.Element(d))
        m_tiles = m // tm_big
        gates, w_out_b, w_br_lru_b = _stream_matmul(
            xn, w_in_t, pl.BlockSpec(w_rows, lambda j, i, l=l: (l, gate_row0(j), 0)), (half, d), n_gate_tiles,
            tm=tm_big, tn=half, out_dtype=BF16, w_rows_are_outputs=True, act="sigmoid",
            casts=[(w, l, _cast_rows(w, n_gate_tiles * m_tiles)) for w in (w_out, w_br_lru)])
        proj, w_br_dn_b, w_br_pool_b = _stream_matmul(
            xn, w_in_t, pl.BlockSpec(w_rows, lambda j, i, l=l: (l, stream_row0(j), 0)), (half, d), n_stream_tiles,
            tm=tm_big, tn=half, out_dtype=F32, w_rows_are_outputs=True,
            casts=[(w, l, _cast_rows(w, n_stream_tiles * m_tiles)) for w in (w_br_dn, w_br_pool)])
        proj3 = proj.reshape(m // SUBLANES, SUBLANES, proj.shape[-1])

        o_lru, p_h, p_cb = _lru_call(proj3, col0, None, p_lru_prev, p_lru_h0, lru_w, l, **long_kw)
        o_lru, s_h, s_cb = _lru_call(proj3, col0, o_lru, s_lru_prev, s_lru_h0, lru_w, l, state_layer=l, **short_kw)

        o_dn, p_s, p_db = _delta_call(proj3, col0 + 2, ba, None, p_dn_prev, p_dn_s0, dn_conv_w, alog, dtb, dn_nw, l,
                                      long_seq=True, tseq=DN_CHUNK, n_outer=bp, n_inner=tp // (dn_sub * DN_CHUNK),
                                      group0=0, state_layer=None, n_sub=dn_sub)
        o_dn, s_dn_stack, s_db = _delta_call(proj3, col0 + 2, ba, o_dn, s_dn_prev, state_dn_s, dn_conv_w, alog, dtb,
                                             dn_nw, l, long_seq=False, tseq=ts, n_outer=bs // (dn_sub_s * seq_per_blk),
                                             n_inner=1, group0=mp // SUBLANES, state_layer=l, n_sub=dn_sub_s,
                                             s_stack=(depth, s_dn_stack))

        o_pool, p_ph = _pool_call(proj3, col0 + 6, None, p_pool_hist, pool_wb, pool_sc, l, past_len=0, **long_kw)
        o_pool, s_ph = _pool_call(proj3, col0 + 6, o_pool, s_pool_hist, pool_wb, pool_sc, l, past_len=PAST_LEN,
                                  state_layer=l, **short_kw)

        merged = _merge(o_lru, o_dn, o_pool, gates, w_br_lru_b, w_br_dn_b, w_br_pool_b, 0, tm=512, tn=d)
        x, xn = _matmul_res(merged, w_out_b, 0, x, tm=tm, tk=d, gain=g_mlp, gain_layer=l)
        w_up_spec = pl.BlockSpec((None, d, half), lambda j, i, l=l: (l, 0, j))
        n_up_tiles = w_up.shape[-1] // half
        hm, w_down_b = _stream_matmul(xn, w_up, w_up_spec, (d, half), n_up_tiles, tm=tm_big, tn=half,
                                      out_dtype=BF16, w_rows_are_outputs=False, act="relu2",
                                      casts=[(w_down, l, _cast_rows(w_down, n_up_tiles * m_tiles))])
        if l + 1 < depth:
            x, xn, ba = _matmul_res(hm, w_down_b, 0, x, tm=tm, tk=d, gain=g_mix, gain_layer=l + 1, w_small=w_ba)
        else:
            x, = _matmul_res(hm, w_down_b, 0, x, tm=tm, tk=d)
        p_states.append((p_h, p_cb, p_s, p_db, p_ph))
        s_states.append((s_h, s_cb, None, s_db, s_ph))

    y_prompt = _final_norm(x, norm_final.reshape(1, d), 0, mp, tm=tm).reshape(bp, tp, d)
    y_sample = _final_norm(x, norm_final.reshape(1, d), mp, ms, tm=tm).reshape(bs, ts, d)

    def collect(states, nseq, sm=None):
        tail = CONV_WIDTH - 1
        h = jnp.stack([s[0] for s in states]).reshape(depth, nseq, half)
        cb = jnp.stack([s[1] for s in states])[:, :, SUBLANES - tail:]
        if sm is None:
            sm = jnp.stack([s[2] for s in states])
        db = jnp.stack([s[3] for s in states])[:, :, SUBLANES - tail:]
        ph = jnp.stack([s[4] for s in states])[:, :, 2 * SUBLANES - POOL_BUF:]
        return h, cb, sm, db, ph

    return (y_prompt, y_sample) + collect(p_states, bp) + collect(s_states, bs, s_dn_stack)
```

```python
import functools

import jax
import jax.numpy as jnp
from jax import lax
from jax.experimental import pallas as pl
from jax.experimental.pallas import tpu as pltpu

F32 = jnp.float32
BF16 = jnp.bfloat16

EPS = 1e-6
LRU_C = 8.0
CONV_WIDTH = 4
HEAD_DIM = 128
LRU_BLOCK = 64
POOL_WINDOWS = (2, 4, 8, 16)
POOL_BUF = 15
DN_CHUNK = 64
PAST_LEN = 16384
SUBLANES = 8
GATE_TILE = 256
VMEM_LIMIT = 56 * 1024 * 1024

_NT = (((1,), (1,)), ((), ()))
_TN = (((0,), (0,)), ((), ()))
_HI = lax.Precision.HIGHEST


def _params(*sem):
    return pltpu.CompilerParams(dimension_semantics=sem, vmem_limit_bytes=VMEM_LIMIT)


def _full(shape):
    return pl.BlockSpec(shape, lambda *_: (0,) * len(shape))


def _layer(l, shape):
    return pl.BlockSpec((None,) + shape, lambda *_: (l,) + (0,) * len(shape))


def _rms(x, gain):
    return x * lax.rsqrt(jnp.mean(x * x, axis=-1, keepdims=True) + EPS) * gain


def _two_source_specs(tm, c, n_first):
    first = pl.BlockSpec((tm, c), lambda i, *_: (jnp.minimum(i, n_first - 1), 0))
    second = pl.BlockSpec((tm, c), lambda i, *_: (jnp.maximum(i - n_first, 0), 0))
    return first, second


def _pick_rows(first_ref, second_ref, n_first):
    return jnp.where(pl.program_id(0) < n_first, first_ref[...], second_ref[...])


def _norm_cast_body(xa_ref, xb_ref, g_ref, ws_ref, x_ref, xn_ref, os_ref, *, n_first):
    x = _pick_rows(xa_ref, xb_ref, n_first)
    x_ref[...] = x
    xn = _rms(x, g_ref[...]).astype(BF16)
    xn_ref[...] = xn
    os_ref[...] = lax.dot_general(xn, ws_ref[...].astype(BF16), _NT, preferred_element_type=F32)


def _norm_cast(xa, xb, gain, w_small, l, *, tm):
    d = xa.shape[-1]
    m = xa.shape[0] + xb.shape[0]
    n_first = xa.shape[0] // tm
    ns = w_small.shape[-2]
    return pl.pallas_call(
        functools.partial(_norm_cast_body, n_first=n_first),
        grid=(m // tm,),
        in_specs=[*_two_source_specs(tm, d, n_first), _layer(l, (1, d)), _layer(l, (ns, d))],
        out_specs=[pl.BlockSpec((tm, d), lambda i: (i, 0)), pl.BlockSpec((tm, d), lambda i: (i, 0)),
                   pl.BlockSpec((tm, ns), lambda i: (i, 0))],
        out_shape=[jax.ShapeDtypeStruct((m, d), F32), jax.ShapeDtypeStruct((m, d), BF16),
                   jax.ShapeDtypeStruct((m, ns), F32)],
        compiler_params=_params("arbitrary"),
        name="norm_cast",
    )(xa, xb, gain, w_small)


def _stream_matmul_body(*refs, w_rows_are_outputs, act, cast_blocks):
    n_cast = len(cast_blocks)
    x_ref, w_ref = refs[:2]
    src_refs = refs[2:2 + n_cast]
    o_ref = refs[2 + n_cast]
    dst_refs = refs[3 + n_cast:3 + 2 * n_cast]
    wb_ref = refs[3 + 2 * n_cast]

    @pl.when(pl.program_id(1) == 0)
    def _():
        wb_ref[...] = w_ref[...].astype(BF16)

    step = pl.program_id(0) * pl.num_programs(1) + pl.program_id(1)
    for src, dst, nblk in zip(src_refs, dst_refs, cast_blocks):
        @pl.when(step < nblk)
        def _(src=src, dst=dst):
            dst[...] = src[...].astype(BF16)

    dims = _NT if w_rows_are_outputs else (((1,), (0,)), ((), ()))
    acc = lax.dot_general(x_ref[...], wb_ref[...], dims, preferred_element_type=F32)
    if act == "relu2":
        acc = jnp.square(jnp.maximum(acc, 0.0))
    elif act == "sigmoid":
        acc = jax.nn.sigmoid(acc)
    o_ref[...] = acc.astype(o_ref.dtype)


def _stream_matmul(x, w, w_spec, w_block, n_tiles, *, tm, tn, out_dtype, w_rows_are_outputs, act=None, casts=()):
    m, d = x.shape
    m_tiles = m // tm
    in_specs = [pl.BlockSpec((tm, d), lambda j, i: (i, 0)), w_spec]
    out_specs = [pl.BlockSpec((tm, tn), lambda j, i: (i, j))]
    out_shape = [jax.ShapeDtypeStruct((m, n_tiles * tn), out_dtype)]
    cast_blocks = []
    for src, layer, rps in casts:
        _, rows, cols = src.shape
        nblk = rows // rps
        assert rows % rps == 0 and nblk <= n_tiles * m_tiles
        cast_blocks.append(nblk)
        in_specs.append(pl.BlockSpec((None, rps, cols),
                                     lambda j, i, layer=layer, nblk=nblk: (layer, jnp.minimum(j * m_tiles + i, nblk - 1), 0)))
        out_specs.append(pl.BlockSpec((None, rps, cols),
                                      lambda j, i, nblk=nblk: (0, jnp.minimum(j * m_tiles + i, nblk - 1), 0)))
        out_shape.append(jax.ShapeDtypeStruct((1, rows, cols), BF16))
    res = pl.pallas_call(
        functools.partial(_stream_matmul_body, w_rows_are_outputs=w_rows_are_outputs, act=act,
                          cast_blocks=tuple(cast_blocks)),
        grid=(n_tiles, m_tiles),
        in_specs=in_specs, out_specs=out_specs, out_shape=out_shape,
        scratch_shapes=[pltpu.VMEM(w_block, BF16)],
        compiler_params=_params("arbitrary", "arbitrary"),
        name="stream_matmul_t" if w_rows_are_outputs else "stream_matmul",
    )(x, w, *[c[0] for c in casts])
    return res if casts else res[0]


def _matmul_res_body(*refs, n_k, emit_norm, emit_small, n_first):
    a_ref, w_ref, r_ref = refs[:3]
    refs = refs[3:]
    if emit_norm:
        g_ref = refs[0]
        refs = refs[1:]
    if emit_small:
        ws_ref = refs[0]
        refs = refs[1:]
    o_ref = refs[0]
    k = pl.program_id(1)

    def emit(acc):
        if n_first is not None:
            y = _rms(acc, g_ref[...])
            i = pl.program_id(0)

            @pl.when(i < n_first)
            def _():
                refs[1][...] = y

            @pl.when(i >= n_first)
            def _():
                refs[2][...] = y
            return
        xn = _rms(acc, g_ref[...]).astype(BF16)
        refs[1][...] = xn
        if emit_small:
            refs[2][...] = lax.dot_general(xn, ws_ref[...].astype(BF16), _NT, preferred_element_type=F32)

    if n_k == 1:
        acc = r_ref[...] + jnp.dot(a_ref[...], w_ref[...], preferred_element_type=F32)
        o_ref[...] = acc
        if emit_norm:
            emit(acc)
        return

    @pl.when(k == 0)
    def _():
        o_ref[...] = r_ref[...]

    o_ref[...] += jnp.dot(a_ref[...], w_ref[...], preferred_element_type=F32)

    if emit_norm:
        pl.when(k == n_k - 1)(lambda: emit(o_ref[...]))


def _matmul_res(a, w, l, res, *, tm, tk, gain=None, gain_layer=0, w_small=None, split_rows=None):
    m, k = a.shape
    d = w.shape[-1]
    n_k = k // tk
    emit_norm = gain is not None
    emit_small = w_small is not None
    n_first = None if split_rows is None else split_rows // tm
    in_specs = [pl.BlockSpec((tm, tk), lambda i, kk: (i, kk)),
                pl.BlockSpec((None, tk, d), lambda i, kk: (l, kk, 0)),
                pl.BlockSpec((tm, d), lambda i, kk: (i, 0))]
    out_specs = [pl.BlockSpec((tm, d), lambda i, kk: (i, 0))]
    out_shape = [jax.ShapeDtypeStruct((m, d), F32)]
    args = [a, w, res]
    if emit_norm:
        in_specs.append(_layer(gain_layer, (1, d)))
        args.append(gain)
        if n_first is None:
            out_specs.append(pl.BlockSpec((tm, d), lambda i, kk: (i, 0)))
            out_shape.append(jax.ShapeDtypeStruct((m, d), BF16))
        else:
            assert split_rows % tm == 0 and not emit_small
            out_specs.extend(_two_source_specs(tm, d, n_first))
            out_shape.extend([jax.ShapeDtypeStruct((split_rows, d), F32),
                              jax.ShapeDtypeStruct((m - split_rows, d), F32)])
    if emit_small:
        ns = w_small.shape[-2]
        in_specs.append(_layer(gain_layer, (ns, d)))
        args.append(w_small)
        out_specs.append(pl.BlockSpec((tm, ns), lambda i, kk: (i, 0)))
        out_shape.append(jax.ShapeDtypeStruct((m, ns), F32))
    return pl.pallas_call(
        functools.partial(_matmul_res_body, n_k=n_k, emit_norm=emit_norm, emit_small=emit_small, n_first=n_first),
        grid=(m // tm, n_k),
        in_specs=in_specs, out_specs=out_specs, out_shape=out_shape,
        compiler_params=_params("arbitrary", "arbitrary"),
        name="matmul_res",
    )(*args)


def _merge_body(ol_a, ol_b, od_a, od_b, op_a, op_b, gl_ref, gd_ref, gp_ref, wl_ref, wd_ref, wp_ref, o_ref, *, n_first):
    def branch(o_a, o_b, g_ref_, w_ref_):
        o = _pick_rows(o_a, o_b, n_first)
        return g_ref_[...].astype(F32) * jnp.dot(o, w_ref_[...], preferred_element_type=F32)

    m = (branch(ol_a, ol_b, gl_ref, wl_ref) + branch(od_a, od_b, gd_ref, wd_ref)
         + branch(op_a, op_b, gp_ref, wp_ref))
    o_ref[...] = m.astype(o_ref.dtype)


def _merge(o_lru, o_dn, o_pool, gates, w_lru, w_dn, w_pool, l, *, tm, tn):
    half = o_lru[0].shape[-1]
    m = o_lru[0].shape[0] + o_lru[1].shape[0]
    n_first = o_lru[0].shape[0] // tm
    d = w_lru.shape[-1]
    nj = d // tn

    def g_spec(b):
        return pl.BlockSpec((tm, tn), lambda i, j: (i, b * nj + j))

    def w_spec():
        return pl.BlockSpec((None, half, tn), lambda i, j: (l, 0, j))

    o_specs = _two_source_specs(tm, half, n_first)
    return pl.pallas_call(
        functools.partial(_merge_body, n_first=n_first),
        grid=(m // tm, nj),
        in_specs=[*o_specs, *o_specs, *o_specs, g_spec(0), g_spec(1), g_spec(2), w_spec(), w_spec(), w_spec()],
        out_specs=pl.BlockSpec((tm, tn), lambda i, j: (i, j)),
        out_shape=jax.ShapeDtypeStruct((m, d), BF16),
        compiler_params=_params("arbitrary", "arbitrary"),
        name="merge",
    )(*o_lru, *o_dn, *o_pool, gates, gates, gates, w_lru, w_dn, w_pool)


def _softplus(x):
    return jnp.maximum(x, 0.0) + jnp.log1p(jnp.exp(-jnp.abs(x)))


def _shift_rows(cur, prev, k, chained=False):
    ax = cur.ndim - 2
    row = lax.broadcasted_iota(jnp.int32, cur.shape, ax)
    if chained:
        rolled = pltpu.roll(jnp.concatenate([prev[None], cur], axis=0), k, ax)
        return jnp.where(row >= k, rolled[1:], rolled[:-1])
    return jnp.where(row >= k, pltpu.roll(cur, k, ax), pltpu.roll(prev, k, ax))


def _causal_conv(x, prev, cw, chained):
    y = cw[CONV_WIDTH - 1:CONV_WIDTH] * x
    for k in range(1, CONV_WIDTH):
        y = y + cw[CONV_WIDTH - 1 - k:CONV_WIDTH - k] * _shift_rows(x, prev, k, chained)
    return y


def _lru_body(*refs, long_seq):
    (x_ref, gate_ref, prev_ref, h0_ref, cw_ref, cb_ref, wa_ref, ba_ref, wx_ref, bx_ref, lam_ref) = refs[:11]
    o_ref, hl_ref, nb_ref, a_scr, b_scr, h_scr, cx_scr, ch_scr = refs[11:]
    g_n, _, c = x_ref.shape
    rows = g_n * SUBLANES
    x = x_ref[...]
    if long_seq:
        @pl.when(pl.program_id(1) == 0)
        def _():
            cx_scr[...] = prev_ref[0]
            ch_scr[...] = h0_ref[0]

        prev = cx_scr[...]
    else:
        prev = prev_ref[...]
    xc = _causal_conv(x, prev, cw_ref[...], long_seq) + cb_ref[...]
    if long_seq:
        cx_scr[...] = x[g_n - 1]
        nb_ref[0] = x[g_n - 1]
    else:
        nb_ref[...] = x

    xc2 = xc.reshape(rows, c)
    xb = xc2.astype(BF16)
    nblk = c // GATE_TILE

    def gate(w_ref, b_ref):
        parts = [jnp.dot(xb[:, i * GATE_TILE:(i + 1) * GATE_TILE], w_ref[i], preferred_element_type=F32)
                 for i in range(nblk)]
        return jax.nn.sigmoid(jnp.concatenate(parts, axis=1) + b_ref[...])

    r = gate(wa_ref, ba_ref)
    i_g = gate(wx_ref, bx_ref)
    log_a = (-LRU_C) * r * _softplus(-lam_ref[...])
    a = jnp.exp(log_a)
    one_minus_a2 = -jnp.tanh(log_a) * (a * a + 1.0)
    b = jnp.sqrt(one_minus_a2) * (i_g * xc2)

    a3 = a.reshape(g_n, SUBLANES, c)
    b3 = b.reshape(g_n, SUBLANES, c)
    row = lax.broadcasted_iota(jnp.int32, a3.shape, 1)
    s = 1
    while s < SUBLANES:
        m = row >= s
        a_sh = pltpu.roll(a3, s, 1)
        b_sh = pltpu.roll(b3, s, 1)
        b3 = jnp.where(m, a3 * b_sh + b3, b3)
        a3 = jnp.where(m, a3 * a_sh, a3)
        s *= 2
    a_scr[...] = a3
    b_scr[...] = b3

    def chain(g, h_in):
        if not long_seq:
            h_in = h0_ref[g]
        hg = a_scr[g] * h_in + b_scr[g]
        h_scr[g] = hg
        h_out = hg[SUBLANES - 1:SUBLANES]
        if not long_seq:
            hl_ref[g] = h_out
        return h_out

    h_init = ch_scr[...] if long_seq else jnp.zeros((1, c), F32)
    h_fin = lax.fori_loop(0, g_n, chain, h_init)
    if long_seq:
        ch_scr[...] = h_fin
        hl_ref[0] = h_fin
    h = h_scr[...].reshape(rows, c)
    gl = gate_ref[...].reshape(rows, c)
    o_ref[...] = (h * jax.nn.gelu(gl)).astype(o_ref.dtype)


def _lru_call(proj3, col0, prev, h0, weights, l, *, long_seq, n_outer, n_inner, g_n, group0, state_layer):
    cw, cb, wa, ba, wx, bx, lam = weights
    c = cw.shape[-1]
    m = proj3.shape[0] * SUBLANES
    nseq_blk = 1 if long_seq else g_n
    nseq = n_outer * nseq_blk
    gb0 = group0 // g_n

    def x_spec(col):
        return pl.BlockSpec((g_n, SUBLANES, c), lambda i, t: (gb0 + i * n_inner + t, 0, col))

    def state_spec(r):
        if state_layer is None:
            return pl.BlockSpec((nseq_blk, r, c), lambda i, t: (i, 0, 0))
        return pl.BlockSpec((None, nseq_blk, r, c), lambda i, t: (state_layer, i, 0, 0))

    ng = c // GATE_TILE
    in_specs = [x_spec(col0), x_spec(col0 + 1), state_spec(SUBLANES), state_spec(1),
                _layer(l, (CONV_WIDTH, c)), _layer(l, (1, c)),
                _layer(l, (ng, GATE_TILE, GATE_TILE)), _layer(l, (1, c)),
                _layer(l, (ng, GATE_TILE, GATE_TILE)), _layer(l, (1, c)), _layer(l, (1, c))]
    args = [proj3, proj3, prev, h0, cw, cb, wa, ba, wx, bx, lam]
    out_specs = [pl.BlockSpec((g_n * SUBLANES, c), lambda i, t: (i * n_inner + t, 0)),
                 pl.BlockSpec((nseq_blk, 1, c), lambda i, t: (i, 0, 0)),
                 pl.BlockSpec((nseq_blk, SUBLANES, c), lambda i, t: (i, 0, 0))]
    out_shape = [jax.ShapeDtypeStruct((n_outer * n_inner * g_n * SUBLANES, c), BF16),
                 jax.ShapeDtypeStruct((nseq, 1, c), F32),
                 jax.ShapeDtypeStruct((nseq, SUBLANES, c), F32)]
    grp = (g_n, SUBLANES, c)
    return pl.pallas_call(
        functools.partial(_lru_body, long_seq=long_seq),
        grid=(n_outer, n_inner),
        in_specs=in_specs, out_specs=out_specs, out_shape=out_shape,
        scratch_shapes=[pltpu.VMEM(grp, F32), pltpu.VMEM(grp, F32), pltpu.VMEM(grp, F32),
                        pltpu.VMEM((SUBLANES, c), F32), pltpu.VMEM((1, c), F32)],
        compiler_params=_params("parallel", "arbitrary"),
        name="lru_long" if long_seq else "lru_short",
    )(*args)


def _pool_body(*refs, long_seq, past_len):
    u_ref, hist_ref, pw_ref, ps_ref, o_ref, nh_ref, c_scr = refs
    g_n, _, c = u_ref.shape
    rows = g_n * SUBLANES
    x = u_ref[...]
    t = pl.program_id(1)
    if long_seq:
        @pl.when(t == 0)
        def _():
            c_scr[0] = hist_ref[0, 0:SUBLANES]
            c_scr[1] = hist_ref[0, SUBLANES:2 * SUBLANES]

        ext = jnp.concatenate([c_scr[...], x], axis=0)
        gs = 1
    else:
        hist = hist_ref[...]
        ext = jnp.concatenate([hist[:, 0:SUBLANES], hist[:, SUBLANES:2 * SUBLANES], x], axis=0)
        gs = g_n
    nh_ref[:, 0:SUBLANES] = ext[-2 * gs:-gs]
    nh_ref[:, SUBLANES:2 * SUBLANES] = ext[-gs:]
    if long_seq:
        c_scr[0] = ext[g_n]
        c_scr[1] = ext[g_n + 1]

    def prev(a):
        return jnp.concatenate([a[:gs], a[:-gs]], axis=0)

    n_grp = len(POOL_WINDOWS)
    pg = c // n_grp
    i0 = lax.broadcasted_iota(jnp.int32, (g_n, SUBLANES, pg), 0)
    i1 = lax.broadcasted_iota(jnp.int32, (g_n, SUBLANES, pg), 1)
    t_abs = (t * g_n + i0) * SUBLANES + i1 if long_seq else i1
    outs = []
    for gi, w in enumerate(POOL_WINDOWS):
        eg = ext[:, :, gi * pg:(gi + 1) * pg]
        s = eg
        k = 1
        while k < w:
            if k < SUBLANES:
                rolled = pltpu.roll(s, k, 1)
                row = lax.broadcasted_iota(jnp.int32, s.shape, 1)
                s = s + jnp.where(row >= k, rolled, prev(rolled))
            else:
                s = s + prev(s)
            k *= 2
        cnt = jnp.minimum(t_abs + (1 + past_len), w).astype(F32)
        d = s[2 * gs:] / cnt - eg[2 * gs:]
        d2 = d.reshape(rows, pg).astype(BF16)
        outs.append(jnp.dot(d2, pw_ref[gi], preferred_element_type=F32))
    y = jnp.concatenate(outs, axis=1) * ps_ref[...]
    o_ref[...] = y.astype(o_ref.dtype)


def _pool_call(proj3, col, hist, pool_w, pool_scale, l, *, long_seq, past_len, n_outer, n_inner, g_n,
               group0, state_layer):
    c = pool_scale.shape[-1]
    m = proj3.shape[0] * SUBLANES
    nseq_blk = 1 if long_seq else g_n
    nseq = n_outer * nseq_blk
    gb0 = group0 // g_n
    pg = c // len(POOL_WINDOWS)
    hrows = 2 * SUBLANES
    if state_layer is None:
        h_spec = pl.BlockSpec((nseq_blk, hrows, c), lambda i, t: (i, 0, 0))
    else:
        h_spec = pl.BlockSpec((None, nseq_blk, hrows, c), lambda i, t: (state_layer, i, 0, 0))
    in_specs = [pl.BlockSpec((g_n, SUBLANES, c), lambda i, t: (gb0 + i * n_inner + t, 0, col)), h_spec,
                _layer(l, (len(POOL_WINDOWS), pg, pg)), _layer(l, (1, c))]
    args = [proj3, hist, pool_w, pool_scale]
    return pl.pallas_call(
        functools.partial(_pool_body, long_seq=long_seq, past_len=past_len),
        grid=(n_outer, n_inner),
        in_specs=in_specs,
        out_specs=[pl.BlockSpec((g_n * SUBLANES, c), lambda i, t: (i * n_inner + t, 0)),
                   pl.BlockSpec((nseq_blk, hrows, c), lambda i, t: (i, 0, 0))],
        out_shape=[jax.ShapeDtypeStruct((n_outer * n_inner * g_n * SUBLANES, c), BF16),
                   jax.ShapeDtypeStruct((nseq, hrows, c), F32)],
        scratch_shapes=[pltpu.VMEM((2, SUBLANES, c), F32)],
        compiler_params=_params("parallel", "arbitrary"),
        name="pool_long" if long_seq else "pool_short",
    )(*args)


CHUNK_STAGE_LAG = 3
SOLVE_PASSES = 1


def _mm(a, b):
    a_hi, b_hi = a.astype(BF16), b.astype(BF16)
    out = jnp.dot(a_hi, b_hi, preferred_element_type=F32)
    if SOLVE_PASSES == 3:
        a_lo = (a - a_hi.astype(F32)).astype(BF16)
        b_lo = (b - b_hi.astype(F32)).astype(BF16)
        out = out + jnp.dot(a_hi, b_lo, preferred_element_type=F32) + jnp.dot(a_lo, b_hi, preferred_element_type=F32)
    return out


def _delta_body(*refs, long_seq, tseq, n_aliased):
    (q_ref, k_ref, v_ref, z_ref, ba_ref, prev_ref, s0_ref, cw_ref, alog_ref, dtb_ref, nw_ref) = refs[:11]
    o_ref, s_ref, nb_ref, cx_scr = refs[11 + n_aliased:]
    g_n, _, c = q_ref.shape
    rows = g_n * SUBLANES
    n_sub = rows // DN_CHUNK
    nseq_chunk = DN_CHUNK // tseq
    heads = c // HEAD_DIM

    if long_seq:
        @pl.when(pl.program_id(1) == 0)
        def _():
            cx_scr[...] = prev_ref[0]
            s_ref[...] = s0_ref[...]

    cw = cw_ref[...]
    streams = []
    for idx, x_ref in enumerate((q_ref, k_ref, v_ref)):
        x = x_ref[...]
        lo, hi = idx * c, (idx + 1) * c
        if long_seq:
            prev = cx_scr[:, lo:hi]
            cx_scr[:, lo:hi] = x[g_n - 1]
            nb_ref[0, :, lo:hi] = x[g_n - 1]
        else:
            prev = prev_ref[:, :, lo:hi]
            nb_ref[:, :, lo:hi] = x
        y = _causal_conv(x, prev, cw[:, lo:hi], long_seq)
        streams.append((y * jax.nn.sigmoid(y)).reshape(rows, c))
    q_all, k_all, v_all = streams
    z_all = z_ref[...].reshape(rows, c)

    ba = ba_ref[...]
    beta_all = jax.nn.sigmoid(ba[:, 0:HEAD_DIM])
    g_all = -jnp.exp(alog_ref[...]) * _softplus(ba[:, HEAD_DIM:2 * HEAD_DIM] + dtb_ref[...])

    ri = lax.broadcasted_iota(jnp.int32, (DN_CHUNK, DN_CHUNK), 0)
    ci = lax.broadcasted_iota(jnp.int32, (DN_CHUNK, DN_CHUNK), 1)
    same = (ri // tseq) == (ci // tseq)
    incl = same & (ci <= ri)
    strict = same & (ci < ri)
    eye = (ri == ci).astype(F32)
    incl_f, same_f = incl.astype(F32), same.astype(F32)

    state = [s_ref[0, h] for h in range(heads)] if long_seq else None
    masks = (ri, ci, incl, strict, eye, incl_f, same_f)
    waiting = [_delta_chunk([(sub, h) for h in range(heads)], state, masks,
                            (q_all, k_all, v_all, z_all, beta_all, g_all), (s0_ref, s_ref, nw_ref, o_ref),
                            long_seq, tseq) for sub in range(n_sub)]
    running, rnd = [], 0
    while waiting or running:
        if waiting and rnd % CHUNK_STAGE_LAG == 0:
            running.append(waiting.pop(0))
        for gen in list(running):
            if next(gen, "done") == "done":
                running.remove(gen)
        rnd += 1
    if long_seq:
        for h in range(heads):
            s_ref[0, h] = state[h]


def _delta_chunk(pairs, state, masks, streams, refs, long_seq, tseq):
    ri, ci, incl, strict, eye, incl_f, same_f = masks
    q_all, k_all, v_all, z_all, beta_all, g_all = streams
    s0_ref, s_ref, nw_ref, o_ref = refs
    nseq_chunk = DN_CHUNK // tseq
    gamma_s, gtot_s, gamma_ts = {}, {}, {}
    for sub in sorted({sub for sub, _ in pairs}):
        g_sub = g_all[sub * DN_CHUNK:(sub + 1) * DN_CHUNK]
        gamma = jnp.dot(incl_f, g_sub, precision=_HI, preferred_element_type=F32)
        gamma_s[sub] = gamma
        gtot_s[sub] = jnp.dot(same_f, g_sub, precision=_HI, preferred_element_type=F32)
        gamma_ts[sub] = gamma.T

    q_l, k_l, v_l, beta_l, gcol_l, gtot_l, decay_l, qb_l, kb_l, a_l = ([] for _ in range(10))
    for sub, h in pairs:
        rs = slice(sub * DN_CHUNK, (sub + 1) * DN_CHUNK)
        hs = slice(h * HEAD_DIM, (h + 1) * HEAD_DIM)
        q_h, k_h = q_all[rs, hs], k_all[rs, hs]
        q_h = q_h * lax.rsqrt(jnp.sum(q_h * q_h, axis=-1, keepdims=True) + EPS) * (HEAD_DIM ** -0.5)
        k_h = k_h * lax.rsqrt(jnp.sum(k_h * k_h, axis=-1, keepdims=True) + EPS)
        gcol = gamma_s[sub][:, h:h + 1]
        decay = jnp.where(incl, jnp.exp(gcol - gamma_ts[sub][h:h + 1, :]), 0.0)
        qb, kb = q_h.astype(BF16), k_h.astype(BF16)
        beta = beta_all[rs, h:h + 1]
        kk = lax.dot_general(kb, kb, _NT, preferred_element_type=F32)
        q_l.append(q_h)
        k_l.append(k_h)
        v_l.append(v_all[rs, hs])
        beta_l.append(beta)
        gcol_l.append(gcol)
        gtot_l.append(gtot_s[sub][:, h:h + 1])
        decay_l.append(decay)
        qb_l.append(qb)
        kb_l.append(kb)
        a_l.append(jnp.where(strict, beta * decay * kk, 0.0))
    yield

    t_l = [eye - jnp.where(ri // 2 == ci // 2, a, 0.0) for a in a_l]
    s = 2
    while s < tseq:
        off_mask = (ri // (2 * s) == ci // (2 * s)) & (ri % (2 * s) >= s) & (ci % (2 * s) < s)
        t_off = [_mm(t, jnp.where(off_mask, a, 0.0)) for t, a in zip(t_l, a_l)]
        t_l = [t - _mm(to, t) for t, to in zip(t_l, t_off)]
        s *= 2
        yield

    eg_l = [jnp.exp(g) for g in gcol_l]
    sol_l = [_mm(t, jnp.concatenate([b * v, (b * e) * k], axis=1))
             for t, b, v, e, k in zip(t_l, beta_l, v_l, eg_l, k_l)]
    qk_l = [(lax.dot_general(qb, kb, _NT, preferred_element_type=F32) * d).astype(BF16)
            for qb, kb, d in zip(qb_l, kb_l, decay_l)]
    qg_l = [(q * e).astype(BF16) for q, e in zip(q_l, eg_l)]
    kd_l = [(k * jnp.exp(gt - g)).astype(BF16) for k, gt, g in zip(k_l, gtot_l, gcol_l)]
    yield

    w_l, o_l, s_old = [], [], []
    for i, (sub, h) in enumerate(pairs):
        u_c, wkb = sol_l[i][:, :HEAD_DIM], sol_l[i][:, HEAD_DIM:].astype(BF16)
        w_parts, o_parts, s_mats = [], [], []
        for sq in range(nseq_chunk):
            rq = slice(sq * tseq, (sq + 1) * tseq)
            s_mat = state[h] if long_seq else s0_ref[sub * nseq_chunk + sq, h]
            sb = s_mat.astype(BF16)
            s_mats.append(s_mat)
            w_parts.append(u_c[rq] - jnp.dot(wkb[rq], sb, preferred_element_type=F32))
            o_parts.append(jnp.dot(qg_l[i][rq], sb, preferred_element_type=F32))
        w_l.append((jnp.concatenate(w_parts, axis=0) if nseq_chunk > 1 else w_parts[0]).astype(BF16))
        o_l.append(jnp.concatenate(o_parts, axis=0) if nseq_chunk > 1 else o_parts[0])
        s_old.append(s_mats)
    yield

    for i, (sub, h) in enumerate(pairs):
        w = w_l[i]
        for sq in range(nseq_chunk):
            rq = slice(sq * tseq, (sq + 1) * tseq)
            g_last = jnp.exp(gtot_l[i][sq * tseq:sq * tseq + 1])
            s_upd = g_last * s_old[i][sq] + lax.dot_general(kd_l[i][rq], w[rq], _TN, preferred_element_type=F32)
            if long_seq:
                state[h] = s_upd
            else:
                s_ref[sub * nseq_chunk + sq, h] = s_upd
        o = o_l[i] + jnp.dot(qk_l[i], w, preferred_element_type=F32)
        o = o * lax.rsqrt(jnp.mean(o * o, axis=-1, keepdims=True) + EPS) * nw_ref[...]
        rs = slice(sub * DN_CHUNK, (sub + 1) * DN_CHUNK)
        hs = slice(h * HEAD_DIM, (h + 1) * HEAD_DIM)
        z_h = z_all[rs, hs]
        o_ref[rs, hs] = (o * (z_h * jax.nn.sigmoid(z_h))).astype(o_ref.dtype)


def _delta_call(proj3, col0, ba, prev, s0, conv_w, alog, dtb, norm_w, l, *, long_seq, tseq, n_outer, n_inner,
                group0, state_layer, n_sub, s_stack=None):
    c = conv_w.shape[-1] // 3
    heads = c // HEAD_DIM
    blk_rows = n_sub * DN_CHUNK
    g_n = blk_rows // SUBLANES
    m = proj3.shape[0] * SUBLANES
    nseq_blk = 1 if long_seq else blk_rows // tseq
    nseq = n_outer * nseq_blk
    gb0 = group0 // g_n

    def x_spec(col):
        return pl.BlockSpec((g_n, SUBLANES, c), lambda i, t: (gb0 + i * n_inner + t, 0, col))

    if state_layer is None:
        p_spec = pl.BlockSpec((nseq_blk, SUBLANES, 3 * c), lambda i, t: (i, 0, 0))
        s_spec = pl.BlockSpec((nseq_blk, heads, HEAD_DIM, HEAD_DIM), lambda i, t: (i, 0, 0, 0))
    else:
        p_spec = pl.BlockSpec((None, nseq_blk, SUBLANES, 3 * c), lambda i, t: (state_layer, i, 0, 0))
        s_spec = pl.BlockSpec((None, nseq_blk, heads, HEAD_DIM, HEAD_DIM), lambda i, t: (state_layer, i, 0, 0, 0))
    in_specs = [x_spec(col0), x_spec(col0 + 1), x_spec(col0 + 2), x_spec(col0 + 3),
                pl.BlockSpec((blk_rows, 2 * HEAD_DIM), lambda i, t: (gb0 + i * n_inner + t, 0)),
                p_spec, s_spec,
                _layer(l, (CONV_WIDTH, 3 * c)), _layer(l, (1, HEAD_DIM)), _layer(l, (1, HEAD_DIM)),
                _layer(l, (1, HEAD_DIM))]
    args = [proj3, proj3, proj3, proj3, ba, prev, s0, conv_w, alog, dtb, norm_w]
    aliases = {}
    s_block = (nseq_blk, heads, HEAD_DIM, HEAD_DIM)
    if s_stack is None:
        s_out_spec = pl.BlockSpec(s_block, lambda i, t: (i, 0, 0, 0))
        s_out_shape = jax.ShapeDtypeStruct((nseq,) + s_block[1:], F32)
    else:
        depth, stacked = s_stack
        s_out_spec = pl.BlockSpec((None,) + s_block, lambda i, t: (l, i, 0, 0, 0))
        s_out_shape = jax.ShapeDtypeStruct((depth, nseq) + s_block[1:], F32)
        if stacked is not None:
            in_specs.append(pl.BlockSpec(memory_space=pl.ANY))
            args.append(stacked)
            aliases[len(args) - 1] = 1
    return pl.pallas_call(
        functools.partial(_delta_body, long_seq=long_seq, tseq=tseq, n_aliased=len(aliases)),
        grid=(n_outer, n_inner),
        in_specs=in_specs,
        out_specs=[pl.BlockSpec((blk_rows, c), lambda i, t: (i * n_inner + t, 0)),
                   s_out_spec,
                   pl.BlockSpec((nseq_blk, SUBLANES, 3 * c), lambda i, t: (i, 0, 0))],
        out_shape=[jax.ShapeDtypeStruct((n_outer * n_inner * blk_rows, c), BF16),
                   s_out_shape,
                   jax.ShapeDtypeStruct((nseq, SUBLANES, 3 * c), F32)],
        scratch_shapes=[pltpu.VMEM((SUBLANES, 3 * c), F32)],
        input_output_aliases=aliases,
        compiler_params=_params("parallel", "arbitrary"),
        name="delta_long" if long_seq else "delta_short",
    )(*args)


def _cast_rows(w, steps):
    rows = w.shape[1]
    nblk = 1
    while nblk * 2 <= steps and rows % (nblk * 2) == 0 and rows // (nblk * 2) >= 2 * SUBLANES:
        nblk *= 2
    return rows // nblk


def _pad_front(a, rows):
    pad = [(0, 0)] * a.ndim
    pad[-2] = (rows - a.shape[-2], 0)
    return jnp.pad(a, pad)


def _block_diag_tiles(w):
    depth, nb, b, _ = w.shape
    per = GATE_TILE // b
    w5 = w.reshape(depth, nb // per, per, b, b)
    eye = jnp.eye(per, dtype=w.dtype)
    t = jnp.einsum('lijcd,jk->lijckd', w5, eye)
    return t.reshape(depth, nb // per, GATE_TILE, GATE_TILE).astype(BF16)


def kernel(x_prompt, x_sample, state_lru_h, state_lru_conv, state_dn_s, state_dn_conv, state_pool, norm_mix, w_in, lru_conv_w, lru_conv_b, lru_gate_a_w, lru_gate_a_b, lru_gate_x_w, lru_gate_x_b, lru_lambda, dn_conv_w, dn_a_log, dn_dt_bias, dn_norm_w, pool_w, pool_scale, w_br_lru, w_br_dn, w_br_pool, w_out, norm_mlp, w_up, w_down, norm_final):
    bp, tp, d = x_prompt.shape
    bs, ts, _ = x_sample.shape
    depth = w_in.shape[0]
    half = d // 2
    heads = half // HEAD_DIM
    mp, ms = bp * tp, bs * ts
    m = mp + ms
    assert ts == SUBLANES and tp % DN_CHUNK == 0 and bs % (DN_CHUNK // ts) == 0
    assert lru_gate_a_w.shape[-1] == LRU_BLOCK and half % GATE_TILE == 0
    assert state_pool.shape[-2] == POOL_BUF

    n_pre = 6 * half
    ba0 = n_pre
    pool0 = ba0 + 2 * heads
    gates0 = pool0 + half
    w_in_t = jnp.swapaxes(w_in, 1, 2)
    n_gate_tiles, n_stream_tiles = 3 * d // half, n_pre // half + 1
    assert gates0 % SUBLANES == 0 and pool0 % SUBLANES == 0

    def gate_row0(j):
        return pl.multiple_of(gates0 + j * half, SUBLANES)

    def stream_row0(j):
        return pl.multiple_of(jnp.where(j < n_stream_tiles - 1, j * half, pool0), SUBLANES)

    row_fill = ((0, 0), (0, HEAD_DIM - heads), (0, 0))
    w_ba = jnp.concatenate([jnp.pad(w_in_t[:, ba0:ba0 + heads], row_fill),
                            jnp.pad(w_in_t[:, ba0 + heads:ba0 + 2 * heads], row_fill)], axis=1)
    col0 = 0

    def row(a):
        return a.reshape(depth, 1, a.shape[-1])

    def lane_pad(a):
        return jnp.pad(a, ((0, 0), (0, HEAD_DIM - a.shape[-1]))).reshape(depth, 1, HEAD_DIM)

    lru_w = (lru_conv_w, row(lru_conv_b), _block_diag_tiles(lru_gate_a_w), row(lru_gate_a_b),
             _block_diag_tiles(lru_gate_x_w), row(lru_gate_x_b), row(lru_lambda))
    alog, dtb, dn_nw = lane_pad(dn_a_log), lane_pad(dn_dt_bias), row(dn_norm_w)
    pool_wb, pool_sc = pool_w.astype(BF16), row(pool_scale)
    g_mix, g_mlp = row(norm_mix), row(norm_mlp)

    s_lru_h0 = state_lru_h.reshape(depth, bs, 1, half)
    s_lru_prev = _pad_front(state_lru_conv, SUBLANES)
    s_dn_prev = _pad_front(state_dn_conv, SUBLANES)
    s_pool_hist = _pad_front(state_pool, 2 * SUBLANES)
    p_lru_h0 = jnp.zeros((bp, 1, half), F32)
    p_lru_prev = jnp.zeros((bp, SUBLANES, half), F32)
    p_dn_prev = jnp.zeros((bp, SUBLANES, 3 * half), F32)
    p_dn_s0 = jnp.zeros((bp, heads, HEAD_DIM, HEAD_DIM), F32)
    p_pool_hist = jnp.zeros((bp, 2 * SUBLANES, half), F32)

    tm = 512 if (mp % 512 == 0 and ms % 512 == 0) else 256
    tm_big = 1024 if (mp % 1024 == 0 and ms % 1024 == 0) else tm
    lru_g = 32
    assert (tp // SUBLANES) % lru_g == 0 and bs % lru_g == 0 and m % tm == 0 and mp % tm == 0
    n_t = tp // SUBLANES // lru_g
    long_kw = dict(long_seq=True, n_outer=bp, n_inner=n_t, g_n=lru_g, group0=0, state_layer=None)
    short_kw = dict(long_seq=False, n_outer=bs // lru_g, n_inner=1, g_n=lru_g, group0=mp // SUBLANES)
    seq_per_blk = DN_CHUNK // ts
    dn_sub, dn_sub_s = 4, 2
    assert tp % (dn_sub * DN_CHUNK) == 0 and bs % (dn_sub_s * seq_per_blk) == 0

    p_states, s_states = [], []
    s_dn_stack = None
    norm_fin = norm_final.reshape(1, 1, d)
    x, xn, ba = _norm_cast(x_prompt.reshape(mp, d), x_sample.reshape(ms, d), g_mix, w_ba, 0, tm=tm)
    for l in range(depth):
        w_rows = (None, pl.Element(half), pl.Element(d))
        m_tiles = m // tm_big
        gates, w_out_b, w_br_lru_b = _stream_matmul(
            xn, w_in_t, pl.BlockSpec(w_rows, lambda j, i, l=l: (l, gate_row0(j), 0)), (half, d), n_gate_tiles,
            tm=tm_big, tn=half, out_dtype=BF16, w_rows_are_outputs=True, act="sigmoid",
            casts=[(w, l, _cast_rows(w, n_gate_tiles * m_tiles)) for w in (w_out, w_br_lru)])
        proj, w_br_dn_b, w_br_pool_b = _stream_matmul(
            xn, w_in_t, pl.BlockSpec(w_rows, lambda j, i, l=l: (l, stream_row0(j), 0)), (half, d), n_stream_tiles,
            tm=tm_big, tn=half, out_dtype=F32, w_rows_are_outputs=True,
            casts=[(w, l, _cast_rows(w, n_stream_tiles * m_tiles)) for w in (w_br_dn, w_br_pool)])
        proj3 = proj.reshape(m // SUBLANES, SUBLANES, proj.shape[-1])

        o_lru_p, p_h, p_cb = _lru_call(proj3, col0, p_lru_prev, p_lru_h0, lru_w, l, **long_kw)
        o_lru_s, s_h, s_cb = _lru_call(proj3, col0, s_lru_prev, s_lru_h0, lru_w, l, state_layer=l, **short_kw)

        o_dn_p, p_s, p_db = _delta_call(proj3, col0 + 2, ba, p_dn_prev, p_dn_s0, dn_conv_w, alog, dtb, dn_nw, l,
                                        long_seq=True, tseq=DN_CHUNK, n_outer=bp, n_inner=tp // (dn_sub * DN_CHUNK),
                                        group0=0, state_layer=None, n_sub=dn_sub)
        o_dn_s, s_dn_stack, s_db = _delta_call(proj3, col0 + 2, ba, s_dn_prev, state_dn_s, dn_conv_w, alog, dtb,
                                               dn_nw, l, long_seq=False, tseq=ts,
                                               n_outer=bs // (dn_sub_s * seq_per_blk), n_inner=1,
                                               group0=mp // SUBLANES, state_layer=l, n_sub=dn_sub_s,
                                               s_stack=(depth, s_dn_stack))

        o_pool_p, p_ph = _pool_call(proj3, col0 + 6, p_pool_hist, pool_wb, pool_sc, l, past_len=0, **long_kw)
        o_pool_s, s_ph = _pool_call(proj3, col0 + 6, s_pool_hist, pool_wb, pool_sc, l, past_len=PAST_LEN,
                                    state_layer=l, **short_kw)

        merged = _merge((o_lru_p, o_lru_s), (o_dn_p, o_dn_s), (o_pool_p, o_pool_s), gates,
                        w_br_lru_b, w_br_dn_b, w_br_pool_b, 0, tm=tm, tn=d)
        x, xn = _matmul_res(merged, w_out_b, 0, x, tm=tm, tk=d, gain=g_mlp, gain_layer=l)
        w_up_spec = pl.BlockSpec((None, d, half), lambda j, i, l=l: (l, 0, j))
        n_up_tiles = w_up.shape[-1] // half
        hm, w_down_b = _stream_matmul(xn, w_up, w_up_spec, (d, half), n_up_tiles, tm=tm_big, tn=half,
                                      out_dtype=BF16, w_rows_are_outputs=False, act="relu2",
                                      casts=[(w_down, l, _cast_rows(w_down, n_up_tiles * m_tiles))])
        if l + 1 < depth:
            x, xn, ba = _matmul_res(hm, w_down_b, 0, x, tm=tm, tk=d, gain=g_mix, gain_layer=l + 1, w_small=w_ba)
        else:
            _, y_prompt, y_sample = _matmul_res(hm, w_down_b, 0, x, tm=tm, tk=half, gain=norm_fin, gain_layer=0,
                                                split_rows=mp)
        p_states.append((p_h, p_cb, p_s, p_db, p_ph))
        s_states.append((s_h, s_cb, None, s_db, s_ph))

    y_prompt = y_prompt.reshape(bp, tp, d)
    y_sample = y_sample.reshape(bs, ts, d)

    def collect(states, nseq, sm=None):
        tail = CONV_WIDTH - 1
        h = jnp.stack([s[0] for s in states]).reshape(depth, nseq, half)
        cb = jnp.stack([s[1] for s in states])[:, :, SUBLANES - tail:]
        if sm is None:
            sm = jnp.stack([s[2] for s in states])
        db = jnp.stack([s[3] for s in states])[:, :, SUBLANES - tail:]
        ph = jnp.stack([s[4] for s in states])[:, :, 2 * SUBLANES - POOL_BUF:]
        return h, cb, sm, db, ph

    return (y_prompt, y_sample) + collect(p_states, bp) + collect(s_states, bs, s_dn_stack)
```

```python
import functools

import jax
import jax.numpy as jnp
from jax import lax
from jax.experimental import pallas as pl
from jax.experimental.pallas import tpu as pltpu

F32 = jnp.float32
BF16 = jnp.bfloat16

EPS = 1e-6
LRU_C = 8.0
CONV_WIDTH = 4
HEAD_DIM = 128
LRU_BLOCK = 64
POOL_WINDOWS = (2, 4, 8, 16)
POOL_BUF = 15
DN_CHUNK = 64
PAST_LEN = 16384
SUBLANES = 8
GATE_TILE = 256
VMEM_LIMIT = 56 * 1024 * 1024

_NT = (((1,), (1,)), ((), ()))
_TN = (((0,), (0,)), ((), ()))
_HI = lax.Precision.HIGHEST


def _params(*sem):
    return pltpu.CompilerParams(dimension_semantics=sem, vmem_limit_bytes=VMEM_LIMIT)


def _full(shape):
    return pl.BlockSpec(shape, lambda *_: (0,) * len(shape))


def _layer(l, shape):
    return pl.BlockSpec((None,) + shape, lambda *_: (l,) + (0,) * len(shape))


def _rms(x, gain):
    return x * lax.rsqrt(jnp.mean(x * x, axis=-1, keepdims=True) + EPS) * gain


def _two_source_specs(tm, c, n_first):
    first = pl.BlockSpec((tm, c), lambda i, *_: (jnp.minimum(i, n_first - 1), 0))
    second = pl.BlockSpec((tm, c), lambda i, *_: (jnp.maximum(i - n_first, 0), 0))
    return first, second


def _pick_rows(first_ref, second_ref, n_first):
    return jnp.where(pl.program_id(0) < n_first, first_ref[...], second_ref[...])


def _norm_cast_body(xa_ref, xb_ref, g_ref, ws_ref, x_ref, xn_ref, os_ref, *, n_first):
    x = _pick_rows(xa_ref, xb_ref, n_first)
    x_ref[...] = x
    xn = _rms(x, g_ref[...]).astype(BF16)
    xn_ref[...] = xn
    os_ref[...] = lax.dot_general(xn, ws_ref[...].astype(BF16), _NT, preferred_element_type=F32)


def _norm_cast(xa, xb, gain, w_small, l, *, tm):
    d = xa.shape[-1]
    m = xa.shape[0] + xb.shape[0]
    n_first = xa.shape[0] // tm
    ns = w_small.shape[-2]
    return pl.pallas_call(
        functools.partial(_norm_cast_body, n_first=n_first),
        grid=(m // tm,),
        in_specs=[*_two_source_specs(tm, d, n_first), _layer(l, (1, d)), _layer(l, (ns, d))],
        out_specs=[pl.BlockSpec((tm, d), lambda i: (i, 0)), pl.BlockSpec((tm, d), lambda i: (i, 0)),
                   pl.BlockSpec((tm, ns), lambda i: (i, 0))],
        out_shape=[jax.ShapeDtypeStruct((m, d), F32), jax.ShapeDtypeStruct((m, d), BF16),
                   jax.ShapeDtypeStruct((m, ns), F32)],
        compiler_params=_params("arbitrary"),
        name="norm_cast",
    )(xa, xb, gain, w_small)


def _stream_matmul_body(*refs, w_rows_are_outputs, act, cast_blocks):
    n_cast = len(cast_blocks)
    x_ref, w_ref = refs[:2]
    src_refs = refs[2:2 + n_cast]
    o_ref = refs[2 + n_cast]
    dst_refs = refs[3 + n_cast:3 + 2 * n_cast]
    wb_ref = refs[3 + 2 * n_cast]

    @pl.when(pl.program_id(1) == 0)
    def _():
        wb_ref[...] = w_ref[...].astype(BF16)

    step = pl.program_id(0) * pl.num_programs(1) + pl.program_id(1)
    for src, dst, nblk in zip(src_refs, dst_refs, cast_blocks):
        @pl.when(step < nblk)
        def _(src=src, dst=dst):
            dst[...] = src[...].astype(BF16)

    dims = _NT if w_rows_are_outputs else (((1,), (0,)), ((), ()))
    acc = lax.dot_general(x_ref[...], wb_ref[...], dims, preferred_element_type=F32)
    if act == "relu2":
        acc = jnp.square(jnp.maximum(acc, 0.0))
    elif act == "sigmoid":
        acc = jax.nn.sigmoid(acc)
    o_ref[...] = acc.astype(o_ref.dtype)


def _stream_matmul(x, w, w_spec, w_block, n_tiles, *, tm, tn, out_dtype, w_rows_are_outputs, act=None, casts=()):
    m, d = x.shape
    m_tiles = m // tm
    in_specs = [pl.BlockSpec((tm, d), lambda j, i: (i, 0)), w_spec]
    out_specs = [pl.BlockSpec((tm, tn), lambda j, i: (i, j))]
    out_shape = [jax.ShapeDtypeStruct((m, n_tiles * tn), out_dtype)]
    cast_blocks = []
    for src, layer, rps in casts:
        _, rows, cols = src.shape
        nblk = rows // rps
        assert rows % rps == 0 and nblk <= n_tiles * m_tiles
        cast_blocks.append(nblk)
        in_specs.append(pl.BlockSpec((None, rps, cols),
                                     lambda j, i, layer=layer, nblk=nblk: (layer, jnp.minimum(j * m_tiles + i, nblk - 1), 0)))
        out_specs.append(pl.BlockSpec((None, rps, cols),
                                      lambda j, i, nblk=nblk: (0, jnp.minimum(j * m_tiles + i, nblk - 1), 0)))
        out_shape.append(jax.ShapeDtypeStruct((1, rows, cols), BF16))
    res = pl.pallas_call(
        functools.partial(_stream_matmul_body, w_rows_are_outputs=w_rows_are_outputs, act=act,
                          cast_blocks=tuple(cast_blocks)),
        grid=(n_tiles, m_tiles),
        in_specs=in_specs, out_specs=out_specs, out_shape=out_shape,
        scratch_shapes=[pltpu.VMEM(w_block, BF16)],
        compiler_params=_params("arbitrary", "arbitrary"),
        name="stream_matmul_t" if w_rows_are_outputs else "stream_matmul",
    )(x, w, *[c[0] for c in casts])
    return res if casts else res[0]


def _matmul_res_body(*refs, n_k, emit_norm, emit_small, n_first):
    a_ref, w_ref, r_ref = refs[:3]
    refs = refs[3:]
    if emit_norm:
        g_ref = refs[0]
        refs = refs[1:]
    if emit_small:
        ws_ref = refs[0]
        refs = refs[1:]
    o_ref = refs[0]
    k = pl.program_id(1)

    def emit(acc):
        if n_first is not None:
            y = _rms(acc, g_ref[...])
            i = pl.program_id(0)

            @pl.when(i < n_first)
            def _():
                refs[1][...] = y

            @pl.when(i >= n_first)
            def _():
                refs[2][...] = y
            return
        xn = _rms(acc, g_ref[...]).astype(BF16)
        refs[1][...] = xn
        if emit_small:
            refs[2][...] = lax.dot_general(xn, ws_ref[...].astype(BF16), _NT, preferred_element_type=F32)

    if n_k == 1:
        acc = r_ref[...] + jnp.dot(a_ref[...], w_ref[...], preferred_element_type=F32)
        o_ref[...] = acc
        if emit_norm:
            emit(acc)
        return

    @pl.when(k == 0)
    def _():
        o_ref[...] = r_ref[...]

    o_ref[...] += jnp.dot(a_ref[...], w_ref[...], preferred_element_type=F32)

    if emit_norm:
        pl.when(k == n_k - 1)(lambda: emit(o_ref[...]))


def _matmul_res(a, w, l, res, *, tm, tk, gain=None, gain_layer=0, w_small=None, split_rows=None):
    m, k = a.shape
    d = w.shape[-1]
    n_k = k // tk
    emit_norm = gain is not None
    emit_small = w_small is not None
    n_first = None if split_rows is None else split_rows // tm
    in_specs = [pl.BlockSpec((tm, tk), lambda i, kk: (i, kk)),
                pl.BlockSpec((None, tk, d), lambda i, kk: (l, kk, 0)),
                pl.BlockSpec((tm, d), lambda i, kk: (i, 0))]
    out_specs = [pl.BlockSpec((tm, d), lambda i, kk: (i, 0))]
    out_shape = [jax.ShapeDtypeStruct((m, d), F32)]
    args = [a, w, res]
    if emit_norm:
        in_specs.append(_layer(gain_layer, (1, d)))
        args.append(gain)
        if n_first is None:
            out_specs.append(pl.BlockSpec((tm, d), lambda i, kk: (i, 0)))
            out_shape.append(jax.ShapeDtypeStruct((m, d), BF16))
        else:
            assert split_rows % tm == 0 and not emit_small
            out_specs.extend(_two_source_specs(tm, d, n_first))
            out_shape.extend([jax.ShapeDtypeStruct((split_rows, d), F32),
                              jax.ShapeDtypeStruct((m - split_rows, d), F32)])
    if emit_small:
        ns = w_small.shape[-2]
        in_specs.append(_layer(gain_layer, (ns, d)))
        args.append(w_small)
        out_specs.append(pl.BlockSpec((tm, ns), lambda i, kk: (i, 0)))
        out_shape.append(jax.ShapeDtypeStruct((m, ns), F32))
    return pl.pallas_call(
        functools.partial(_matmul_res_body, n_k=n_k, emit_norm=emit_norm, emit_small=emit_small, n_first=n_first),
        grid=(m // tm, n_k),
        in_specs=in_specs, out_specs=out_specs, out_shape=out_shape,
        compiler_params=_params("arbitrary", "arbitrary"),
        name="matmul_res",
    )(*args)


def _merge_body(ol_a, ol_b, od_a, od_b, op_a, op_b, gl_ref, gd_ref, gp_ref, wl_ref, wd_ref, wp_ref, o_ref, *, n_first):
    def branch(o_a, o_b, g_ref_, w_ref_):
        o = _pick_rows(o_a, o_b, n_first)
        return g_ref_[...].astype(F32) * jnp.dot(o, w_ref_[...], preferred_element_type=F32)

    m = (branch(ol_a, ol_b, gl_ref, wl_ref) + branch(od_a, od_b, gd_ref, wd_ref)
         + branch(op_a, op_b, gp_ref, wp_ref))
    o_ref[...] = m.astype(o_ref.dtype)


def _merge(o_lru, o_dn, o_pool, gates, w_lru, w_dn, w_pool, l, *, tm, tn):
    half = o_lru[0].shape[-1]
    m = o_lru[0].shape[0] + o_lru[1].shape[0]
    n_first = o_lru[0].shape[0] // tm
    d = w_lru.shape[-1]
    nj = d // tn

    def g_spec(b):
        return pl.BlockSpec((tm, tn), lambda i, j: (i, b * nj + j))

    def w_spec():
        return pl.BlockSpec((None, half, tn), lambda i, j: (l, 0, j))

    o_specs = _two_source_specs(tm, half, n_first)
    return pl.pallas_call(
        functools.partial(_merge_body, n_first=n_first),
        grid=(m // tm, nj),
        in_specs=[*o_specs, *o_specs, *o_specs, g_spec(0), g_spec(1), g_spec(2), w_spec(), w_spec(), w_spec()],
        out_specs=pl.BlockSpec((tm, tn), lambda i, j: (i, j)),
        out_shape=jax.ShapeDtypeStruct((m, d), BF16),
        compiler_params=_params("arbitrary", "arbitrary"),
        name="merge",
    )(*o_lru, *o_dn, *o_pool, gates, gates, gates, w_lru, w_dn, w_pool)


def _softplus(x):
    return jnp.maximum(x, 0.0) + jnp.log1p(jnp.exp(-jnp.abs(x)))


def _shift_rows(cur, prev, k, chained=False):
    ax = cur.ndim - 2
    row = lax.broadcasted_iota(jnp.int32, cur.shape, ax)
    if chained:
        rolled = pltpu.roll(jnp.concatenate([prev[None], cur], axis=0), k, ax)
        return jnp.where(row >= k, rolled[1:], rolled[:-1])
    return jnp.where(row >= k, pltpu.roll(cur, k, ax), pltpu.roll(prev, k, ax))


def _causal_conv(x, prev, cw, chained):
    y = cw[CONV_WIDTH - 1:CONV_WIDTH] * x
    for k in range(1, CONV_WIDTH):
        y = y + cw[CONV_WIDTH - 1 - k:CONV_WIDTH - k] * _shift_rows(x, prev, k, chained)
    return y


def _lru_body(*refs, long_seq):
    (x_ref, gate_ref, prev_ref, h0_ref, cw_ref, cb_ref, wa_ref, ba_ref, wx_ref, bx_ref, lam_ref) = refs[:11]
    o_ref, hl_ref, nb_ref, a_scr, b_scr, h_scr, cx_scr, ch_scr = refs[11:]
    g_n, _, c = x_ref.shape
    rows = g_n * SUBLANES
    x = x_ref[...]
    if long_seq:
        @pl.when(pl.program_id(1) == 0)
        def _():
            cx_scr[...] = prev_ref[0]
            ch_scr[...] = h0_ref[0]

        prev = cx_scr[...]
    else:
        prev = prev_ref[...]
    xc = _causal_conv(x, prev, cw_ref[...], long_seq) + cb_ref[...]
    if long_seq:
        cx_scr[...] = x[g_n - 1]
        nb_ref[0] = x[g_n - 1]
    else:
        nb_ref[...] = x

    xc2 = xc.reshape(rows, c)
    xb = xc2.astype(BF16)
    nblk = c // GATE_TILE

    def gate(w_ref, b_ref):
        parts = [jnp.dot(xb[:, i * GATE_TILE:(i + 1) * GATE_TILE], w_ref[i], preferred_element_type=F32)
                 for i in range(nblk)]
        return jax.nn.sigmoid(jnp.concatenate(parts, axis=1) + b_ref[...])

    r = gate(wa_ref, ba_ref)
    i_g = gate(wx_ref, bx_ref)
    log_a = (-LRU_C) * r * _softplus(-lam_ref[...])
    a = jnp.exp(log_a)
    one_minus_a2 = -jnp.tanh(log_a) * (a * a + 1.0)
    b = jnp.sqrt(one_minus_a2) * (i_g * xc2)

    a3 = a.reshape(g_n, SUBLANES, c)
    b3 = b.reshape(g_n, SUBLANES, c)
    row = lax.broadcasted_iota(jnp.int32, a3.shape, 1)
    s = 1
    while s < SUBLANES:
        m = row >= s
        a_sh = pltpu.roll(a3, s, 1)
        b_sh = pltpu.roll(b3, s, 1)
        b3 = jnp.where(m, a3 * b_sh + b3, b3)
        a3 = jnp.where(m, a3 * a_sh, a3)
        s *= 2
    a_scr[...] = a3
    b_scr[...] = b3

    def chain(g, h_in):
        if not long_seq:
            h_in = h0_ref[g]
        hg = a_scr[g] * h_in + b_scr[g]
        h_scr[g] = hg
        h_out = hg[SUBLANES - 1:SUBLANES]
        if not long_seq:
            hl_ref[g] = h_out
        return h_out

    h_init = ch_scr[...] if long_seq else jnp.zeros((1, c), F32)
    h_fin = lax.fori_loop(0, g_n, chain, h_init)
    if long_seq:
        ch_scr[...] = h_fin
        hl_ref[0] = h_fin
    h = h_scr[...].reshape(rows, c)
    gl = gate_ref[...].reshape(rows, c)
    o_ref[...] = (h * jax.nn.gelu(gl)).astype(o_ref.dtype)


def _lru_call(proj3, col0, prev, h0, weights, l, *, long_seq, n_outer, n_inner, g_n, group0, state_layer):
    cw, cb, wa, ba, wx, bx, lam = weights
    c = cw.shape[-1]
    m = proj3.shape[0] * SUBLANES
    nseq_blk = 1 if long_seq else g_n
    nseq = n_outer * nseq_blk
    gb0 = group0 // g_n

    def x_spec(col):
        return pl.BlockSpec((g_n, SUBLANES, c), lambda i, t: (gb0 + i * n_inner + t, 0, col))

    def state_spec(r):
        if state_layer is None:
            return pl.BlockSpec((nseq_blk, r, c), lambda i, t: (i, 0, 0))
        return pl.BlockSpec((None, nseq_blk, r, c), lambda i, t: (state_layer, i, 0, 0))

    ng = c // GATE_TILE
    in_specs = [x_spec(col0), x_spec(col0 + 1), state_spec(SUBLANES), state_spec(1),
                _layer(l, (CONV_WIDTH, c)), _layer(l, (1, c)),
                _layer(l, (ng, GATE_TILE, GATE_TILE)), _layer(l, (1, c)),
                _layer(l, (ng, GATE_TILE, GATE_TILE)), _layer(l, (1, c)), _layer(l, (1, c))]
    args = [proj3, proj3, prev, h0, cw, cb, wa, ba, wx, bx, lam]
    out_specs = [pl.BlockSpec((g_n * SUBLANES, c), lambda i, t: (i * n_inner + t, 0)),
                 pl.BlockSpec((nseq_blk, 1, c), lambda i, t: (i, 0, 0)),
                 pl.BlockSpec((nseq_blk, SUBLANES, c), lambda i, t: (i, 0, 0))]
    out_shape = [jax.ShapeDtypeStruct((n_outer * n_inner * g_n * SUBLANES, c), BF16),
                 jax.ShapeDtypeStruct((nseq, 1, c), F32),
                 jax.ShapeDtypeStruct((nseq, SUBLANES, c), F32)]
    grp = (g_n, SUBLANES, c)
    return pl.pallas_call(
        functools.partial(_lru_body, long_seq=long_seq),
        grid=(n_outer, n_inner),
        in_specs=in_specs, out_specs=out_specs, out_shape=out_shape,
        scratch_shapes=[pltpu.VMEM(grp, F32), pltpu.VMEM(grp, F32), pltpu.VMEM(grp, F32),
                        pltpu.VMEM((SUBLANES, c), F32), pltpu.VMEM((1, c), F32)],
        compiler_params=_params("parallel", "arbitrary"),
        name="lru_long" if long_seq else "lru_short",
    )(*args)


def _pool_body(*refs, long_seq, past_len):
    u_ref, hist_ref, pw_ref, ps_ref, o_ref, nh_ref, c_scr = refs
    g_n, _, c = u_ref.shape
    rows = g_n * SUBLANES
    x = u_ref[...]
    t = pl.program_id(1)
    if long_seq:
        @pl.when(t == 0)
        def _():
            c_scr[0] = hist_ref[0, 0:SUBLANES]
            c_scr[1] = hist_ref[0, SUBLANES:2 * SUBLANES]

        ext = jnp.concatenate([c_scr[...], x], axis=0)
        gs = 1
    else:
        hist = hist_ref[...]
        ext = jnp.concatenate([hist[:, 0:SUBLANES], hist[:, SUBLANES:2 * SUBLANES], x], axis=0)
        gs = g_n
    nh_ref[:, 0:SUBLANES] = ext[-2 * gs:-gs]
    nh_ref[:, SUBLANES:2 * SUBLANES] = ext[-gs:]
    if long_seq:
        c_scr[0] = ext[g_n]
        c_scr[1] = ext[g_n + 1]

    def prev(a):
        return jnp.concatenate([a[:gs], a[:-gs]], axis=0)

    n_grp = len(POOL_WINDOWS)
    pg = c // n_grp
    i0 = lax.broadcasted_iota(jnp.int32, (g_n, SUBLANES, pg), 0)
    i1 = lax.broadcasted_iota(jnp.int32, (g_n, SUBLANES, pg), 1)
    t_abs = (t * g_n + i0) * SUBLANES + i1 if long_seq else i1
    outs = []
    for gi, w in enumerate(POOL_WINDOWS):
        eg = ext[:, :, gi * pg:(gi + 1) * pg]
        s = eg
        k = 1
        while k < w:
            if k < SUBLANES:
                rolled = pltpu.roll(s, k, 1)
                row = lax.broadcasted_iota(jnp.int32, s.shape, 1)
                s = s + jnp.where(row >= k, rolled, prev(rolled))
            else:
                s = s + prev(s)
            k *= 2
        cnt = jnp.minimum(t_abs + (1 + past_len), w).astype(F32)
        d = s[2 * gs:] / cnt - eg[2 * gs:]
        d2 = d.reshape(rows, pg).astype(BF16)
        outs.append(jnp.dot(d2, pw_ref[gi], preferred_element_type=F32))
    y = jnp.concatenate(outs, axis=1) * ps_ref[...]
    o_ref[...] = y.astype(o_ref.dtype)


def _pool_call(proj3, col, hist, pool_w, pool_scale, l, *, long_seq, past_len, n_outer, n_inner, g_n,
               group0, state_layer):
    c = pool_scale.shape[-1]
    m = proj3.shape[0] * SUBLANES
    nseq_blk = 1 if long_seq else g_n
    nseq = n_outer * nseq_blk
    gb0 = group0 // g_n
    pg = c // len(POOL_WINDOWS)
    hrows = 2 * SUBLANES
    if state_layer is None:
        h_spec = pl.BlockSpec((nseq_blk, hrows, c), lambda i, t: (i, 0, 0))
    else:
        h_spec = pl.BlockSpec((None, nseq_blk, hrows, c), lambda i, t: (state_layer, i, 0, 0))
    in_specs = [pl.BlockSpec((g_n, SUBLANES, c), lambda i, t: (gb0 + i * n_inner + t, 0, col)), h_spec,
                _layer(l, (len(POOL_WINDOWS), pg, pg)), _layer(l, (1, c))]
    args = [proj3, hist, pool_w, pool_scale]
    return pl.pallas_call(
        functools.partial(_pool_body, long_seq=long_seq, past_len=past_len),
        grid=(n_outer, n_inner),
        in_specs=in_specs,
        out_specs=[pl.BlockSpec((g_n * SUBLANES, c), lambda i, t: (i * n_inner + t, 0)),
                   pl.BlockSpec((nseq_blk, hrows, c), lambda i, t: (i, 0, 0))],
        out_shape=[jax.ShapeDtypeStruct((n_outer * n_inner * g_n * SUBLANES, c), BF16),
                   jax.ShapeDtypeStruct((nseq, hrows, c), F32)],
        scratch_shapes=[pltpu.VMEM((2, SUBLANES, c), F32)],
        compiler_params=_params("parallel", "arbitrary"),
        name="pool_long" if long_seq else "pool_short",
    )(*args)


CHUNK_STAGE_LAG = 3
SOLVE_PASSES = 1


def _mm(a, b):
    a_hi, b_hi = a.astype(BF16), b.astype(BF16)
    out = jnp.dot(a_hi, b_hi, preferred_element_type=F32)
    if SOLVE_PASSES == 3:
        a_lo = (a - a_hi.astype(F32)).astype(BF16)
        b_lo = (b - b_hi.astype(F32)).astype(BF16)
        out = out + jnp.dot(a_hi, b_lo, preferred_element_type=F32) + jnp.dot(a_lo, b_hi, preferred_element_type=F32)
    return out


def _delta_body(*refs, long_seq, tseq, n_aliased):
    (q_ref, k_ref, v_ref, z_ref, ba_ref, prev_ref, s0_ref, cw_ref, alog_ref, dtb_ref, nw_ref) = refs[:11]
    o_ref, s_ref, nb_ref, cx_scr = refs[11 + n_aliased:]
    g_n, _, c = q_ref.shape
    rows = g_n * SUBLANES
    n_sub = rows // DN_CHUNK
    nseq_chunk = DN_CHUNK // tseq
    heads = c // HEAD_DIM

    if long_seq:
        @pl.when(pl.program_id(1) == 0)
        def _():
            cx_scr[...] = prev_ref[0]
            s_ref[...] = s0_ref[...]

    cw = cw_ref[...]
    streams = []
    for idx, x_ref in enumerate((q_ref, k_ref, v_ref)):
        x = x_ref[...]
        lo, hi = idx * c, (idx + 1) * c
        if long_seq:
            prev = cx_scr[:, lo:hi]
            cx_scr[:, lo:hi] = x[g_n - 1]
            nb_ref[0, :, lo:hi] = x[g_n - 1]
        else:
            prev = prev_ref[:, :, lo:hi]
            nb_ref[:, :, lo:hi] = x
        y = _causal_conv(x, prev, cw[:, lo:hi], long_seq)
        streams.append((y * jax.nn.sigmoid(y)).reshape(rows, c))
    q_all, k_all, v_all = streams
    z_all = z_ref[...].reshape(rows, c)

    ba = ba_ref[...]
    beta_all = jax.nn.sigmoid(ba[:, 0:HEAD_DIM])
    g_all = -jnp.exp(alog_ref[...]) * _softplus(ba[:, HEAD_DIM:2 * HEAD_DIM] + dtb_ref[...])

    ri = lax.broadcasted_iota(jnp.int32, (DN_CHUNK, DN_CHUNK), 0)
    ci = lax.broadcasted_iota(jnp.int32, (DN_CHUNK, DN_CHUNK), 1)
    same = (ri // tseq) == (ci // tseq)
    incl = same & (ci <= ri)
    strict = same & (ci < ri)
    eye = (ri == ci).astype(F32)
    incl_f, same_f = incl.astype(F32), same.astype(F32)

    state = [s_ref[0, h] for h in range(heads)] if long_seq else None
    masks = (ri, ci, incl, strict, eye, incl_f, same_f)
    waiting = [_delta_chunk([(sub, h) for h in range(heads)], state, masks,
                            (q_all, k_all, v_all, z_all, beta_all, g_all), (s0_ref, s_ref, nw_ref, o_ref),
                            long_seq, tseq) for sub in range(n_sub)]
    running, rnd = [], 0
    while waiting or running:
        if waiting and rnd % CHUNK_STAGE_LAG == 0:
            running.append(waiting.pop(0))
        for gen in list(running):
            if next(gen, "done") == "done":
                running.remove(gen)
        rnd += 1
    if long_seq:
        for h in range(heads):
            s_ref[0, h] = state[h]


def _delta_chunk(pairs, state, masks, streams, refs, long_seq, tseq):
    ri, ci, incl, strict, eye, incl_f, same_f = masks
    q_all, k_all, v_all, z_all, beta_all, g_all = streams
    s0_ref, s_ref, nw_ref, o_ref = refs
    nseq_chunk = DN_CHUNK // tseq
    gamma_s, gtot_s, gamma_ts = {}, {}, {}
    for sub in sorted({sub for sub, _ in pairs}):
        g_sub = g_all[sub * DN_CHUNK:(sub + 1) * DN_CHUNK]
        gamma = jnp.dot(incl_f, g_sub, precision=_HI, preferred_element_type=F32)
        gamma_s[sub] = gamma
        gtot_s[sub] = jnp.dot(same_f, g_sub, precision=_HI, preferred_element_type=F32)
        gamma_ts[sub] = gamma.T

    q_l, k_l, v_l, beta_l, gcol_l, gtot_l, decay_l, qb_l, kb_l, a_l = ([] for _ in range(10))
    for sub, h in pairs:
        rs = slice(sub * DN_CHUNK, (sub + 1) * DN_CHUNK)
        hs = slice(h * HEAD_DIM, (h + 1) * HEAD_DIM)
        q_h, k_h = q_all[rs, hs], k_all[rs, hs]
        q_h = q_h * lax.rsqrt(jnp.sum(q_h * q_h, axis=-1, keepdims=True) + EPS) * (HEAD_DIM ** -0.5)
        k_h = k_h * lax.rsqrt(jnp.sum(k_h * k_h, axis=-1, keepdims=True) + EPS)
        gcol = gamma_s[sub][:, h:h + 1]
        decay = jnp.where(incl, jnp.exp(gcol - gamma_ts[sub][h:h + 1, :]), 0.0)
        qb, kb = q_h.astype(BF16), k_h.astype(BF16)
        beta = beta_all[rs, h:h + 1]
        kk = lax.dot_general(kb, kb, _NT, preferred_element_type=F32)
        q_l.append(q_h)
        k_l.append(k_h)
        v_l.append(v_all[rs, hs])
        beta_l.append(beta)
        gcol_l.append(gcol)
        gtot_l.append(gtot_s[sub][:, h:h + 1])
        decay_l.append(decay)
        qb_l.append(qb)
        kb_l.append(kb)
        a_l.append(jnp.where(strict, beta * decay * kk, 0.0))
    yield

    t_l = [eye - jnp.where(ri // 2 == ci // 2, a, 0.0) for a in a_l]
    s = 2
    while s < tseq:
        off_mask = (ri // (2 * s) == ci // (2 * s)) & (ri % (2 * s) >= s) & (ci % (2 * s) < s)
        t_off = [_mm(t, jnp.where(off_mask, a, 0.0)) for t, a in zip(t_l, a_l)]
        t_l = [t - _mm(to, t) for t, to in zip(t_l, t_off)]
        s *= 2
        yield

    eg_l = [jnp.exp(g) for g in gcol_l]
    sol_l = [_mm(t, jnp.concatenate([b * v, (b * e) * k], axis=1))
             for t, b, v, e, k in zip(t_l, beta_l, v_l, eg_l, k_l)]
    qk_l = [(lax.dot_general(qb, kb, _NT, preferred_element_type=F32) * d).astype(BF16)
            for qb, kb, d in zip(qb_l, kb_l, decay_l)]
    qg_l = [(q * e).astype(BF16) for q, e in zip(q_l, eg_l)]
    kd_l = [(k * jnp.exp(gt - g)).astype(BF16) for k, gt, g in zip(k_l, gtot_l, gcol_l)]
    yield

    w_l, o_l, s_old = [], [], []
    for i, (sub, h) in enumerate(pairs):
        u_c, wkb = sol_l[i][:, :HEAD_DIM], sol_l[i][:, HEAD_DIM:].astype(BF16)
        w_parts, o_parts, s_mats = [], [], []
        for sq in range(nseq_chunk):
            rq = slice(sq * tseq, (sq + 1) * tseq)
            s_mat = state[h] if long_seq else s0_ref[sub * nseq_chunk + sq, h]
            sb = s_mat.astype(BF16)
            s_mats.append(s_mat)
            w_parts.append(u_c[rq] - jnp.dot(wkb[rq], sb, preferred_element_type=F32))
            o_parts.append(jnp.dot(qg_l[i][rq], sb, preferred_element_type=F32))
        w_l.append((jnp.concatenate(w_parts, axis=0) if nseq_chunk > 1 else w_parts[0]).astype(BF16))
        o_l.append(jnp.concatenate(o_parts, axis=0) if nseq_chunk > 1 else o_parts[0])
        s_old.append(s_mats)
    yield

    for i, (sub, h) in enumerate(pairs):
        w = w_l[i]
        for sq in range(nseq_chunk):
            rq = slice(sq * tseq, (sq + 1) * tseq)
            g_last = jnp.exp(gtot_l[i][sq * tseq:sq * tseq + 1])
            s_upd = g_last * s_old[i][sq] + lax.dot_general(kd_l[i][rq], w[rq], _TN, preferred_element_type=F32)
            if long_seq:
                state[h] = s_upd
            else:
                s_ref[sub * nseq_chunk + sq, h] = s_upd
        o = o_l[i] + jnp.dot(qk_l[i], w, preferred_element_type=F32)
        o = o * lax.rsqrt(jnp.mean(o * o, axis=-1, keepdims=True) + EPS) * nw_ref[...]
        rs = slice(sub * DN_CHUNK, (sub + 1) * DN_CHUNK)
        hs = slice(h * HEAD_DIM, (h + 1) * HEAD_DIM)
        z_h = z_all[rs, hs]
        o_ref[rs, hs] = (o * (z_h * jax.nn.sigmoid(z_h))).astype(o_ref.dtype)


def _delta_call(proj3, col0, ba, prev, s0, conv_w, alog, dtb, norm_w, l, *, long_seq, tseq, n_outer, n_inner,
                group0, state_layer, n_sub, s_stack=None):
    c = conv_w.shape[-1] // 3
    heads = c // HEAD_DIM
    blk_rows = n_sub * DN_CHUNK
    g_n = blk_rows // SUBLANES
    m = proj3.shape[0] * SUBLANES
    nseq_blk = 1 if long_seq else blk_rows // tseq
    nseq = n_outer * nseq_blk
    gb0 = group0 // g_n

    def x_spec(col):
        return pl.BlockSpec((g_n, SUBLANES, c), lambda i, t: (gb0 + i * n_inner + t, 0, col))

    if state_layer is None:
        p_spec = pl.BlockSpec((nseq_blk, SUBLANES, 3 * c), lambda i, t: (i, 0, 0))
        s_spec = pl.BlockSpec((nseq_blk, heads, HEAD_DIM, HEAD_DIM), lambda i, t: (i, 0, 0, 0))
    else:
        p_spec = pl.BlockSpec((None, nseq_blk, SUBLANES, 3 * c), lambda i, t: (state_layer, i, 0, 0))
        s_spec = pl.BlockSpec((None, nseq_blk, heads, HEAD_DIM, HEAD_DIM), lambda i, t: (state_layer, i, 0, 0, 0))
    in_specs = [x_spec(col0), x_spec(col0 + 1), x_spec(col0 + 2), x_spec(col0 + 3),
                pl.BlockSpec((blk_rows, 2 * HEAD_DIM), lambda i, t: (gb0 + i * n_inner + t, 0)),
                p_spec, s_spec,
                _layer(l, (CONV_WIDTH, 3 * c)), _layer(l, (1, HEAD_DIM)), _layer(l, (1, HEAD_DIM)),
                _layer(l, (1, HEAD_DIM))]
    args = [proj3, proj3, proj3, proj3, ba, prev, s0, conv_w, alog, dtb, norm_w]
    aliases = {}
    s_block = (nseq_blk, heads, HEAD_DIM, HEAD_DIM)
    if s_stack is None:
        s_out_spec = pl.BlockSpec(s_block, lambda i, t: (i, 0, 0, 0))
        s_out_shape = jax.ShapeDtypeStruct((nseq,) + s_block[1:], F32)
    else:
        depth, stacked = s_stack
        s_out_spec = pl.BlockSpec((None,) + s_block, lambda i, t: (l, i, 0, 0, 0))
        s_out_shape = jax.ShapeDtypeStruct((depth, nseq) + s_block[1:], F32)
        if stacked is not None:
            in_specs.append(pl.BlockSpec(memory_space=pl.ANY))
            args.append(stacked)
            aliases[len(args) - 1] = 1
    return pl.pallas_call(
        functools.partial(_delta_body, long_seq=long_seq, tseq=tseq, n_aliased=len(aliases)),
        grid=(n_outer, n_inner),
        in_specs=in_specs,
        out_specs=[pl.BlockSpec((blk_rows, c), lambda i, t: (i * n_inner + t, 0)),
                   s_out_spec,
                   pl.BlockSpec((nseq_blk, SUBLANES, 3 * c), lambda i, t: (i, 0, 0))],
        out_shape=[jax.ShapeDtypeStruct((n_outer * n_inner * blk_rows, c), BF16),
                   s_out_shape,
                   jax.ShapeDtypeStruct((nseq, SUBLANES, 3 * c), F32)],
        scratch_shapes=[pltpu.VMEM((SUBLANES, 3 * c), F32)],
        input_output_aliases=aliases,
        compiler_params=_params("parallel", "arbitrary"),
        name="delta_long" if long_seq else "delta_short",
    )(*args)


def _cast_rows(w, steps):
    rows = w.shape[1]
    nblk = 1
    while nblk * 2 <= steps and rows % (nblk * 2) == 0 and rows // (nblk * 2) >= 2 * SUBLANES:
        nblk *= 2
    return rows // nblk


def _pad_front(a, rows):
    pad = [(0, 0)] * a.ndim
    pad[-2] = (rows - a.shape[-2], 0)
    return jnp.pad(a, pad)


def _block_diag_tiles(w):
    depth, nb, b, _ = w.shape
    per = GATE_TILE // b
    w5 = w.reshape(depth, nb // per, per, b, b)
    eye = jnp.eye(per, dtype=w.dtype)
    t = jnp.einsum('lijcd,jk->lijckd', w5, eye)
    return t.reshape(depth, nb // per, GATE_TILE, GATE_TILE).astype(BF16)


def kernel(x_prompt, x_sample, state_lru_h, state_lru_conv, state_dn_s, state_dn_conv, state_pool, norm_mix, w_in, lru_conv_w, lru_conv_b, lru_gate_a_w, lru_gate_a_b, lru_gate_x_w, lru_gate_x_b, lru_lambda, dn_conv_w, dn_a_log, dn_dt_bias, dn_norm_w, pool_w, pool_scale, w_br_lru, w_br_dn, w_br_pool, w_out, norm_mlp, w_up, w_down, norm_final):
    bp, tp, d = x_prompt.shape
    bs, ts, _ = x_sample.shape
    depth = w_in.shape[0]
    half = d // 2
    heads = half // HEAD_DIM
    mp, ms = bp * tp, bs * ts
    m = mp + ms
    assert ts == SUBLANES and tp % DN_CHUNK == 0 and bs % (DN_CHUNK // ts) == 0
    assert lru_gate_a_w.shape[-1] == LRU_BLOCK and half % GATE_TILE == 0
    assert state_pool.shape[-2] == POOL_BUF

    n_pre = 6 * half
    ba0 = n_pre
    pool0 = ba0 + 2 * heads
    gates0 = pool0 + half
    w_in_t = jnp.swapaxes(w_in, 1, 2)
    n_gate_tiles, n_stream_tiles = 3 * d // half, n_pre // half + 1
    assert gates0 % SUBLANES == 0 and pool0 % SUBLANES == 0

    def gate_row0(j):
        return pl.multiple_of(gates0 + j * half, SUBLANES)

    def stream_row0(j):
        return pl.multiple_of(jnp.where(j < n_stream_tiles - 1, j * half, pool0), SUBLANES)

    row_fill = ((0, 0), (0, HEAD_DIM - heads), (0, 0))
    w_ba = jnp.concatenate([jnp.pad(w_in_t[:, ba0:ba0 + heads], row_fill),
                            jnp.pad(w_in_t[:, ba0 + heads:ba0 + 2 * heads], row_fill)], axis=1)
    col0 = 0

    def row(a):
        return a.reshape(depth, 1, a.shape[-1])

    def lane_pad(a):
        return jnp.pad(a, ((0, 0), (0, HEAD_DIM - a.shape[-1]))).reshape(depth, 1, HEAD_DIM)

    lru_w = (lru_conv_w, row(lru_conv_b), _block_diag_tiles(lru_gate_a_w), row(lru_gate_a_b),
             _block_diag_tiles(lru_gate_x_w), row(lru_gate_x_b), row(lru_lambda))
    alog, dtb, dn_nw = lane_pad(dn_a_log), lane_pad(dn_dt_bias), row(dn_norm_w)
    pool_wb, pool_sc = pool_w.astype(BF16), row(pool_scale)
    g_mix, g_mlp = row(norm_mix), row(norm_mlp)

    s_lru_h0 = state_lru_h.reshape(depth, bs, 1, half)
    s_lru_prev = _pad_front(state_lru_conv, SUBLANES)
    s_dn_prev = _pad_front(state_dn_conv, SUBLANES)
    s_pool_hist = _pad_front(state_pool, 2 * SUBLANES)
    p_lru_h0 = jnp.zeros((bp, 1, half), F32)
    p_lru_prev = jnp.zeros((bp, SUBLANES, half), F32)
    p_dn_prev = jnp.zeros((bp, SUBLANES, 3 * half), F32)
    p_dn_s0 = jnp.zeros((bp, heads, HEAD_DIM, HEAD_DIM), F32)
    p_pool_hist = jnp.zeros((bp, 2 * SUBLANES, half), F32)

    tm = 512 if (mp % 512 == 0 and ms % 512 == 0) else 256
    tm_big = next(t for t in (1536, 1024, tm) if m % t == 0)
    lru_g = 32
    assert (tp // SUBLANES) % lru_g == 0 and bs % lru_g == 0 and m % tm == 0 and mp % tm == 0
    n_t = tp // SUBLANES // lru_g
    long_kw = dict(long_seq=True, n_outer=bp, n_inner=n_t, g_n=lru_g, group0=0, state_layer=None)
    short_kw = dict(long_seq=False, n_outer=bs // lru_g, n_inner=1, g_n=lru_g, group0=mp // SUBLANES)
    seq_per_blk = DN_CHUNK // ts
    dn_sub, dn_sub_s = 8, 2
    assert tp % (dn_sub * DN_CHUNK) == 0 and bs % (dn_sub_s * seq_per_blk) == 0

    p_states, s_states = [], []
    s_dn_stack = None
    norm_fin = norm_final.reshape(1, 1, d)
    x, xn, ba = _norm_cast(x_prompt.reshape(mp, d), x_sample.reshape(ms, d), g_mix, w_ba, 0, tm=tm)
    for l in range(depth):
        w_rows = (None, pl.Element(half), pl.Element(d))
        m_tiles = m // tm_big
        gates, w_out_b, w_br_lru_b = _stream_matmul(
            xn, w_in_t, pl.BlockSpec(w_rows, lambda j, i, l=l: (l, gate_row0(j), 0)), (half, d), n_gate_tiles,
            tm=tm_big, tn=half, out_dtype=BF16, w_rows_are_outputs=True, act="sigmoid",
            casts=[(w, l, _cast_rows(w, n_gate_tiles * m_tiles)) for w in (w_out, w_br_lru)])
        proj, w_br_dn_b, w_br_pool_b = _stream_matmul(
            xn, w_in_t, pl.BlockSpec(w_rows, lambda j, i, l=l: (l, stream_row0(j), 0)), (half, d), n_stream_tiles,
            tm=tm_big, tn=half, out_dtype=F32, w_rows_are_outputs=True,
            casts=[(w, l, _cast_rows(w, n_stream_tiles * m_tiles)) for w in (w_br_dn, w_br_pool)])
        proj3 = proj.reshape(m // SUBLANES, SUBLANES, proj.shape[-1])

        o_lru_p, p_h, p_cb = _lru_call(proj3, col0, p_lru_prev, p_lru_h0, lru_w, l, **long_kw)
        o_lru_s, s_h, s_cb = _lru_call(proj3, col0, s_lru_prev, s_lru_h0, lru_w, l, state_layer=l, **short_kw)

        o_dn_p, p_s, p_db = _delta_call(proj3, col0 + 2, ba, p_dn_prev, p_dn_s0, dn_conv_w, alog, dtb, dn_nw, l,
                                        long_seq=True, tseq=DN_CHUNK, n_outer=bp, n_inner=tp // (dn_sub * DN_CHUNK),
                                        group0=0, state_layer=None, n_sub=dn_sub)
        o_dn_s, s_dn_stack, s_db = _delta_call(proj3, col0 + 2, ba, s_dn_prev, state_dn_s, dn_conv_w, alog, dtb,
                                               dn_nw, l, long_seq=False, tseq=ts,
                                               n_outer=bs // (dn_sub_s * seq_per_blk), n_inner=1,
                                               group0=mp // SUBLANES, state_layer=l, n_sub=dn_sub_s,
                                               s_stack=(depth, s_dn_stack))

        o_pool_p, p_ph = _pool_call(proj3, col0 + 6, p_pool_hist, pool_wb, pool_sc, l, past_len=0, **long_kw)
        o_pool_s, s_ph = _pool_call(proj3, col0 + 6, s_pool_hist, pool_wb, pool_sc, l, past_len=PAST_LEN,
                                    state_layer=l, **short_kw)

        merged = _merge((o_lru_p, o_lru_s), (o_dn_p, o_dn_s), (o_pool_p, o_pool_s), gates,
                        w_br_lru_b, w_br_dn_b, w_br_pool_b, 0, tm=tm, tn=d)
        x, xn = _matmul_res(merged, w_out_b, 0, x, tm=tm, tk=d, gain=g_mlp, gain_layer=l)
        w_up_spec = pl.BlockSpec((None, d, half), lambda j, i, l=l: (l, 0, j))
        n_up_tiles = w_up.shape[-1] // half
        hm, w_down_b = _stream_matmul(xn, w_up, w_up_spec, (d, half), n_up_tiles, tm=tm_big, tn=half,
                                      out_dtype=BF16, w_rows_are_outputs=False, act="relu2",
                                      casts=[(w_down, l, _cast_rows(w_down, n_up_tiles * m_tiles))])
        if l + 1 < depth:
            x, xn, ba = _matmul_res(hm, w_down_b, 0, x, tm=tm, tk=d, gain=g_mix, gain_layer=l + 1, w_small=w_ba)
        else:
            _, y_prompt, y_sample = _matmul_res(hm, w_down_b, 0, x, tm=tm, tk=half, gain=norm_fin, gain_layer=0,
                                                split_rows=mp)
        p_states.append((p_h, p_cb, p_s, p_db, p_ph))
        s_states.append((s_h, s_cb, None, s_db, s_ph))

    y_prompt = y_prompt.reshape(bp, tp, d)
    y_sample = y_sample.reshape(bs, ts, d)

    def collect(states, nseq, sm=None):
        tail = CONV_WIDTH - 1
        h = jnp.stack([s[0] for s in states]).reshape(depth, nseq, half)
        cb = jnp.stack([s[1] for s in states])[:, :, SUBLANES - tail:]
        if sm is None:
            sm = jnp.stack([s[2] for s in states])
        db = jnp.stack([s[3] for s in states])[:, :, SUBLANES - tail:]
        ph = jnp.stack([s[4] for s in states])[:, :, 2 * SUBLANES - POOL_BUF:]
        return h, cb, sm, db, ph

    return (y_prompt, y_sample) + collect(p_states, bp) + collect(s_states, bs, s_dn_stack)
```

```python
import functools

import jax
import jax.numpy as jnp
from jax import lax
from jax.experimental import pallas as pl
from jax.experimental.pallas import tpu as pltpu

F32 = jnp.float32
BF16 = jnp.bfloat16

EPS = 1e-6
LRU_C = 8.0
CONV_WIDTH = 4
HEAD_DIM = 128
LRU_BLOCK = 64
POOL_WINDOWS = (2, 4, 8, 16)
POOL_BUF = 15
DN_CHUNK = 64
PAST_LEN = 16384
SUBLANES = 8
GATE_TILE = 256
VMEM_LIMIT = 56 * 1024 * 1024

_NT = (((1,), (1,)), ((), ()))
_TN = (((0,), (0,)), ((), ()))
_HI = lax.Precision.HIGHEST


def _params(*sem):
    return pltpu.CompilerParams(dimension_semantics=sem, vmem_limit_bytes=VMEM_LIMIT)


def _full(shape):
    return pl.BlockSpec(shape, lambda *_: (0,) * len(shape))


def _layer(l, shape):
    return pl.BlockSpec((None,) + shape, lambda *_: (l,) + (0,) * len(shape))


def _rms(x, gain):
    return x * lax.rsqrt(jnp.mean(x * x, axis=-1, keepdims=True) + EPS) * gain


def _two_source_specs(tm, c, n_first):
    first = pl.BlockSpec((tm, c), lambda i, *_: (jnp.minimum(i, n_first - 1), 0))
    second = pl.BlockSpec((tm, c), lambda i, *_: (jnp.maximum(i - n_first, 0), 0))
    return first, second


def _pick_rows(first_ref, second_ref, n_first):
    return jnp.where(pl.program_id(0) < n_first, first_ref[...], second_ref[...])


def _norm_cast_body(xa_ref, xb_ref, g_ref, ws_ref, x_ref, xn_ref, os_ref, *, n_first):
    x = _pick_rows(xa_ref, xb_ref, n_first)
    x_ref[...] = x
    xn = _rms(x, g_ref[...]).astype(BF16)
    xn_ref[...] = xn
    os_ref[...] = lax.dot_general(xn, ws_ref[...].astype(BF16), _NT, preferred_element_type=F32)


def _norm_cast(xa, xb, gain, w_small, l, *, tm):
    d = xa.shape[-1]
    m = xa.shape[0] + xb.shape[0]
    n_first = xa.shape[0] // tm
    ns = w_small.shape[-2]
    return pl.pallas_call(
        functools.partial(_norm_cast_body, n_first=n_first),
        grid=(m // tm,),
        in_specs=[*_two_source_specs(tm, d, n_first), _layer(l, (1, d)), _layer(l, (ns, d))],
        out_specs=[pl.BlockSpec((tm, d), lambda i: (i, 0)), pl.BlockSpec((tm, d), lambda i: (i, 0)),
                   pl.BlockSpec((tm, ns), lambda i: (i, 0))],
        out_shape=[jax.ShapeDtypeStruct((m, d), F32), jax.ShapeDtypeStruct((m, d), BF16),
                   jax.ShapeDtypeStruct((m, ns), F32)],
        compiler_params=_params("arbitrary"),
        name="norm_cast",
    )(xa, xb, gain, w_small)


def _stream_matmul_body(*refs, w_rows_are_outputs, act, cast_blocks):
    n_cast = len(cast_blocks)
    x_ref, w_ref = refs[:2]
    src_refs = refs[2:2 + n_cast]
    o_ref = refs[2 + n_cast]
    dst_refs = refs[3 + n_cast:3 + 2 * n_cast]
    wb_ref = refs[3 + 2 * n_cast]

    @pl.when(pl.program_id(1) == 0)
    def _():
        wb_ref[...] = w_ref[...].astype(BF16)

    step = pl.program_id(0) * pl.num_programs(1) + pl.program_id(1)
    for src, dst, nblk in zip(src_refs, dst_refs, cast_blocks):
        @pl.when(step < nblk)
        def _(src=src, dst=dst):
            dst[...] = src[...].astype(BF16)

    dims = _NT if w_rows_are_outputs else (((1,), (0,)), ((), ()))
    acc = lax.dot_general(x_ref[...], wb_ref[...], dims, preferred_element_type=F32)
    if act == "relu2":
        acc = jnp.square(jnp.maximum(acc, 0.0))
    elif act == "sigmoid":
        acc = jax.nn.sigmoid(acc)
    o_ref[...] = acc.astype(o_ref.dtype)


def _stream_matmul(x, w, w_spec, w_block, n_tiles, *, tm, tn, out_dtype, w_rows_are_outputs, act=None, casts=()):
    m, d = x.shape
    m_tiles = m // tm
    in_specs = [pl.BlockSpec((tm, d), lambda j, i: (i, 0)), w_spec]
    out_specs = [pl.BlockSpec((tm, tn), lambda j, i: (i, j))]
    out_shape = [jax.ShapeDtypeStruct((m, n_tiles * tn), out_dtype)]
    cast_blocks = []
    for src, layer, rps in casts:
        _, rows, cols = src.shape
        nblk = rows // rps
        assert rows % rps == 0 and nblk <= n_tiles * m_tiles
        cast_blocks.append(nblk)
        in_specs.append(pl.BlockSpec((None, rps, cols),
                                     lambda j, i, layer=layer, nblk=nblk: (layer, jnp.minimum(j * m_tiles + i, nblk - 1), 0)))
        out_specs.append(pl.BlockSpec((None, rps, cols),
                                      lambda j, i, nblk=nblk: (0, jnp.minimum(j * m_tiles + i, nblk - 1), 0)))
        out_shape.append(jax.ShapeDtypeStruct((1, rows, cols), BF16))
    res = pl.pallas_call(
        functools.partial(_stream_matmul_body, w_rows_are_outputs=w_rows_are_outputs, act=act,
                          cast_blocks=tuple(cast_blocks)),
        grid=(n_tiles, m_tiles),
        in_specs=in_specs, out_specs=out_specs, out_shape=out_shape,
        scratch_shapes=[pltpu.VMEM(w_block, BF16)],
        compiler_params=_params("arbitrary", "arbitrary"),
        name="stream_matmul_t" if w_rows_are_outputs else "stream_matmul",
    )(x, w, *[c[0] for c in casts])
    return res if casts else res[0]


def _matmul_res_body(*refs, n_k, emit_norm, emit_small, n_first):
    a_ref, w_ref, r_ref = refs[:3]
    refs = refs[3:]
    if emit_norm:
        g_ref = refs[0]
        refs = refs[1:]
    if emit_small:
        ws_ref = refs[0]
        refs = refs[1:]
    o_ref = refs[0]
    k = pl.program_id(1)

    def emit(acc):
        if n_first is not None:
            y = _rms(acc, g_ref[...])
            i = pl.program_id(0)

            @pl.when(i < n_first)
            def _():
                refs[1][...] = y

            @pl.when(i >= n_first)
            def _():
                refs[2][...] = y
            return
        xn = _rms(acc, g_ref[...]).astype(BF16)
        refs[1][...] = xn
        if emit_small:
            refs[2][...] = lax.dot_general(xn, ws_ref[...].astype(BF16), _NT, preferred_element_type=F32)

    if n_k == 1:
        acc = r_ref[...] + jnp.dot(a_ref[...], w_ref[...], preferred_element_type=F32)
        o_ref[...] = acc
        if emit_norm:
            emit(acc)
        return

    @pl.when(k == 0)
    def _():
        o_ref[...] = r_ref[...]

    o_ref[...] += jnp.dot(a_ref[...], w_ref[...], preferred_element_type=F32)

    if emit_norm:
        pl.when(k == n_k - 1)(lambda: emit(o_ref[...]))


def _matmul_res(a, w, l, res, *, tm, tk, gain=None, gain_layer=0, w_small=None, split_rows=None):
    m, k = a.shape
    d = w.shape[-1]
    n_k = k // tk
    emit_norm = gain is not None
    emit_small = w_small is not None
    n_first = None if split_rows is None else split_rows // tm
    in_specs = [pl.BlockSpec((tm, tk), lambda i, kk: (i, kk)),
                pl.BlockSpec((None, tk, d), lambda i, kk: (l, kk, 0)),
                pl.BlockSpec((tm, d), lambda i, kk: (i, 0))]
    out_specs = [pl.BlockSpec((tm, d), lambda i, kk: (i, 0))]
    out_shape = [jax.ShapeDtypeStruct((m, d), F32)]
    args = [a, w, res]
    if emit_norm:
        in_specs.append(_layer(gain_layer, (1, d)))
        args.append(gain)
        if n_first is None:
            out_specs.append(pl.BlockSpec((tm, d), lambda i, kk: (i, 0)))
            out_shape.append(jax.ShapeDtypeStruct((m, d), BF16))
        else:
            assert split_rows % tm == 0 and not emit_small
            out_specs.extend(_two_source_specs(tm, d, n_first))
            out_shape.extend([jax.ShapeDtypeStruct((split_rows, d), F32),
                              jax.ShapeDtypeStruct((m - split_rows, d), F32)])
    if emit_small:
        ns = w_small.shape[-2]
        in_specs.append(_layer(gain_layer, (ns, d)))
        args.append(w_small)
        out_specs.append(pl.BlockSpec((tm, ns), lambda i, kk: (i, 0)))
        out_shape.append(jax.ShapeDtypeStruct((m, ns), F32))
    return pl.pallas_call(
        functools.partial(_matmul_res_body, n_k=n_k, emit_norm=emit_norm, emit_small=emit_small, n_first=n_first),
        grid=(m // tm, n_k),
        in_specs=in_specs, out_specs=out_specs, out_shape=out_shape,
        compiler_params=_params("arbitrary", "arbitrary"),
        name="matmul_res",
    )(*args)


def _merge_body(ol_a, ol_b, od_a, od_b, op_a, op_b, gl_ref, gd_ref, gp_ref, wl_ref, wd_ref, wp_ref, o_ref, *, n_first):
    def branch(o_a, o_b, g_ref_, w_ref_):
        o = _pick_rows(o_a, o_b, n_first)
        return g_ref_[...].astype(F32) * jnp.dot(o, w_ref_[...], preferred_element_type=F32)

    m = (branch(ol_a, ol_b, gl_ref, wl_ref) + branch(od_a, od_b, gd_ref, wd_ref)
         + branch(op_a, op_b, gp_ref, wp_ref))
    o_ref[...] = m.astype(o_ref.dtype)


def _merge(o_lru, o_dn, o_pool, gates, w_lru, w_dn, w_pool, l, *, tm, tn):
    half = o_lru[0].shape[-1]
    m = gates.shape[0]
    n_first = (m - o_lru[1].shape[0]) // tm
    d = w_lru.shape[-1]
    nj = d // tn

    def g_spec(b):
        return pl.BlockSpec((tm, tn), lambda i, j: (i, b * nj + j))

    def w_spec():
        return pl.BlockSpec((None, half, tn), lambda i, j: (l, 0, j))

    o_specs = _two_source_specs(tm, half, n_first)
    return pl.pallas_call(
        functools.partial(_merge_body, n_first=n_first),
        grid=(m // tm, nj),
        in_specs=[*o_specs, *o_specs, *o_specs, g_spec(0), g_spec(1), g_spec(2), w_spec(), w_spec(), w_spec()],
        out_specs=pl.BlockSpec((tm, tn), lambda i, j: (i, j)),
        out_shape=jax.ShapeDtypeStruct((m, d), BF16),
        compiler_params=_params("arbitrary", "arbitrary"),
        name="merge",
    )(*o_lru, *o_dn, *o_pool, gates, gates, gates, w_lru, w_dn, w_pool)


def _softplus(x):
    return jnp.maximum(x, 0.0) + jnp.log1p(jnp.exp(-jnp.abs(x)))


def _shift_rows(cur, prev, k, chained=False):
    ax = cur.ndim - 2
    row = lax.broadcasted_iota(jnp.int32, cur.shape, ax)
    if chained:
        rolled = pltpu.roll(jnp.concatenate([prev[None], cur], axis=0), k, ax)
        return jnp.where(row >= k, rolled[1:], rolled[:-1])
    return jnp.where(row >= k, pltpu.roll(cur, k, ax), pltpu.roll(prev, k, ax))


def _causal_conv(x, prev, cw, chained):
    y = cw[CONV_WIDTH - 1:CONV_WIDTH] * x
    for k in range(1, CONV_WIDTH):
        y = y + cw[CONV_WIDTH - 1 - k:CONV_WIDTH - k] * _shift_rows(x, prev, k, chained)
    return y


def _lru_body(*refs, long_seq):
    for _ in _lru_stages(refs[:11], refs[11:14], refs[14:], long_seq=long_seq, is_first=pl.program_id(1) == 0):
        pass


def _lru_stages(in_refs, out_refs, scratch, *, long_seq, is_first):
    (x_ref, gate_ref, prev_ref, h0_ref, cw_ref, cb_ref, wa_ref, ba_ref, wx_ref, bx_ref, lam_ref) = in_refs
    o_ref, hl_ref, nb_ref = out_refs
    cx_scr, ch_scr = scratch
    g_n, _, c = x_ref.shape
    rows = g_n * SUBLANES
    x = x_ref[...]
    if long_seq:
        @pl.when(is_first)
        def _():
            cx_scr[...] = prev_ref[0]
            ch_scr[...] = h0_ref[0]

        prev = cx_scr[...]
    else:
        prev = prev_ref[...]
    xc = _causal_conv(x, prev, cw_ref[...], long_seq) + cb_ref[...]
    if long_seq:
        cx_scr[...] = x[g_n - 1]
        nb_ref[0] = x[g_n - 1]
    else:
        nb_ref[...] = x
    yield

    xc2 = xc.reshape(rows, c)
    xb = xc2.astype(BF16)
    nblk = c // GATE_TILE

    def gate(w_ref, b_ref):
        parts = [jnp.dot(xb[:, i * GATE_TILE:(i + 1) * GATE_TILE], w_ref[i], preferred_element_type=F32)
                 for i in range(nblk)]
        return jax.nn.sigmoid(jnp.concatenate(parts, axis=1) + b_ref[...])

    r = gate(wa_ref, ba_ref)
    yield
    i_g = gate(wx_ref, bx_ref)
    yield
    log_a = (-LRU_C) * r * _softplus(-lam_ref[...])
    a = jnp.exp(log_a)
    one_minus_a2 = -jnp.tanh(log_a) * (a * a + 1.0)
    b = jnp.sqrt(one_minus_a2) * (i_g * xc2)
    yield

    a3 = a.reshape(g_n, SUBLANES, c)
    b3 = b.reshape(g_n, SUBLANES, c)
    row = lax.broadcasted_iota(jnp.int32, a3.shape, 1)
    s = 1
    while s < SUBLANES:
        m = row >= s
        a_sh = pltpu.roll(a3, s, 1)
        b_sh = pltpu.roll(b3, s, 1)
        b3 = jnp.where(m, a3 * b_sh + b3, b3)
        a3 = jnp.where(m, a3 * a_sh, a3)
        s *= 2
        yield

    h_in = ch_scr[...] if long_seq else None
    h_groups = []
    for g in range(g_n):
        if not long_seq:
            h_in = h0_ref[g]
        hg = a3[g] * h_in + b3[g]
        h_groups.append(hg)
        h_in = hg[SUBLANES - 1:SUBLANES]
        if not long_seq:
            hl_ref[g] = h_in
    if long_seq:
        ch_scr[...] = h_in
        hl_ref[0] = h_in
    yield
    h = jnp.stack(h_groups, axis=0).reshape(rows, c)
    gl = gate_ref[...].reshape(rows, c)
    o_ref[...] = (h * jax.nn.gelu(gl)).astype(o_ref.dtype)


def _lru_call(proj3, col0, prev, h0, weights, l, *, long_seq, n_outer, n_inner, g_n, group0, state_layer):
    cw, cb, wa, ba, wx, bx, lam = weights
    c = cw.shape[-1]
    m = proj3.shape[0] * SUBLANES
    nseq_blk = 1 if long_seq else g_n
    nseq = n_outer * nseq_blk
    gb0 = group0 // g_n

    def x_spec(col):
        return pl.BlockSpec((g_n, SUBLANES, c), lambda i, t: (gb0 + i * n_inner + t, 0, col))

    def state_spec(r):
        if state_layer is None:
            return pl.BlockSpec((nseq_blk, r, c), lambda i, t: (i, 0, 0))
        return pl.BlockSpec((None, nseq_blk, r, c), lambda i, t: (state_layer, i, 0, 0))

    ng = c // GATE_TILE
    in_specs = [x_spec(col0), x_spec(col0 + 1), state_spec(SUBLANES), state_spec(1),
                _layer(l, (CONV_WIDTH, c)), _layer(l, (1, c)),
                _layer(l, (ng, GATE_TILE, GATE_TILE)), _layer(l, (1, c)),
                _layer(l, (ng, GATE_TILE, GATE_TILE)), _layer(l, (1, c)), _layer(l, (1, c))]
    args = [proj3, proj3, prev, h0, cw, cb, wa, ba, wx, bx, lam]
    out_specs = [pl.BlockSpec((g_n * SUBLANES, c), lambda i, t: (i * n_inner + t, 0)),
                 pl.BlockSpec((nseq_blk, 1, c), lambda i, t: (i, 0, 0)),
                 pl.BlockSpec((nseq_blk, SUBLANES, c), lambda i, t: (i, 0, 0))]
    out_shape = [jax.ShapeDtypeStruct((n_outer * n_inner * g_n * SUBLANES, c), BF16),
                 jax.ShapeDtypeStruct((nseq, 1, c), F32),
                 jax.ShapeDtypeStruct((nseq, SUBLANES, c), F32)]
    return pl.pallas_call(
        functools.partial(_lru_body, long_seq=long_seq),
        grid=(n_outer, n_inner),
        in_specs=in_specs, out_specs=out_specs, out_shape=out_shape,
        scratch_shapes=[pltpu.VMEM((SUBLANES, c), F32), pltpu.VMEM((1, c), F32)],
        compiler_params=_params("parallel", "arbitrary"),
        name="lru_long" if long_seq else "lru_short",
    )(*args)


def _pool_body(*refs, long_seq, past_len):
    u_ref, hist_ref, pw_ref, ps_ref, o_ref, nh_ref, c_scr = refs
    g_n, _, c = u_ref.shape
    rows = g_n * SUBLANES
    x = u_ref[...]
    t = pl.program_id(1)
    if long_seq:
        @pl.when(t == 0)
        def _():
            c_scr[0] = hist_ref[0, 0:SUBLANES]
            c_scr[1] = hist_ref[0, SUBLANES:2 * SUBLANES]

        ext = jnp.concatenate([c_scr[...], x], axis=0)
        gs = 1
    else:
        hist = hist_ref[...]
        ext = jnp.concatenate([hist[:, 0:SUBLANES], hist[:, SUBLANES:2 * SUBLANES], x], axis=0)
        gs = g_n
    nh_ref[:, 0:SUBLANES] = ext[-2 * gs:-gs]
    nh_ref[:, SUBLANES:2 * SUBLANES] = ext[-gs:]
    if long_seq:
        c_scr[0] = ext[g_n]
        c_scr[1] = ext[g_n + 1]

    def prev(a):
        return jnp.concatenate([a[:gs], a[:-gs]], axis=0)

    n_grp = len(POOL_WINDOWS)
    pg = c // n_grp
    i0 = lax.broadcasted_iota(jnp.int32, (g_n, SUBLANES, pg), 0)
    i1 = lax.broadcasted_iota(jnp.int32, (g_n, SUBLANES, pg), 1)
    t_abs = (t * g_n + i0) * SUBLANES + i1 if long_seq else i1
    outs = []
    for gi, w in enumerate(POOL_WINDOWS):
        eg = ext[:, :, gi * pg:(gi + 1) * pg]
        s = eg
        k = 1
        while k < w:
            if k < SUBLANES:
                rolled = pltpu.roll(s, k, 1)
                row = lax.broadcasted_iota(jnp.int32, s.shape, 1)
                s = s + jnp.where(row >= k, rolled, prev(rolled))
            else:
                s = s + prev(s)
            k *= 2
        cnt = jnp.minimum(t_abs + (1 + past_len), w).astype(F32)
        d = s[2 * gs:] / cnt - eg[2 * gs:]
        d2 = d.reshape(rows, pg).astype(BF16)
        outs.append(jnp.dot(d2, pw_ref[gi], preferred_element_type=F32))
    y = jnp.concatenate(outs, axis=1) * ps_ref[...]
    o_ref[...] = y.astype(o_ref.dtype)


def _pool_call(proj3, col, hist, pool_w, pool_scale, l, *, long_seq, past_len, n_outer, n_inner, g_n,
               group0, state_layer):
    c = pool_scale.shape[-1]
    m = proj3.shape[0] * SUBLANES
    nseq_blk = 1 if long_seq else g_n
    nseq = n_outer * nseq_blk
    gb0 = group0 // g_n
    pg = c // len(POOL_WINDOWS)
    hrows = 2 * SUBLANES
    if state_layer is None:
        h_spec = pl.BlockSpec((nseq_blk, hrows, c), lambda i, t: (i, 0, 0))
    else:
        h_spec = pl.BlockSpec((None, nseq_blk, hrows, c), lambda i, t: (state_layer, i, 0, 0))
    in_specs = [pl.BlockSpec((g_n, SUBLANES, c), lambda i, t: (gb0 + i * n_inner + t, 0, col)), h_spec,
                _layer(l, (len(POOL_WINDOWS), pg, pg)), _layer(l, (1, c))]
    args = [proj3, hist, pool_w, pool_scale]
    return pl.pallas_call(
        functools.partial(_pool_body, long_seq=long_seq, past_len=past_len),
        grid=(n_outer, n_inner),
        in_specs=in_specs,
        out_specs=[pl.BlockSpec((g_n * SUBLANES, c), lambda i, t: (i * n_inner + t, 0)),
                   pl.BlockSpec((nseq_blk, hrows, c), lambda i, t: (i, 0, 0))],
        out_shape=[jax.ShapeDtypeStruct((n_outer * n_inner * g_n * SUBLANES, c), BF16),
                   jax.ShapeDtypeStruct((nseq, hrows, c), F32)],
        scratch_shapes=[pltpu.VMEM((2, SUBLANES, c), F32)],
        compiler_params=_params("parallel", "arbitrary"),
        name="pool_long" if long_seq else "pool_short",
    )(*args)


CHUNK_STAGE_LAG = 3
SOLVE_PASSES = 1


def _mm(a, b):
    a_hi, b_hi = a.astype(BF16), b.astype(BF16)
    out = jnp.dot(a_hi, b_hi, preferred_element_type=F32)
    if SOLVE_PASSES == 3:
        a_lo = (a - a_hi.astype(F32)).astype(BF16)
        b_lo = (b - b_hi.astype(F32)).astype(BF16)
        out = out + jnp.dot(a_hi, b_lo, preferred_element_type=F32) + jnp.dot(a_lo, b_hi, preferred_element_type=F32)
    return out


def _delta_body(*refs, long_seq, tseq, n_aliased):
    (q_ref, k_ref, v_ref, z_ref, ba_ref, prev_ref, s0_ref, cw_ref, alog_ref, dtb_ref, nw_ref) = refs[:11]
    o_ref, s_ref, nb_ref, cx_scr = refs[11 + n_aliased:]
    g_n, _, c = q_ref.shape
    rows = g_n * SUBLANES
    n_sub = rows // DN_CHUNK
    nseq_chunk = DN_CHUNK // tseq
    heads = c // HEAD_DIM

    if long_seq:
        @pl.when(pl.program_id(1) == 0)
        def _():
            cx_scr[...] = prev_ref[0]
            s_ref[...] = s0_ref[...]

    cw = cw_ref[...]
    streams = []
    for idx, x_ref in enumerate((q_ref, k_ref, v_ref)):
        x = x_ref[...]
        lo, hi = idx * c, (idx + 1) * c
        if long_seq:
            prev = cx_scr[:, lo:hi]
            cx_scr[:, lo:hi] = x[g_n - 1]
            nb_ref[0, :, lo:hi] = x[g_n - 1]
        else:
            prev = prev_ref[:, :, lo:hi]
            nb_ref[:, :, lo:hi] = x
        y = _causal_conv(x, prev, cw[:, lo:hi], long_seq)
        streams.append((y * jax.nn.sigmoid(y)).reshape(rows, c))
    q_all, k_all, v_all = streams
    z_all = z_ref[...].reshape(rows, c)

    ba = ba_ref[...]
    beta_all = jax.nn.sigmoid(ba[:, 0:HEAD_DIM])
    g_all = -jnp.exp(alog_ref[...]) * _softplus(ba[:, HEAD_DIM:2 * HEAD_DIM] + dtb_ref[...])

    ri = lax.broadcasted_iota(jnp.int32, (DN_CHUNK, DN_CHUNK), 0)
    ci = lax.broadcasted_iota(jnp.int32, (DN_CHUNK, DN_CHUNK), 1)
    same = (ri // tseq) == (ci // tseq)
    incl = same & (ci <= ri)
    strict = same & (ci < ri)
    eye = (ri == ci).astype(F32)
    incl_f, same_f = incl.astype(F32), same.astype(F32)

    state = [s_ref[0, h] for h in range(heads)] if long_seq else None
    masks = (ri, ci, incl, strict, eye, incl_f, same_f)
    waiting = [_delta_chunk([(sub, h) for h in range(heads)], state, masks,
                            (q_all, k_all, v_all, z_all, beta_all, g_all), (s0_ref, s_ref, nw_ref, o_ref),
                            long_seq, tseq) for sub in range(n_sub)]
    running, rnd = [], 0
    while waiting or running:
        if waiting and rnd % CHUNK_STAGE_LAG == 0:
            running.append(waiting.pop(0))
        for gen in list(running):
            if next(gen, "done") == "done":
                running.remove(gen)
        rnd += 1
    if long_seq:
        for h in range(heads):
            s_ref[0, h] = state[h]


def _delta_chunk(pairs, state, masks, streams, refs, long_seq, tseq):
    ri, ci, incl, strict, eye, incl_f, same_f = masks
    q_all, k_all, v_all, z_all, beta_all, g_all = streams
    s0_ref, s_ref, nw_ref, o_ref = refs
    nseq_chunk = DN_CHUNK // tseq
    gamma_s, gtot_s, gamma_ts = {}, {}, {}
    for sub in sorted({sub for sub, _ in pairs}):
        g_sub = g_all[sub * DN_CHUNK:(sub + 1) * DN_CHUNK]
        gamma = jnp.dot(incl_f, g_sub, precision=_HI, preferred_element_type=F32)
        gamma_s[sub] = gamma
        gtot_s[sub] = jnp.dot(same_f, g_sub, precision=_HI, preferred_element_type=F32)
        gamma_ts[sub] = gamma.T

    q_l, k_l, v_l, beta_l, gcol_l, gtot_l, decay_l, qb_l, kb_l, a_l = ([] for _ in range(10))
    for sub, h in pairs:
        rs = slice(sub * DN_CHUNK, (sub + 1) * DN_CHUNK)
        hs = slice(h * HEAD_DIM, (h + 1) * HEAD_DIM)
        q_h, k_h = q_all[rs, hs], k_all[rs, hs]
        q_h = q_h * lax.rsqrt(jnp.sum(q_h * q_h, axis=-1, keepdims=True) + EPS) * (HEAD_DIM ** -0.5)
        k_h = k_h * lax.rsqrt(jnp.sum(k_h * k_h, axis=-1, keepdims=True) + EPS)
        gcol = gamma_s[sub][:, h:h + 1]
        decay = jnp.where(incl, jnp.exp(gcol - gamma_ts[sub][h:h + 1, :]), 0.0)
        qb, kb = q_h.astype(BF16), k_h.astype(BF16)
        beta = beta_all[rs, h:h + 1]
        kk = lax.dot_general(kb, kb, _NT, preferred_element_type=F32)
        q_l.append(q_h)
        k_l.append(k_h)
        v_l.append(v_all[rs, hs])
        beta_l.append(beta)
        gcol_l.append(gcol)
        gtot_l.append(gtot_s[sub][:, h:h + 1])
        decay_l.append(decay)
        qb_l.append(qb)
        kb_l.append(kb)
        a_l.append(jnp.where(strict, beta * decay * kk, 0.0))
    yield

    t_l = [eye - jnp.where(ri // 2 == ci // 2, a, 0.0) for a in a_l]
    s = 2
    while s < tseq:
        off_mask = (ri // (2 * s) == ci // (2 * s)) & (ri % (2 * s) >= s) & (ci % (2 * s) < s)
        t_off = [_mm(t, jnp.where(off_mask, a, 0.0)) for t, a in zip(t_l, a_l)]
        t_l = [t - _mm(to, t) for t, to in zip(t_l, t_off)]
        s *= 2
        yield

    eg_l = [jnp.exp(g) for g in gcol_l]
    sol_l = [_mm(t, jnp.concatenate([b * v, (b * e) * k], axis=1))
             for t, b, v, e, k in zip(t_l, beta_l, v_l, eg_l, k_l)]
    qk_l = [(lax.dot_general(qb, kb, _NT, preferred_element_type=F32) * d).astype(BF16)
            for qb, kb, d in zip(qb_l, kb_l, decay_l)]
    qg_l = [(q * e).astype(BF16) for q, e in zip(q_l, eg_l)]
    kd_l = [(k * jnp.exp(gt - g)).astype(BF16) for k, gt, g in zip(k_l, gtot_l, gcol_l)]
    yield

    w_l, o_l, s_old = [], [], []
    for i, (sub, h) in enumerate(pairs):
        u_c, wkb = sol_l[i][:, :HEAD_DIM], sol_l[i][:, HEAD_DIM:].astype(BF16)
        w_parts, o_parts, s_mats = [], [], []
        for sq in range(nseq_chunk):
            rq = slice(sq * tseq, (sq + 1) * tseq)
            s_mat = state[h] if long_seq else s0_ref[sub * nseq_chunk + sq, h]
            sb = s_mat.astype(BF16)
            s_mats.append(s_mat)
            w_parts.append(u_c[rq] - jnp.dot(wkb[rq], sb, preferred_element_type=F32))
            o_parts.append(jnp.dot(qg_l[i][rq], sb, preferred_element_type=F32))
        w_l.append((jnp.concatenate(w_parts, axis=0) if nseq_chunk > 1 else w_parts[0]).astype(BF16))
        o_l.append(jnp.concatenate(o_parts, axis=0) if nseq_chunk > 1 else o_parts[0])
        s_old.append(s_mats)
    yield

    for i, (sub, h) in enumerate(pairs):
        w = w_l[i]
        for sq in range(nseq_chunk):
            rq = slice(sq * tseq, (sq + 1) * tseq)
            g_last = jnp.exp(gtot_l[i][sq * tseq:sq * tseq + 1])
            s_upd = g_last * s_old[i][sq] + lax.dot_general(kd_l[i][rq], w[rq], _TN, preferred_element_type=F32)
            if long_seq:
                state[h] = s_upd
            else:
                s_ref[sub * nseq_chunk + sq, h] = s_upd
        o = o_l[i] + jnp.dot(qk_l[i], w, preferred_element_type=F32)
        o = o * lax.rsqrt(jnp.mean(o * o, axis=-1, keepdims=True) + EPS) * nw_ref[...]
        rs = slice(sub * DN_CHUNK, (sub + 1) * DN_CHUNK)
        hs = slice(h * HEAD_DIM, (h + 1) * HEAD_DIM)
        z_h = z_all[rs, hs]
        o_ref[rs, hs] = (o * (z_h * jax.nn.sigmoid(z_h))).astype(o_ref.dtype)


def _delta_call(proj3, col0, ba, prev, s0, conv_w, alog, dtb, norm_w, l, *, long_seq, tseq, n_outer, n_inner,
                group0, state_layer, n_sub, s_stack=None):
    c = conv_w.shape[-1] // 3
    heads = c // HEAD_DIM
    blk_rows = n_sub * DN_CHUNK
    g_n = blk_rows // SUBLANES
    m = proj3.shape[0] * SUBLANES
    nseq_blk = 1 if long_seq else blk_rows // tseq
    nseq = n_outer * nseq_blk
    gb0 = group0 // g_n

    def x_spec(col):
        return pl.BlockSpec((g_n, SUBLANES, c), lambda i, t: (gb0 + i * n_inner + t, 0, col))

    if state_layer is None:
        p_spec = pl.BlockSpec((nseq_blk, SUBLANES, 3 * c), lambda i, t: (i, 0, 0))
        s_spec = pl.BlockSpec((nseq_blk, heads, HEAD_DIM, HEAD_DIM), lambda i, t: (i, 0, 0, 0))
    else:
        p_spec = pl.BlockSpec((None, nseq_blk, SUBLANES, 3 * c), lambda i, t: (state_layer, i, 0, 0))
        s_spec = pl.BlockSpec((None, nseq_blk, heads, HEAD_DIM, HEAD_DIM), lambda i, t: (state_layer, i, 0, 0, 0))
    in_specs = [x_spec(col0), x_spec(col0 + 1), x_spec(col0 + 2), x_spec(col0 + 3),
                pl.BlockSpec((blk_rows, 2 * HEAD_DIM), lambda i, t: (gb0 + i * n_inner + t, 0)),
                p_spec, s_spec,
                _layer(l, (CONV_WIDTH, 3 * c)), _layer(l, (1, HEAD_DIM)), _layer(l, (1, HEAD_DIM)),
                _layer(l, (1, HEAD_DIM))]
    args = [proj3, proj3, proj3, proj3, ba, prev, s0, conv_w, alog, dtb, norm_w]
    aliases = {}
    s_block = (nseq_blk, heads, HEAD_DIM, HEAD_DIM)
    if s_stack is None:
        s_out_spec = pl.BlockSpec(s_block, lambda i, t: (i, 0, 0, 0))
        s_out_shape = jax.ShapeDtypeStruct((nseq,) + s_block[1:], F32)
    else:
        depth, stacked = s_stack
        s_out_spec = pl.BlockSpec((None,) + s_block, lambda i, t: (l, i, 0, 0, 0))
        s_out_shape = jax.ShapeDtypeStruct((depth, nseq) + s_block[1:], F32)
        if stacked is not None:
            in_specs.append(pl.BlockSpec(memory_space=pl.ANY))
            args.append(stacked)
            aliases[len(args) - 1] = 1
    return pl.pallas_call(
        functools.partial(_delta_body, long_seq=long_seq, tseq=tseq, n_aliased=len(aliases)),
        grid=(n_outer, n_inner),
        in_specs=in_specs,
        out_specs=[pl.BlockSpec((blk_rows, c), lambda i, t: (i * n_inner + t, 0)),
                   s_out_spec,
                   pl.BlockSpec((nseq_blk, SUBLANES, 3 * c), lambda i, t: (i, 0, 0))],
        out_shape=[jax.ShapeDtypeStruct((n_outer * n_inner * blk_rows, c), BF16),
                   s_out_shape,
                   jax.ShapeDtypeStruct((nseq, SUBLANES, 3 * c), F32)],
        scratch_shapes=[pltpu.VMEM((SUBLANES, 3 * c), F32)],
        input_output_aliases=aliases,
        compiler_params=_params("parallel", "arbitrary"),
        name="delta_long" if long_seq else "delta_short",
    )(*args)


def _cast_rows(w, steps):
    rows = w.shape[1]
    nblk = 1
    while nblk * 2 <= steps and rows % (nblk * 2) == 0 and rows // (nblk * 2) >= 2 * SUBLANES:
        nblk *= 2
    return rows // nblk


def _pad_front(a, rows):
    pad = [(0, 0)] * a.ndim
    pad[-2] = (rows - a.shape[-2], 0)
    return jnp.pad(a, pad)


def _block_diag_tiles(w):
    depth, nb, b, _ = w.shape
    per = GATE_TILE // b
    w5 = w.reshape(depth, nb // per, per, b, b)
    eye = jnp.eye(per, dtype=w.dtype)
    t = jnp.einsum('lijcd,jk->lijckd', w5, eye)
    return t.reshape(depth, nb // per, GATE_TILE, GATE_TILE).astype(BF16)


def kernel(x_prompt, x_sample, state_lru_h, state_lru_conv, state_dn_s, state_dn_conv, state_pool, norm_mix, w_in, lru_conv_w, lru_conv_b, lru_gate_a_w, lru_gate_a_b, lru_gate_x_w, lru_gate_x_b, lru_lambda, dn_conv_w, dn_a_log, dn_dt_bias, dn_norm_w, pool_w, pool_scale, w_br_lru, w_br_dn, w_br_pool, w_out, norm_mlp, w_up, w_down, norm_final):
    bp, tp, d = x_prompt.shape
    bs, ts, _ = x_sample.shape
    depth = w_in.shape[0]
    half = d // 2
    heads = half // HEAD_DIM
    mp, ms = bp * tp, bs * ts
    m = mp + ms
    assert ts == SUBLANES and tp % DN_CHUNK == 0 and bs % (DN_CHUNK // ts) == 0
    assert lru_gate_a_w.shape[-1] == LRU_BLOCK and half % GATE_TILE == 0
    assert state_pool.shape[-2] == POOL_BUF

    n_pre = 6 * half
    ba0 = n_pre
    pool0 = ba0 + 2 * heads
    gates0 = pool0 + half
    w_in_t = jnp.swapaxes(w_in, 1, 2)
    n_gate_tiles, n_stream_tiles = 3 * d // half, n_pre // half + 1
    assert gates0 % SUBLANES == 0 and pool0 % SUBLANES == 0

    def gate_row0(j):
        return pl.multiple_of(gates0 + j * half, SUBLANES)

    def stream_row0(j):
        return pl.multiple_of(jnp.where(j < n_stream_tiles - 1, j * half, pool0), SUBLANES)

    row_fill = ((0, 0), (0, HEAD_DIM - heads), (0, 0))
    w_ba = jnp.concatenate([jnp.pad(w_in_t[:, ba0:ba0 + heads], row_fill),
                            jnp.pad(w_in_t[:, ba0 + heads:ba0 + 2 * heads], row_fill)], axis=1)
    col0 = 0

    def row(a):
        return a.reshape(depth, 1, a.shape[-1])

    def lane_pad(a):
        return jnp.pad(a, ((0, 0), (0, HEAD_DIM - a.shape[-1]))).reshape(depth, 1, HEAD_DIM)

    lru_w = (lru_conv_w, row(lru_conv_b), _block_diag_tiles(lru_gate_a_w), row(lru_gate_a_b),
             _block_diag_tiles(lru_gate_x_w), row(lru_gate_x_b), row(lru_lambda))
    alog, dtb, dn_nw = lane_pad(dn_a_log), lane_pad(dn_dt_bias), row(dn_norm_w)
    pool_wb, pool_sc = pool_w.astype(BF16), row(pool_scale)
    g_mix, g_mlp = row(norm_mix), row(norm_mlp)

    s_lru_h0 = state_lru_h.reshape(depth, bs, 1, half)
    s_lru_prev = _pad_front(state_lru_conv, SUBLANES)
    s_dn_prev = _pad_front(state_dn_conv, SUBLANES)
    s_pool_hist = _pad_front(state_pool, 2 * SUBLANES)
    p_lru_h0 = jnp.zeros((bp, 1, half), F32)
    p_lru_prev = jnp.zeros((bp, SUBLANES, half), F32)
    p_dn_prev = jnp.zeros((bp, SUBLANES, 3 * half), F32)
    p_dn_s0 = jnp.zeros((bp, heads, HEAD_DIM, HEAD_DIM), F32)
    p_pool_hist = jnp.zeros((bp, 2 * SUBLANES, half), F32)

    tm = 512 if (mp % 512 == 0 and ms % 512 == 0) else 256
    tm_big = next(t for t in (1536, 1024, tm) if m % t == 0)
    lru_g = 64
    assert (tp // SUBLANES) % lru_g == 0 and bs % lru_g == 0 and m % tm == 0 and mp % tm == 0
    n_t = tp // SUBLANES // lru_g
    long_kw = dict(long_seq=True, n_outer=bp, n_inner=n_t, g_n=lru_g, group0=0, state_layer=None)
    short_kw = dict(long_seq=False, n_outer=bs // lru_g, n_inner=1, g_n=lru_g, group0=mp // SUBLANES)
    seq_per_blk = DN_CHUNK // ts
    dn_sub, dn_sub_s = 8, 2
    assert tp % (dn_sub * DN_CHUNK) == 0 and bs % (dn_sub_s * seq_per_blk) == 0

    p_states, s_states = [], []
    s_dn_stack = None
    norm_fin = norm_final.reshape(1, 1, d)
    x, xn, ba = _norm_cast(x_prompt.reshape(mp, d), x_sample.reshape(ms, d), g_mix, w_ba, 0, tm=tm)
    for l in range(depth):
        w_rows = (None, pl.Element(half), pl.Element(d))
        m_tiles = m // tm_big
        proj, w_br_dn_b, w_br_pool_b = _stream_matmul(
            xn, w_in_t, pl.BlockSpec(w_rows, lambda j, i, l=l: (l, stream_row0(j), 0)), (half, d), n_stream_tiles,
            tm=tm_big, tn=half, out_dtype=F32, w_rows_are_outputs=True,
            casts=[(w, l, _cast_rows(w, n_stream_tiles * m_tiles)) for w in (w_br_dn, w_br_pool)])
        proj3 = proj.reshape(m // SUBLANES, SUBLANES, proj.shape[-1])

        gates, w_out_b, w_br_lru_b = _stream_matmul(
            xn, w_in_t, pl.BlockSpec(w_rows, lambda j, i, l=l: (l, gate_row0(j), 0)), (half, d), n_gate_tiles,
            tm=tm_big, tn=half, out_dtype=BF16, w_rows_are_outputs=True, act="sigmoid",
            casts=[(w, l, _cast_rows(w, n_gate_tiles * m_tiles)) for w in (w_out, w_br_lru)])
        o_lru_p, p_h, p_cb = _lru_call(proj3, col0, p_lru_prev, p_lru_h0, lru_w, l, **long_kw)
        o_lru_s, s_h, s_cb = _lru_call(proj3, col0, s_lru_prev, s_lru_h0, lru_w, l, state_layer=l, **short_kw)

        o_dn_p, p_s, p_db = _delta_call(proj3, col0 + 2, ba, p_dn_prev, p_dn_s0, dn_conv_w, alog, dtb, dn_nw, l,
                                        long_seq=True, tseq=DN_CHUNK, n_outer=bp, n_inner=tp // (dn_sub * DN_CHUNK),
                                        group0=0, state_layer=None, n_sub=dn_sub)
        o_dn_s, s_dn_stack, s_db = _delta_call(proj3, col0 + 2, ba, s_dn_prev, state_dn_s, dn_conv_w, alog, dtb,
                                               dn_nw, l, long_seq=False, tseq=ts,
                                               n_outer=bs // (dn_sub_s * seq_per_blk), n_inner=1,
                                               group0=mp // SUBLANES, state_layer=l, n_sub=dn_sub_s,
                                               s_stack=(depth, s_dn_stack))

        o_pool_p, p_ph = _pool_call(proj3, col0 + 6, p_pool_hist, pool_wb, pool_sc, l, past_len=0, **long_kw)
        o_pool_s, s_ph = _pool_call(proj3, col0 + 6, s_pool_hist, pool_wb, pool_sc, l, past_len=PAST_LEN,
                                    state_layer=l, **short_kw)

        merged = _merge((o_lru_p, o_lru_s), (o_dn_p, o_dn_s), (o_pool_p, o_pool_s), gates,
                        w_br_lru_b, w_br_dn_b, w_br_pool_b, 0, tm=tm, tn=d)
        x, xn = _matmul_res(merged, w_out_b, 0, x, tm=tm, tk=d, gain=g_mlp, gain_layer=l)
        w_up_spec = pl.BlockSpec((None, d, half), lambda j, i, l=l: (l, 0, j))
        n_up_tiles = w_up.shape[-1] // half
        hm, w_down_b = _stream_matmul(xn, w_up, w_up_spec, (d, half), n_up_tiles, tm=tm_big, tn=half,
                                      out_dtype=BF16, w_rows_are_outputs=False, act="relu2",
                                      casts=[(w_down, l, _cast_rows(w_down, n_up_tiles * m_tiles))])
        if l + 1 < depth:
            x, xn, ba = _matmul_res(hm, w_down_b, 0, x, tm=tm, tk=d, gain=g_mix, gain_layer=l + 1, w_small=w_ba)
        else:
            _, y_prompt, y_sample = _matmul_res(hm, w_down_b, 0, x, tm=tm, tk=half, gain=norm_fin, gain_layer=0,
                                                split_rows=mp)
        p_states.append((p_h, p_cb, p_s, p_db, p_ph))
        s_states.append((s_h, s_cb, None, s_db, s_ph))

    y_prompt = y_prompt.reshape(bp, tp, d)
    y_sample = y_sample.reshape(bs, ts, d)

    def collect(states, nseq, sm=None):
        tail = CONV_WIDTH - 1
        h = jnp.stack([s[0] for s in states]).reshape(depth, nseq, half)
        cb = jnp.stack([s[1] for s in states])[:, :, SUBLANES - tail:]
        if sm is None:
            sm = jnp.stack([s[2] for s in states])
        db = jnp.stack([s[3] for s in states])[:, :, SUBLANES - tail:]
        ph = jnp.stack([s[4] for s in states])[:, :, 2 * SUBLANES - POOL_BUF:]
        return h, cb, sm, db, ph

    return (y_prompt, y_sample) + collect(p_states, bp) + collect(s_states, bs, s_dn_stack)
```

```python
import functools

import jax
import jax.numpy as jnp
from jax import lax
from jax.experimental import pallas as pl
from jax.experimental.pallas import tpu as pltpu

F32 = jnp.float32
BF16 = jnp.bfloat16

EPS = 1e-6
LRU_C = 8.0
CONV_WIDTH = 4
HEAD_DIM = 128
LRU_BLOCK = 64
POOL_WINDOWS = (2, 4, 8, 16)
POOL_BUF = 15
DN_CHUNK = 64
PAST_LEN = 16384
SUBLANES = 8
GATE_TILE = 256
VMEM_LIMIT = 56 * 1024 * 1024

_NT = (((1,), (1,)), ((), ()))
_TN = (((0,), (0,)), ((), ()))
_HI = lax.Precision.HIGHEST


def _params(*sem):
    return pltpu.CompilerParams(dimension_semantics=sem, vmem_limit_bytes=VMEM_LIMIT)


def _full(shape):
    return pl.BlockSpec(shape, lambda *_: (0,) * len(shape))


def _layer(l, shape):
    return pl.BlockSpec((None,) + shape, lambda *_: (l,) + (0,) * len(shape))


def _rms(x, gain):
    return x * lax.rsqrt(jnp.mean(x * x, axis=-1, keepdims=True) + EPS) * gain


def _two_source_specs(tm, c, n_first):
    first = pl.BlockSpec((tm, c), lambda i, *_: (jnp.minimum(i, n_first - 1), 0))
    second = pl.BlockSpec((tm, c), lambda i, *_: (jnp.maximum(i - n_first, 0), 0))
    return first, second


def _pick_rows(first_ref, second_ref, n_first):
    return jnp.where(pl.program_id(0) < n_first, first_ref[...], second_ref[...])


def _norm_cast_body(xa_ref, xb_ref, g_ref, ws_ref, x_ref, xn_ref, os_ref, *, n_first):
    x = _pick_rows(xa_ref, xb_ref, n_first)
    x_ref[...] = x
    xn = _rms(x, g_ref[...]).astype(BF16)
    xn_ref[...] = xn
    os_ref[...] = lax.dot_general(xn, ws_ref[...].astype(BF16), _NT, preferred_element_type=F32)


def _norm_cast(xa, xb, gain, w_small, l, *, tm):
    d = xa.shape[-1]
    m = xa.shape[0] + xb.shape[0]
    n_first = xa.shape[0] // tm
    ns = w_small.shape[-2]
    return pl.pallas_call(
        functools.partial(_norm_cast_body, n_first=n_first),
        grid=(m // tm,),
        in_specs=[*_two_source_specs(tm, d, n_first), _layer(l, (1, d)), _layer(l, (ns, d))],
        out_specs=[pl.BlockSpec((tm, d), lambda i: (i, 0)), pl.BlockSpec((tm, d), lambda i: (i, 0)),
                   pl.BlockSpec((tm, ns), lambda i: (i, 0))],
        out_shape=[jax.ShapeDtypeStruct((m, d), F32), jax.ShapeDtypeStruct((m, d), BF16),
                   jax.ShapeDtypeStruct((m, ns), F32)],
        compiler_params=_params("arbitrary"),
        name="norm_cast",
    )(xa, xb, gain, w_small)


def _stream_matmul_body(*refs, w_rows_are_outputs, act, cast_blocks):
    n_cast = len(cast_blocks)
    x_ref, w_ref = refs[:2]
    src_refs = refs[2:2 + n_cast]
    o_ref = refs[2 + n_cast]
    dst_refs = refs[3 + n_cast:3 + 2 * n_cast]
    wb_ref = refs[3 + 2 * n_cast]

    @pl.when(pl.program_id(1) == 0)
    def _():
        wb_ref[...] = w_ref[...].astype(BF16)

    step = pl.program_id(0) * pl.num_programs(1) + pl.program_id(1)
    for src, dst, nblk in zip(src_refs, dst_refs, cast_blocks):
        @pl.when(step < nblk)
        def _(src=src, dst=dst):
            dst[...] = src[...].astype(BF16)

    dims = _NT if w_rows_are_outputs else (((1,), (0,)), ((), ()))
    acc = lax.dot_general(x_ref[...], wb_ref[...], dims, preferred_element_type=F32)
    if act == "relu2":
        acc = jnp.square(jnp.maximum(acc, 0.0))
    elif act == "sigmoid":
        acc = jax.nn.sigmoid(acc)
    o_ref[...] = acc.astype(o_ref.dtype)


def _stream_matmul(x, w, w_spec, w_block, n_tiles, *, tm, tn, out_dtype, w_rows_are_outputs, act=None, casts=()):
    m, d = x.shape
    m_tiles = m // tm
    in_specs = [pl.BlockSpec((tm, d), lambda j, i: (i, 0)), w_spec]
    out_specs = [pl.BlockSpec((tm, tn), lambda j, i: (i, j))]
    out_shape = [jax.ShapeDtypeStruct((m, n_tiles * tn), out_dtype)]
    cast_blocks = []
    for src, layer, rps in casts:
        _, rows, cols = src.shape
        nblk = rows // rps
        assert rows % rps == 0 and nblk <= n_tiles * m_tiles
        cast_blocks.append(nblk)
        in_specs.append(pl.BlockSpec((None, rps, cols),
                                     lambda j, i, layer=layer, nblk=nblk: (layer, jnp.minimum(j * m_tiles + i, nblk - 1), 0)))
        out_specs.append(pl.BlockSpec((None, rps, cols),
                                      lambda j, i, nblk=nblk: (0, jnp.minimum(j * m_tiles + i, nblk - 1), 0)))
        out_shape.append(jax.ShapeDtypeStruct((1, rows, cols), BF16))
    res = pl.pallas_call(
        functools.partial(_stream_matmul_body, w_rows_are_outputs=w_rows_are_outputs, act=act,
                          cast_blocks=tuple(cast_blocks)),
        grid=(n_tiles, m_tiles),
        in_specs=in_specs, out_specs=out_specs, out_shape=out_shape,
        scratch_shapes=[pltpu.VMEM(w_block, BF16)],
        compiler_params=_params("arbitrary", "arbitrary"),
        name="stream_matmul_t" if w_rows_are_outputs else "stream_matmul",
    )(x, w, *[c[0] for c in casts])
    return res if casts else res[0]


def _matmul_res_body(*refs, n_k, emit_norm, emit_small, n_first):
    a_ref, w_ref, r_ref = refs[:3]
    refs = refs[3:]
    if emit_norm:
        g_ref = refs[0]
        refs = refs[1:]
    if emit_small:
        ws_ref = refs[0]
        refs = refs[1:]
    o_ref = refs[0]
    k = pl.program_id(1)

    def emit(acc):
        if n_first is not None:
            y = _rms(acc, g_ref[...])
            i = pl.program_id(0)

            @pl.when(i < n_first)
            def _():
                refs[1][...] = y

            @pl.when(i >= n_first)
            def _():
                refs[2][...] = y
            return
        xn = _rms(acc, g_ref[...]).astype(BF16)
        refs[1][...] = xn
        if emit_small:
            refs[2][...] = lax.dot_general(xn, ws_ref[...].astype(BF16), _NT, preferred_element_type=F32)

    if n_k == 1:
        acc = r_ref[...] + jnp.dot(a_ref[...], w_ref[...], preferred_element_type=F32)
        o_ref[...] = acc
        if emit_norm:
            emit(acc)
        return

    @pl.when(k == 0)
    def _():
        o_ref[...] = r_ref[...]

    o_ref[...] += jnp.dot(a_ref[...], w_ref[...], preferred_element_type=F32)

    if emit_norm:
        pl.when(k == n_k - 1)(lambda: emit(o_ref[...]))


def _matmul_res(a, w, l, res, *, tm, tk, gain=None, gain_layer=0, w_small=None, split_rows=None):
    m, k = a.shape
    d = w.shape[-1]
    n_k = k // tk
    emit_norm = gain is not None
    emit_small = w_small is not None
    n_first = None if split_rows is None else split_rows // tm
    in_specs = [pl.BlockSpec((tm, tk), lambda i, kk: (i, kk)),
                pl.BlockSpec((None, tk, d), lambda i, kk: (l, kk, 0)),
                pl.BlockSpec((tm, d), lambda i, kk: (i, 0))]
    out_specs = [pl.BlockSpec((tm, d), lambda i, kk: (i, 0))]
    out_shape = [jax.ShapeDtypeStruct((m, d), F32)]
    args = [a, w, res]
    if emit_norm:
        in_specs.append(_layer(gain_layer, (1, d)))
        args.append(gain)
        if n_first is None:
            out_specs.append(pl.BlockSpec((tm, d), lambda i, kk: (i, 0)))
            out_shape.append(jax.ShapeDtypeStruct((m, d), BF16))
        else:
            assert split_rows % tm == 0 and not emit_small
            out_specs.extend(_two_source_specs(tm, d, n_first))
            out_shape.extend([jax.ShapeDtypeStruct((split_rows, d), F32),
                              jax.ShapeDtypeStruct((m - split_rows, d), F32)])
    if emit_small:
        ns = w_small.shape[-2]
        in_specs.append(_layer(gain_layer, (ns, d)))
        args.append(w_small)
        out_specs.append(pl.BlockSpec((tm, ns), lambda i, kk: (i, 0)))
        out_shape.append(jax.ShapeDtypeStruct((m, ns), F32))
    return pl.pallas_call(
        functools.partial(_matmul_res_body, n_k=n_k, emit_norm=emit_norm, emit_small=emit_small, n_first=n_first),
        grid=(m // tm, n_k),
        in_specs=in_specs, out_specs=out_specs, out_shape=out_shape,
        compiler_params=_params("arbitrary", "arbitrary"),
        name="matmul_res",
    )(*args)


def _merge_body(ol_a, ol_b, od_a, od_b, op_a, op_b, gl_ref, gd_ref, gp_ref, wl_ref, wd_ref, wp_ref, o_ref, *, n_first):
    def branch(o_a, o_b, g_ref_, w_ref_):
        o = _pick_rows(o_a, o_b, n_first)
        return g_ref_[...].astype(F32) * jnp.dot(o, w_ref_[...], preferred_element_type=F32)

    m = (branch(ol_a, ol_b, gl_ref, wl_ref) + branch(od_a, od_b, gd_ref, wd_ref)
         + branch(op_a, op_b, gp_ref, wp_ref))
    o_ref[...] = m.astype(o_ref.dtype)


def _merge(o_lru, o_dn, o_pool, gates, w_lru, w_dn, w_pool, l, *, tm, tn):
    half = o_lru[0].shape[-1]
    m = gates.shape[0]
    n_first = (m - o_lru[1].shape[0]) // tm
    d = w_lru.shape[-1]
    nj = d // tn

    def g_spec(b):
        return pl.BlockSpec((tm, tn), lambda i, j: (i, b * nj + j))

    def w_spec():
        return pl.BlockSpec((None, half, tn), lambda i, j: (l, 0, j))

    o_specs = _two_source_specs(tm, half, n_first)
    return pl.pallas_call(
        functools.partial(_merge_body, n_first=n_first),
        grid=(m // tm, nj),
        in_specs=[*o_specs, *o_specs, *o_specs, g_spec(0), g_spec(1), g_spec(2), w_spec(), w_spec(), w_spec()],
        out_specs=pl.BlockSpec((tm, tn), lambda i, j: (i, j)),
        out_shape=jax.ShapeDtypeStruct((m, d), BF16),
        compiler_params=_params("arbitrary", "arbitrary"),
        name="merge",
    )(*o_lru, *o_dn, *o_pool, gates, gates, gates, w_lru, w_dn, w_pool)


def _softplus(x):
    return jnp.maximum(x, 0.0) + jnp.log1p(jnp.exp(-jnp.abs(x)))


def _shift_rows(cur, prev, k, chained=False):
    ax = cur.ndim - 2
    row = lax.broadcasted_iota(jnp.int32, cur.shape, ax)
    if chained:
        rolled = pltpu.roll(jnp.concatenate([prev[None], cur], axis=0), k, ax)
        return jnp.where(row >= k, rolled[1:], rolled[:-1])
    return jnp.where(row >= k, pltpu.roll(cur, k, ax), pltpu.roll(prev, k, ax))


def _rows_before(state_ref, tile_rows):
    r, nb, c = state_ref.shape
    row = lax.broadcasted_iota(jnp.int32, (tile_rows, c), 0)
    tiles = []
    for b in range(nb):
        t = jnp.zeros((tile_rows, c), F32)
        for j in range(r):
            t = jnp.where(row == tile_rows - r + j, state_ref[j, b:b + 1, :], t)
        tiles.append(t)
    return jnp.stack(tiles, axis=0)


def _causal_conv(x, prev, cw, chained):
    y = cw[CONV_WIDTH - 1:CONV_WIDTH] * x
    for k in range(1, CONV_WIDTH):
        y = y + cw[CONV_WIDTH - 1 - k:CONV_WIDTH - k] * _shift_rows(x, prev, k, chained)
    return y


def _lru_body(*refs, long_seq):
    for _ in _lru_stages(refs[:11], refs[11:14], refs[14:], long_seq=long_seq, is_first=pl.program_id(1) == 0):
        pass


def _lru_stages(in_refs, out_refs, scratch, *, long_seq, is_first):
    (x_ref, gate_ref, prev_ref, h0_ref, cw_ref, cb_ref, wa_ref, ba_ref, wx_ref, bx_ref, lam_ref) = in_refs
    o_ref, hl_ref, nb_ref = out_refs
    cx_scr, ch_scr = scratch
    g_n, _, c = x_ref.shape
    rows = g_n * SUBLANES
    x = x_ref[...]
    if long_seq:
        @pl.when(is_first)
        def _():
            cx_scr[...] = prev_ref[0]
            ch_scr[...] = h0_ref[0]

        prev = cx_scr[...]
    else:
        prev = _rows_before(prev_ref, SUBLANES)
    xc = _causal_conv(x, prev, cw_ref[...], long_seq) + cb_ref[...]
    if long_seq:
        cx_scr[...] = x[g_n - 1]
        nb_ref[0] = x[g_n - 1]
    else:
        nb_ref[...] = x
    yield

    xc2 = xc.reshape(rows, c)
    xb = xc2.astype(BF16)
    nblk = c // GATE_TILE

    def gate(w_ref, b_ref):
        parts = [jnp.dot(xb[:, i * GATE_TILE:(i + 1) * GATE_TILE], w_ref[i], preferred_element_type=F32)
                 for i in range(nblk)]
        return jax.nn.sigmoid(jnp.concatenate(parts, axis=1) + b_ref[...])

    r = gate(wa_ref, ba_ref)
    yield
    i_g = gate(wx_ref, bx_ref)
    yield
    log_a = (-LRU_C) * r * _softplus(-lam_ref[...])
    a = jnp.exp(log_a)
    one_minus_a2 = -jnp.tanh(log_a) * (a * a + 1.0)
    b = jnp.sqrt(one_minus_a2) * (i_g * xc2)
    yield

    a3 = a.reshape(g_n, SUBLANES, c)
    b3 = b.reshape(g_n, SUBLANES, c)
    row = lax.broadcasted_iota(jnp.int32, a3.shape, 1)
    s = 1
    while s < SUBLANES:
        m = row >= s
        a_sh = pltpu.roll(a3, s, 1)
        b_sh = pltpu.roll(b3, s, 1)
        b3 = jnp.where(m, a3 * b_sh + b3, b3)
        a3 = jnp.where(m, a3 * a_sh, a3)
        s *= 2
        yield

    h_in = ch_scr[...] if long_seq else None
    h_groups = []
    for g in range(g_n):
        if not long_seq:
            h_in = h0_ref[g]
        hg = a3[g] * h_in + b3[g]
        h_groups.append(hg)
        h_in = hg[SUBLANES - 1:SUBLANES]
        if not long_seq:
            hl_ref[g] = h_in
    if long_seq:
        ch_scr[...] = h_in
        hl_ref[0] = h_in
    yield
    h = jnp.stack(h_groups, axis=0).reshape(rows, c)
    gl = gate_ref[...].reshape(rows, c)
    o_ref[...] = (h * jax.nn.gelu(gl)).astype(o_ref.dtype)


def _lru_call(proj3, col0, prev, h0, weights, l, *, long_seq, n_outer, n_inner, g_n, group0, state_layer):
    cw, cb, wa, ba, wx, bx, lam = weights
    c = cw.shape[-1]
    m = proj3.shape[0] * SUBLANES
    nseq_blk = 1 if long_seq else g_n
    nseq = n_outer * nseq_blk
    gb0 = group0 // g_n

    def x_spec(col):
        return pl.BlockSpec((g_n, SUBLANES, c), lambda i, t: (gb0 + i * n_inner + t, 0, col))

    if state_layer is None:
        prev_spec = pl.BlockSpec((nseq_blk, SUBLANES, c), lambda i, t: (i, 0, 0))
        h0_spec = pl.BlockSpec((nseq_blk, 1, c), lambda i, t: (i, 0, 0))
    else:
        prev_spec = pl.BlockSpec((None, prev.shape[1], nseq_blk, c), lambda i, t: (state_layer, 0, i, 0))
        h0_spec = pl.BlockSpec((None, nseq_blk, 1, c), lambda i, t: (state_layer, i, 0, 0))

    ng = c // GATE_TILE
    in_specs = [x_spec(col0), x_spec(col0 + 1), prev_spec, h0_spec,
                _layer(l, (CONV_WIDTH, c)), _layer(l, (1, c)),
                _layer(l, (ng, GATE_TILE, GATE_TILE)), _layer(l, (1, c)),
                _layer(l, (ng, GATE_TILE, GATE_TILE)), _layer(l, (1, c)), _layer(l, (1, c))]
    args = [proj3, proj3, prev, h0, cw, cb, wa, ba, wx, bx, lam]
    out_specs = [pl.BlockSpec((g_n * SUBLANES, c), lambda i, t: (i * n_inner + t, 0)),
                 pl.BlockSpec((nseq_blk, 1, c), lambda i, t: (i, 0, 0)),
                 pl.BlockSpec((nseq_blk, SUBLANES, c), lambda i, t: (i, 0, 0))]
    out_shape = [jax.ShapeDtypeStruct((n_outer * n_inner * g_n * SUBLANES, c), BF16),
                 jax.ShapeDtypeStruct((nseq, 1, c), F32),
                 jax.ShapeDtypeStruct((nseq, SUBLANES, c), F32)]
    return pl.pallas_call(
        functools.partial(_lru_body, long_seq=long_seq),
        grid=(n_outer, n_inner),
        in_specs=in_specs, out_specs=out_specs, out_shape=out_shape,
        scratch_shapes=[pltpu.VMEM((SUBLANES, c), F32), pltpu.VMEM((1, c), F32)],
        compiler_params=_params("parallel", "arbitrary"),
        name="lru_long" if long_seq else "lru_short",
    )(*args)


def _pool_body(*refs, long_seq, past_len):
    u_ref, hist_ref, pw_ref, ps_ref, o_ref, nh_ref, c_scr = refs
    g_n, _, c = u_ref.shape
    rows = g_n * SUBLANES
    x = u_ref[...]
    t = pl.program_id(1)
    if long_seq:
        @pl.when(t == 0)
        def _():
            c_scr[0] = hist_ref[0, 0:SUBLANES]
            c_scr[1] = hist_ref[0, SUBLANES:2 * SUBLANES]

        ext = jnp.concatenate([c_scr[...], x], axis=0)
        gs = 1
    else:
        hist = _rows_before(hist_ref, 2 * SUBLANES)
        ext = jnp.concatenate([hist[:, 0:SUBLANES], hist[:, SUBLANES:2 * SUBLANES], x], axis=0)
        gs = g_n
    nh_ref[:, 0:SUBLANES] = ext[-2 * gs:-gs]
    nh_ref[:, SUBLANES:2 * SUBLANES] = ext[-gs:]
    if long_seq:
        c_scr[0] = ext[g_n]
        c_scr[1] = ext[g_n + 1]

    def prev(a):
        return jnp.concatenate([a[:gs], a[:-gs]], axis=0)

    n_grp = len(POOL_WINDOWS)
    pg = c // n_grp
    i0 = lax.broadcasted_iota(jnp.int32, (g_n, SUBLANES, pg), 0)
    i1 = lax.broadcasted_iota(jnp.int32, (g_n, SUBLANES, pg), 1)
    t_abs = (t * g_n + i0) * SUBLANES + i1 if long_seq else i1
    outs = []
    for gi, w in enumerate(POOL_WINDOWS):
        eg = ext[:, :, gi * pg:(gi + 1) * pg]
        s = eg
        k = 1
        while k < w:
            if k < SUBLANES:
                rolled = pltpu.roll(s, k, 1)
                row = lax.broadcasted_iota(jnp.int32, s.shape, 1)
                s = s + jnp.where(row >= k, rolled, prev(rolled))
            else:
                s = s + prev(s)
            k *= 2
        cnt = jnp.minimum(t_abs + (1 + past_len), w).astype(F32)
        d = s[2 * gs:] / cnt - eg[2 * gs:]
        d2 = d.reshape(rows, pg).astype(BF16)
        outs.append(jnp.dot(d2, pw_ref[gi], preferred_element_type=F32))
    y = jnp.concatenate(outs, axis=1) * ps_ref[...]
    o_ref[...] = y.astype(o_ref.dtype)


def _pool_call(proj3, col, hist, pool_w, pool_scale, l, *, long_seq, past_len, n_outer, n_inner, g_n,
               group0, state_layer):
    c = pool_scale.shape[-1]
    m = proj3.shape[0] * SUBLANES
    nseq_blk = 1 if long_seq else g_n
    nseq = n_outer * nseq_blk
    gb0 = group0 // g_n
    pg = c // len(POOL_WINDOWS)
    hrows = 2 * SUBLANES
    if state_layer is None:
        h_spec = pl.BlockSpec((nseq_blk, hrows, c), lambda i, t: (i, 0, 0))
    else:
        h_spec = pl.BlockSpec((None, hist.shape[1], nseq_blk, c), lambda i, t: (state_layer, 0, i, 0))
    in_specs = [pl.BlockSpec((g_n, SUBLANES, c), lambda i, t: (gb0 + i * n_inner + t, 0, col)), h_spec,
                _layer(l, (len(POOL_WINDOWS), pg, pg)), _layer(l, (1, c))]
    args = [proj3, hist, pool_w, pool_scale]
    return pl.pallas_call(
        functools.partial(_pool_body, long_seq=long_seq, past_len=past_len),
        grid=(n_outer, n_inner),
        in_specs=in_specs,
        out_specs=[pl.BlockSpec((g_n * SUBLANES, c), lambda i, t: (i * n_inner + t, 0)),
                   pl.BlockSpec((nseq_blk, hrows, c), lambda i, t: (i, 0, 0))],
        out_shape=[jax.ShapeDtypeStruct((n_outer * n_inner * g_n * SUBLANES, c), BF16),
                   jax.ShapeDtypeStruct((nseq, hrows, c), F32)],
        scratch_shapes=[pltpu.VMEM((2, SUBLANES, c), F32)],
        compiler_params=_params("parallel", "arbitrary"),
        name="pool_long" if long_seq else "pool_short",
    )(*args)


CHUNK_STAGE_LAG = 3
SOLVE_PASSES = 1


def _mm(a, b):
    a_hi, b_hi = a.astype(BF16), b.astype(BF16)
    out = jnp.dot(a_hi, b_hi, preferred_element_type=F32)
    if SOLVE_PASSES == 3:
        a_lo = (a - a_hi.astype(F32)).astype(BF16)
        b_lo = (b - b_hi.astype(F32)).astype(BF16)
        out = out + jnp.dot(a_hi, b_lo, preferred_element_type=F32) + jnp.dot(a_lo, b_hi, preferred_element_type=F32)
    return out


def _delta_body(*refs, long_seq, tseq, n_aliased):
    (q_ref, k_ref, v_ref, z_ref, ba_ref, prev_ref, s0_ref, cw_ref, alog_ref, dtb_ref, nw_ref) = refs[:11]
    o_ref, s_ref, nb_ref, cx_scr = refs[11 + n_aliased:]
    g_n, _, c = q_ref.shape
    rows = g_n * SUBLANES
    n_sub = rows // DN_CHUNK
    nseq_chunk = DN_CHUNK // tseq
    heads = c // HEAD_DIM

    if long_seq:
        @pl.when(pl.program_id(1) == 0)
        def _():
            cx_scr[...] = prev_ref[0]
            s_ref[...] = s0_ref[...]

    cw = cw_ref[...]
    streams = []
    for idx, x_ref in enumerate((q_ref, k_ref, v_ref)):
        x = x_ref[...]
        lo, hi = idx * c, (idx + 1) * c
        if long_seq:
            prev = cx_scr[:, lo:hi]
            cx_scr[:, lo:hi] = x[g_n - 1]
            nb_ref[0, :, lo:hi] = x[g_n - 1]
        else:
            prev = _rows_before(prev_ref.at[:, :, lo:hi], SUBLANES)
            nb_ref[:, :, lo:hi] = x
        y = _causal_conv(x, prev, cw[:, lo:hi], long_seq)
        streams.append((y * jax.nn.sigmoid(y)).reshape(rows, c))
    q_all, k_all, v_all = streams
    z_all = z_ref[...].reshape(rows, c)

    ba = ba_ref[...]
    beta_all = jax.nn.sigmoid(ba[:, 0:HEAD_DIM])
    g_all = -jnp.exp(alog_ref[...]) * _softplus(ba[:, HEAD_DIM:2 * HEAD_DIM] + dtb_ref[...])

    ri = lax.broadcasted_iota(jnp.int32, (DN_CHUNK, DN_CHUNK), 0)
    ci = lax.broadcasted_iota(jnp.int32, (DN_CHUNK, DN_CHUNK), 1)
    same = (ri // tseq) == (ci // tseq)
    incl = same & (ci <= ri)
    strict = same & (ci < ri)
    eye = (ri == ci).astype(F32)
    incl_f, same_f = incl.astype(F32), same.astype(F32)

    state = [s_ref[0, h] for h in range(heads)] if long_seq else None
    masks = (ri, ci, incl, strict, eye, incl_f, same_f)
    waiting = [_delta_chunk([(sub, h) for h in range(heads)], state, masks,
                            (q_all, k_all, v_all, z_all, beta_all, g_all), (s0_ref, s_ref, nw_ref, o_ref),
                            long_seq, tseq) for sub in range(n_sub)]
    running, rnd = [], 0
    while waiting or running:
        if waiting and rnd % CHUNK_STAGE_LAG == 0:
            running.append(waiting.pop(0))
        for gen in list(running):
            if next(gen, "done") == "done":
                running.remove(gen)
        rnd += 1
    if long_seq:
        for h in range(heads):
            s_ref[0, h] = state[h]


def _delta_chunk(pairs, state, masks, streams, refs, long_seq, tseq):
    ri, ci, incl, strict, eye, incl_f, same_f = masks
    q_all, k_all, v_all, z_all, beta_all, g_all = streams
    s0_ref, s_ref, nw_ref, o_ref = refs
    nseq_chunk = DN_CHUNK // tseq
    gamma_s, gtot_s, gamma_ts = {}, {}, {}
    for sub in sorted({sub for sub, _ in pairs}):
        g_sub = g_all[sub * DN_CHUNK:(sub + 1) * DN_CHUNK]
        gamma = jnp.dot(incl_f, g_sub, precision=_HI, preferred_element_type=F32)
        gamma_s[sub] = gamma
        gtot_s[sub] = jnp.dot(same_f, g_sub, precision=_HI, preferred_element_type=F32)
        gamma_ts[sub] = gamma.T

    q_l, k_l, v_l, beta_l, gcol_l, gtot_l, decay_l, qb_l, kb_l, a_l = ([] for _ in range(10))
    for sub, h in pairs:
        rs = slice(sub * DN_CHUNK, (sub + 1) * DN_CHUNK)
        hs = slice(h * HEAD_DIM, (h + 1) * HEAD_DIM)
        q_h, k_h = q_all[rs, hs], k_all[rs, hs]
        q_h = q_h * lax.rsqrt(jnp.sum(q_h * q_h, axis=-1, keepdims=True) + EPS) * (HEAD_DIM ** -0.5)
        k_h = k_h * lax.rsqrt(jnp.sum(k_h * k_h, axis=-1, keepdims=True) + EPS)
        gcol = gamma_s[sub][:, h:h + 1]
        decay = jnp.where(incl, jnp.exp(gcol - gamma_ts[sub][h:h + 1, :]), 0.0)
        qb, kb = q_h.astype(BF16), k_h.astype(BF16)
        beta = beta_all[rs, h:h + 1]
        kk = lax.dot_general(kb, kb, _NT, preferred_element_type=F32)
        q_l.append(q_h)
        k_l.append(k_h)
        v_l.append(v_all[rs, hs])
        beta_l.append(beta)
        gcol_l.append(gcol)
        gtot_l.append(gtot_s[sub][:, h:h + 1])
        decay_l.append(decay)
        qb_l.append(qb)
        kb_l.append(kb)
        a_l.append(jnp.where(strict, beta * decay * kk, 0.0))
    yield

    t_l = [eye - jnp.where(ri // 2 == ci // 2, a, 0.0) for a in a_l]
    s = 2
    while s < tseq:
        off_mask = (ri // (2 * s) == ci // (2 * s)) & (ri % (2 * s) >= s) & (ci % (2 * s) < s)
        t_off = [_mm(t, jnp.where(off_mask, a, 0.0)) for t, a in zip(t_l, a_l)]
        t_l = [t - _mm(to, t) for t, to in zip(t_l, t_off)]
        s *= 2
        yield

    eg_l = [jnp.exp(g) for g in gcol_l]
    sol_l = [_mm(t, jnp.concatenate([b * v, (b * e) * k], axis=1))
             for t, b, v, e, k in zip(t_l, beta_l, v_l, eg_l, k_l)]
    qk_l = [(lax.dot_general(qb, kb, _NT, preferred_element_type=F32) * d).astype(BF16)
            for qb, kb, d in zip(qb_l, kb_l, decay_l)]
    qg_l = [(q * e).astype(BF16) for q, e in zip(q_l, eg_l)]
    kd_l = [(k * jnp.exp(gt - g)).astype(BF16) for k, gt, g in zip(k_l, gtot_l, gcol_l)]
    yield

    w_l, o_l, s_old = [], [], []
    for i, (sub, h) in enumerate(pairs):
        u_c, wkb = sol_l[i][:, :HEAD_DIM], sol_l[i][:, HEAD_DIM:].astype(BF16)
        w_parts, o_parts, s_mats = [], [], []
        for sq in range(nseq_chunk):
            rq = slice(sq * tseq, (sq + 1) * tseq)
            s_mat = state[h] if long_seq else s0_ref[sub * nseq_chunk + sq, h]
            sb = s_mat.astype(BF16)
            s_mats.append(s_mat)
            w_parts.append(u_c[rq] - jnp.dot(wkb[rq], sb, preferred_element_type=F32))
            o_parts.append(jnp.dot(qg_l[i][rq], sb, preferred_element_type=F32))
        w_l.append((jnp.concatenate(w_parts, axis=0) if nseq_chunk > 1 else w_parts[0]).astype(BF16))
        o_l.append(jnp.concatenate(o_parts, axis=0) if nseq_chunk > 1 else o_parts[0])
        s_old.append(s_mats)
    yield

    for i, (sub, h) in enumerate(pairs):
        w = w_l[i]
        for sq in range(nseq_chunk):
            rq = slice(sq * tseq, (sq + 1) * tseq)
            g_last = jnp.exp(gtot_l[i][sq * tseq:sq * tseq + 1])
            s_upd = g_last * s_old[i][sq] + lax.dot_general(kd_l[i][rq], w[rq], _TN, preferred_element_type=F32)
            if long_seq:
                state[h] = s_upd
            else:
                s_ref[sub * nseq_chunk + sq, h] = s_upd
        o = o_l[i] + jnp.dot(qk_l[i], w, preferred_element_type=F32)
        o = o * lax.rsqrt(jnp.mean(o * o, axis=-1, keepdims=True) + EPS) * nw_ref[...]
        rs = slice(sub * DN_CHUNK, (sub + 1) * DN_CHUNK)
        hs = slice(h * HEAD_DIM, (h + 1) * HEAD_DIM)
        z_h = z_all[rs, hs]
        o_ref[rs, hs] = (o * (z_h * jax.nn.sigmoid(z_h))).astype(o_ref.dtype)


def _delta_call(proj3, col0, ba, prev, s0, conv_w, alog, dtb, norm_w, l, *, long_seq, tseq, n_outer, n_inner,
                group0, state_layer, n_sub, s_stack=None):
    c = conv_w.shape[-1] // 3
    heads = c // HEAD_DIM
    blk_rows = n_sub * DN_CHUNK
    g_n = blk_rows // SUBLANES
    m = proj3.shape[0] * SUBLANES
    nseq_blk = 1 if long_seq else blk_rows // tseq
    nseq = n_outer * nseq_blk
    gb0 = group0 // g_n

    def x_spec(col):
        return pl.BlockSpec((g_n, SUBLANES, c), lambda i, t: (gb0 + i * n_inner + t, 0, col))

    if state_layer is None:
        p_spec = pl.BlockSpec((nseq_blk, SUBLANES, 3 * c), lambda i, t: (i, 0, 0))
        s_spec = pl.BlockSpec((nseq_blk, heads, HEAD_DIM, HEAD_DIM), lambda i, t: (i, 0, 0, 0))
    else:
        p_spec = pl.BlockSpec((None, prev.shape[1], nseq_blk, 3 * c), lambda i, t: (state_layer, 0, i, 0))
        s_spec = pl.BlockSpec((None, nseq_blk, heads, HEAD_DIM, HEAD_DIM), lambda i, t: (state_layer, i, 0, 0, 0))
    in_specs = [x_spec(col0), x_spec(col0 + 1), x_spec(col0 + 2), x_spec(col0 + 3),
                pl.BlockSpec((blk_rows, 2 * HEAD_DIM), lambda i, t: (gb0 + i * n_inner + t, 0)),
                p_spec, s_spec,
                _layer(l, (CONV_WIDTH, 3 * c)), _layer(l, (1, HEAD_DIM)), _layer(l, (1, HEAD_DIM)),
                _layer(l, (1, HEAD_DIM))]
    args = [proj3, proj3, proj3, proj3, ba, prev, s0, conv_w, alog, dtb, norm_w]
    aliases = {}
    s_block = (nseq_blk, heads, HEAD_DIM, HEAD_DIM)
    if s_stack is None:
        s_out_spec = pl.BlockSpec(s_block, lambda i, t: (i, 0, 0, 0))
        s_out_shape = jax.ShapeDtypeStruct((nseq,) + s_block[1:], F32)
    else:
        depth, stacked = s_stack
        s_out_spec = pl.BlockSpec((None,) + s_block, lambda i, t: (l, i, 0, 0, 0))
        s_out_shape = jax.ShapeDtypeStruct((depth, nseq) + s_block[1:], F32)
        if stacked is not None:
            in_specs.append(pl.BlockSpec(memory_space=pl.ANY))
            args.append(stacked)
            aliases[len(args) - 1] = 1
    return pl.pallas_call(
        functools.partial(_delta_body, long_seq=long_seq, tseq=tseq, n_aliased=len(aliases)),
        grid=(n_outer, n_inner),
        in_specs=in_specs,
        out_specs=[pl.BlockSpec((blk_rows, c), lambda i, t: (i * n_inner + t, 0)),
                   s_out_spec,
                   pl.BlockSpec((nseq_blk, SUBLANES, 3 * c), lambda i, t: (i, 0, 0))],
        out_shape=[jax.ShapeDtypeStruct((n_outer * n_inner * blk_rows, c), BF16),
                   s_out_shape,
                   jax.ShapeDtypeStruct((nseq, SUBLANES, 3 * c), F32)],
        scratch_shapes=[pltpu.VMEM((SUBLANES, 3 * c), F32)],
        input_output_aliases=aliases,
        compiler_params=_params("parallel", "arbitrary"),
        name="delta_long" if long_seq else "delta_short",
    )(*args)


def _cast_rows(w, steps):
    rows = w.shape[1]
    nblk = 1
    while nblk * 2 <= steps and rows % (nblk * 2) == 0 and rows // (nblk * 2) >= 2 * SUBLANES:
        nblk *= 2
    return rows // nblk


def _block_diag_tiles(w):
    depth, nb, b, _ = w.shape
    per = GATE_TILE // b
    w5 = w.reshape(depth, nb // per, per, b, b)
    eye = jnp.eye(per, dtype=w.dtype)
    t = jnp.einsum('lijcd,jk->lijckd', w5, eye)
    return t.reshape(depth, nb // per, GATE_TILE, GATE_TILE).astype(BF16)


def kernel(x_prompt, x_sample, state_lru_h, state_lru_conv, state_dn_s, state_dn_conv, state_pool, norm_mix, w_in, lru_conv_w, lru_conv_b, lru_gate_a_w, lru_gate_a_b, lru_gate_x_w, lru_gate_x_b, lru_lambda, dn_conv_w, dn_a_log, dn_dt_bias, dn_norm_w, pool_w, pool_scale, w_br_lru, w_br_dn, w_br_pool, w_out, norm_mlp, w_up, w_down, norm_final):
    bp, tp, d = x_prompt.shape
    bs, ts, _ = x_sample.shape
    depth = w_in.shape[0]
    half = d // 2
    heads = half // HEAD_DIM
    mp, ms = bp * tp, bs * ts
    m = mp + ms
    assert ts == SUBLANES and tp % DN_CHUNK == 0 and bs % (DN_CHUNK // ts) == 0
    assert lru_gate_a_w.shape[-1] == LRU_BLOCK and half % GATE_TILE == 0
    assert state_pool.shape[-2] == POOL_BUF

    n_pre = 6 * half
    ba0 = n_pre
    pool0 = ba0 + 2 * heads
    gates0 = pool0 + half
    w_in_t = jnp.swapaxes(w_in, 1, 2)
    n_gate_tiles, n_stream_tiles = 3 * d // half, n_pre // half + 1
    assert gates0 % SUBLANES == 0 and pool0 % SUBLANES == 0

    def gate_row0(j):
        return pl.multiple_of(gates0 + j * half, SUBLANES)

    def stream_row0(j):
        return pl.multiple_of(jnp.where(j < n_stream_tiles - 1, j * half, pool0), SUBLANES)

    row_fill = ((0, 0), (0, HEAD_DIM - heads), (0, 0))
    w_ba = jnp.concatenate([jnp.pad(w_in_t[:, ba0:ba0 + heads], row_fill),
                            jnp.pad(w_in_t[:, ba0 + heads:ba0 + 2 * heads], row_fill)], axis=1)
    col0 = 0

    def row(a):
        return a.reshape(depth, 1, a.shape[-1])

    def lane_pad(a):
        return jnp.pad(a, ((0, 0), (0, HEAD_DIM - a.shape[-1]))).reshape(depth, 1, HEAD_DIM)

    lru_w = (lru_conv_w, row(lru_conv_b), _block_diag_tiles(lru_gate_a_w), row(lru_gate_a_b),
             _block_diag_tiles(lru_gate_x_w), row(lru_gate_x_b), row(lru_lambda))
    alog, dtb, dn_nw = lane_pad(dn_a_log), lane_pad(dn_dt_bias), row(dn_norm_w)
    pool_wb, pool_sc = pool_w.astype(BF16), row(pool_scale)
    g_mix, g_mlp = row(norm_mix), row(norm_mlp)

    s_lru_h0 = state_lru_h.reshape(depth, bs, 1, half)
    s_lru_prev = jnp.swapaxes(state_lru_conv, 1, 2)
    s_dn_prev = jnp.swapaxes(state_dn_conv, 1, 2)
    s_pool_hist = jnp.swapaxes(state_pool, 1, 2)
    p_lru_h0 = jnp.zeros((bp, 1, half), F32)
    p_lru_prev = jnp.zeros((bp, SUBLANES, half), F32)
    p_dn_prev = jnp.zeros((bp, SUBLANES, 3 * half), F32)
    p_dn_s0 = jnp.zeros((bp, heads, HEAD_DIM, HEAD_DIM), F32)
    p_pool_hist = jnp.zeros((bp, 2 * SUBLANES, half), F32)

    tm = 512 if (mp % 512 == 0 and ms % 512 == 0) else 256
    tm_big = next(t for t in (1536, 1024, tm) if m % t == 0)
    lru_g = 64
    assert (tp // SUBLANES) % lru_g == 0 and bs % lru_g == 0 and m % tm == 0 and mp % tm == 0
    n_t = tp // SUBLANES // lru_g
    long_kw = dict(long_seq=True, n_outer=bp, n_inner=n_t, g_n=lru_g, group0=0, state_layer=None)
    short_kw = dict(long_seq=False, n_outer=bs // lru_g, n_inner=1, g_n=lru_g, group0=mp // SUBLANES)
    seq_per_blk = DN_CHUNK // ts
    dn_sub, dn_sub_s = 8, 2
    assert tp % (dn_sub * DN_CHUNK) == 0 and bs % (dn_sub_s * seq_per_blk) == 0

    p_states, s_states = [], []
    s_dn_stack = None
    norm_fin = norm_final.reshape(1, 1, d)
    x, xn, ba = _norm_cast(x_prompt.reshape(mp, d), x_sample.reshape(ms, d), g_mix, w_ba, 0, tm=tm)
    for l in range(depth):
        w_rows = (None, pl.Element(half), pl.Element(d))
        m_tiles = m // tm_big
        proj, w_br_dn_b, w_br_pool_b = _stream_matmul(
            xn, w_in_t, pl.BlockSpec(w_rows, lambda j, i, l=l: (l, stream_row0(j), 0)), (half, d), n_stream_tiles,
            tm=tm_big, tn=half, out_dtype=F32, w_rows_are_outputs=True,
            casts=[(w, l, _cast_rows(w, n_stream_tiles * m_tiles)) for w in (w_br_dn, w_br_pool)])
        proj3 = proj.reshape(m // SUBLANES, SUBLANES, proj.shape[-1])

        gates, w_out_b, w_br_lru_b = _stream_matmul(
            xn, w_in_t, pl.BlockSpec(w_rows, lambda j, i, l=l: (l, gate_row0(j), 0)), (half, d), n_gate_tiles,
            tm=tm_big, tn=half, out_dtype=BF16, w_rows_are_outputs=True, act="sigmoid",
            casts=[(w, l, _cast_rows(w, n_gate_tiles * m_tiles)) for w in (w_out, w_br_lru)])
        o_lru_p, p_h, p_cb = _lru_call(proj3, col0, p_lru_prev, p_lru_h0, lru_w, l, **long_kw)
        o_lru_s, s_h, s_cb = _lru_call(proj3, col0, s_lru_prev, s_lru_h0, lru_w, l, state_layer=l, **short_kw)

        o_dn_p, p_s, p_db = _delta_call(proj3, col0 + 2, ba, p_dn_prev, p_dn_s0, dn_conv_w, alog, dtb, dn_nw, l,
                                        long_seq=True, tseq=DN_CHUNK, n_outer=bp, n_inner=tp // (dn_sub * DN_CHUNK),
                                        group0=0, state_layer=None, n_sub=dn_sub)
        o_dn_s, s_dn_stack, s_db = _delta_call(proj3, col0 + 2, ba, s_dn_prev, state_dn_s, dn_conv_w, alog, dtb,
                                               dn_nw, l, long_seq=False, tseq=ts,
                                               n_outer=bs // (dn_sub_s * seq_per_blk), n_inner=1,
                                               group0=mp // SUBLANES, state_layer=l, n_sub=dn_sub_s,
                                               s_stack=(depth, s_dn_stack))

        o_pool_p, p_ph = _pool_call(proj3, col0 + 6, p_pool_hist, pool_wb, pool_sc, l, past_len=0, **long_kw)
        o_pool_s, s_ph = _pool_call(proj3, col0 + 6, s_pool_hist, pool_wb, pool_sc, l, past_len=PAST_LEN,
                                    state_layer=l, **short_kw)

        merged = _merge((o_lru_p, o_lru_s), (o_dn_p, o_dn_s), (o_pool_p, o_pool_s), gates,
                        w_br_lru_b, w_br_dn_b, w_br_pool_b, 0, tm=tm, tn=d)
        x, xn = _matmul_res(merged, w_out_b, 0, x, tm=tm, tk=d, gain=g_mlp, gain_layer=l)
        w_up_spec = pl.BlockSpec((None, d, half), lambda j, i, l=l: (l, 0, j))
        n_up_tiles = w_up.shape[-1] // half
        hm, w_down_b = _stream_matmul(xn, w_up, w_up_spec, (d, half), n_up_tiles, tm=tm_big, tn=half,
                                      out_dtype=BF16, w_rows_are_outputs=False, act="relu2",
                                      casts=[(w_down, l, _cast_rows(w_down, n_up_tiles * m_tiles))])
        if l + 1 < depth:
            x, xn, ba = _matmul_res(hm, w_down_b, 0, x, tm=tm, tk=d, gain=g_mix, gain_layer=l + 1, w_small=w_ba)
        else:
            _, y_prompt, y_sample = _matmul_res(hm, w_down_b, 0, x, tm=tm, tk=half, gain=norm_fin, gain_layer=0,
                                                split_rows=mp)
        p_states.append((p_h, p_cb, p_s, p_db, p_ph))
        s_states.append((s_h, s_cb, None, s_db, s_ph))

    y_prompt = y_prompt.reshape(bp, tp, d)
    y_sample = y_sample.reshape(bs, ts, d)

    def collect(states, nseq, sm=None):
        tail = CONV_WIDTH - 1
        h = jnp.stack([s[0] for s in states]).reshape(depth, nseq, half)
        cb = jnp.stack([s[1] for s in states])[:, :, SUBLANES - tail:]
        if sm is None:
            sm = jnp.stack([s[2] for s in states])
        db = jnp.stack([s[3] for s in states])[:, :, SUBLANES - tail:]
        ph = jnp.stack([s[4] for s in states])[:, :, 2 * SUBLANES - POOL_BUF:]
        return h, cb, sm, db, ph

    return (y_prompt, y_sample) + collect(p_states, bp) + collect(s_states, bs, s_dn_stack)
```

```python
import functools

import jax
import jax.numpy as jnp
from jax import lax
from jax.experimental import pallas as pl
from jax.experimental.pallas import tpu as pltpu

F32 = jnp.float32
BF16 = jnp.bfloat16

EPS = 1e-6
LRU_C = 8.0
CONV_WIDTH = 4
HEAD_DIM = 128
LRU_BLOCK = 64
POOL_WINDOWS = (2, 4, 8, 16)
POOL_BUF = 15
DN_CHUNK = 64
PAST_LEN = 16384
SUBLANES = 8
GATE_TILE = 256
VMEM_LIMIT = 56 * 1024 * 1024

_NT = (((1,), (1,)), ((), ()))
_TN = (((0,), (0,)), ((), ()))
_HI = lax.Precision.HIGHEST


def _params(*sem):
    return pltpu.CompilerParams(dimension_semantics=sem, vmem_limit_bytes=VMEM_LIMIT)


def _full(shape):
    return pl.BlockSpec(shape, lambda *_: (0,) * len(shape))


def _layer(l, shape):
    return pl.BlockSpec((None,) + shape, lambda *_: (l,) + (0,) * len(shape))


def _rms(x, gain):
    return x * lax.rsqrt(jnp.mean(x * x, axis=-1, keepdims=True) + EPS) * gain


def _two_source_specs(tm, c, n_first):
    first = pl.BlockSpec((tm, c), lambda i, *_: (jnp.minimum(i, n_first - 1), 0))
    second = pl.BlockSpec((tm, c), lambda i, *_: (jnp.maximum(i - n_first, 0), 0))
    return first, second


def _pick_rows(first_ref, second_ref, n_first):
    return jnp.where(pl.program_id(0) < n_first, first_ref[...], second_ref[...])


def _norm_cast_body(xa_ref, xb_ref, g_ref, ws_ref, xn_ref, os_ref, *, n_first):
    x = _pick_rows(xa_ref, xb_ref, n_first)
    xn = _rms(x, g_ref[...]).astype(BF16)
    xn_ref[...] = xn
    os_ref[...] = lax.dot_general(xn, ws_ref[...].astype(BF16), _NT, preferred_element_type=F32)


def _norm_cast(xa, xb, gain, w_small, l, *, tm):
    d = xa.shape[-1]
    m = xa.shape[0] + xb.shape[0]
    n_first = xa.shape[0] // tm
    ns = w_small.shape[-2]
    return pl.pallas_call(
        functools.partial(_norm_cast_body, n_first=n_first),
        grid=(m // tm,),
        in_specs=[*_two_source_specs(tm, d, n_first), _layer(l, (1, d)), _layer(l, (ns, d))],
        out_specs=[pl.BlockSpec((tm, d), lambda i: (i, 0)), pl.BlockSpec((tm, ns), lambda i: (i, 0))],
        out_shape=[jax.ShapeDtypeStruct((m, d), BF16), jax.ShapeDtypeStruct((m, ns), F32)],
        compiler_params=_params("arbitrary"),
        name="norm_cast",
    )(xa, xb, gain, w_small)


def _stream_matmul_body(*refs, w_rows_are_outputs, act, cast_blocks):
    n_cast = len(cast_blocks)
    x_ref, w_ref = refs[:2]
    src_refs = refs[2:2 + n_cast]
    o_ref = refs[2 + n_cast]
    dst_refs = refs[3 + n_cast:3 + 2 * n_cast]
    wb_ref = refs[3 + 2 * n_cast]

    @pl.when(pl.program_id(1) == 0)
    def _():
        wb_ref[...] = w_ref[...].astype(BF16)

    step = pl.program_id(0) * pl.num_programs(1) + pl.program_id(1)
    for src, dst, nblk in zip(src_refs, dst_refs, cast_blocks):
        @pl.when(step < nblk)
        def _(src=src, dst=dst):
            dst[...] = src[...].astype(BF16)

    dims = _NT if w_rows_are_outputs else (((1,), (0,)), ((), ()))
    acc = lax.dot_general(x_ref[...], wb_ref[...], dims, preferred_element_type=F32)
    if act == "relu2":
        acc = jnp.square(jnp.maximum(acc, 0.0))
    elif act == "sigmoid":
        acc = jax.nn.sigmoid(acc)
    o_ref[...] = acc.astype(o_ref.dtype)


def _stream_matmul(x, w, w_spec, w_block, n_tiles, *, tm, tn, out_dtype, w_rows_are_outputs, act=None, casts=()):
    m, d = x.shape
    m_tiles = m // tm
    in_specs = [pl.BlockSpec((tm, d), lambda j, i: (i, 0)), w_spec]
    out_specs = [pl.BlockSpec((tm, tn), lambda j, i: (i, j))]
    out_shape = [jax.ShapeDtypeStruct((m, n_tiles * tn), out_dtype)]
    cast_blocks = []
    for src, layer, rps in casts:
        _, rows, cols = src.shape
        nblk = rows // rps
        assert rows % rps == 0 and nblk <= n_tiles * m_tiles
        cast_blocks.append(nblk)
        in_specs.append(pl.BlockSpec((None, rps, cols),
                                     lambda j, i, layer=layer, nblk=nblk: (layer, jnp.minimum(j * m_tiles + i, nblk - 1), 0)))
        out_specs.append(pl.BlockSpec((None, rps, cols),
                                      lambda j, i, nblk=nblk: (0, jnp.minimum(j * m_tiles + i, nblk - 1), 0)))
        out_shape.append(jax.ShapeDtypeStruct((1, rows, cols), BF16))
    res = pl.pallas_call(
        functools.partial(_stream_matmul_body, w_rows_are_outputs=w_rows_are_outputs, act=act,
                          cast_blocks=tuple(cast_blocks)),
        grid=(n_tiles, m_tiles),
        in_specs=in_specs, out_specs=out_specs, out_shape=out_shape,
        scratch_shapes=[pltpu.VMEM(w_block, BF16)],
        compiler_params=_params("arbitrary", "arbitrary"),
        name="stream_matmul_t" if w_rows_are_outputs else "stream_matmul",
    )(x, w, *[c[0] for c in casts])
    return res if casts else res[0]


def _matmul_res_body(*refs, n_k, emit_norm, emit_small, n_first, res_first):
    a_ref, w_ref = refs[:2]
    if res_first is None:
        r_ref = refs[2]
        refs = refs[3:]

        def residual():
            return r_ref[...]
    else:
        ra_ref, rb_ref = refs[2:4]
        refs = refs[4:]

        def residual():
            return _pick_rows(ra_ref, rb_ref, res_first)
    if emit_norm:
        g_ref = refs[0]
        refs = refs[1:]
    if emit_small:
        ws_ref = refs[0]
        refs = refs[1:]
    o_ref = refs[0]
    k = pl.program_id(1)

    def emit(acc):
        if n_first is not None:
            y = _rms(acc, g_ref[...])
            i = pl.program_id(0)

            @pl.when(i < n_first)
            def _():
                refs[1][...] = y

            @pl.when(i >= n_first)
            def _():
                refs[2][...] = y
            return
        xn = _rms(acc, g_ref[...]).astype(BF16)
        refs[1][...] = xn
        if emit_small:
            refs[2][...] = lax.dot_general(xn, ws_ref[...].astype(BF16), _NT, preferred_element_type=F32)

    if n_k == 1:
        acc = residual() + jnp.dot(a_ref[...], w_ref[...], preferred_element_type=F32)
        o_ref[...] = acc
        if emit_norm:
            emit(acc)
        return

    @pl.when(k == 0)
    def _():
        o_ref[...] = residual()

    o_ref[...] += jnp.dot(a_ref[...], w_ref[...], preferred_element_type=F32)

    if emit_norm:
        pl.when(k == n_k - 1)(lambda: emit(o_ref[...]))


def _matmul_res(a, w, l, res, *, tm, tk, gain=None, gain_layer=0, w_small=None, split_rows=None):
    m, k = a.shape
    d = w.shape[-1]
    n_k = k // tk
    emit_norm = gain is not None
    emit_small = w_small is not None
    n_first = None if split_rows is None else split_rows // tm
    in_specs = [pl.BlockSpec((tm, tk), lambda i, kk: (i, kk)),
                pl.BlockSpec((None, tk, d), lambda i, kk: (l, kk, 0))]
    if isinstance(res, tuple):
        res_first = res[0].shape[0] // tm
        in_specs.extend(_two_source_specs(tm, d, res_first))
        args = [a, w, *res]
    else:
        res_first = None
        in_specs.append(pl.BlockSpec((tm, d), lambda i, kk: (i, 0)))
        args = [a, w, res]
    out_specs = [pl.BlockSpec((tm, d), lambda i, kk: (i, 0))]
    out_shape = [jax.ShapeDtypeStruct((m, d), F32)]
    if emit_norm:
        in_specs.append(_layer(gain_layer, (1, d)))
        args.append(gain)
        if n_first is None:
            out_specs.append(pl.BlockSpec((tm, d), lambda i, kk: (i, 0)))
            out_shape.append(jax.ShapeDtypeStruct((m, d), BF16))
        else:
            assert split_rows % tm == 0 and not emit_small
            out_specs.extend(_two_source_specs(tm, d, n_first))
            out_shape.extend([jax.ShapeDtypeStruct((split_rows, d), F32),
                              jax.ShapeDtypeStruct((m - split_rows, d), F32)])
    if emit_small:
        ns = w_small.shape[-2]
        in_specs.append(_layer(gain_layer, (ns, d)))
        args.append(w_small)
        out_specs.append(pl.BlockSpec((tm, ns), lambda i, kk: (i, 0)))
        out_shape.append(jax.ShapeDtypeStruct((m, ns), F32))
    return pl.pallas_call(
        functools.partial(_matmul_res_body, n_k=n_k, emit_norm=emit_norm, emit_small=emit_small, n_first=n_first,
                          res_first=res_first),
        grid=(m // tm, n_k),
        in_specs=in_specs, out_specs=out_specs, out_shape=out_shape,
        compiler_params=_params("arbitrary", "arbitrary"),
        name="matmul_res",
    )(*args)


def _merge_body(ol_a, ol_b, od_a, od_b, op_a, op_b, gl_ref, gd_ref, gp_ref, wl_ref, wd_ref, wp_ref, o_ref, *, n_first):
    def branch(o_a, o_b, g_ref_, w_ref_):
        o = _pick_rows(o_a, o_b, n_first)
        return g_ref_[...].astype(F32) * jnp.dot(o, w_ref_[...], preferred_element_type=F32)

    m = (branch(ol_a, ol_b, gl_ref, wl_ref) + branch(od_a, od_b, gd_ref, wd_ref)
         + branch(op_a, op_b, gp_ref, wp_ref))
    o_ref[...] = m.astype(o_ref.dtype)


def _merge(o_lru, o_dn, o_pool, gates, w_lru, w_dn, w_pool, l, *, tm, tn):
    half = o_lru[0].shape[-1]
    m = gates.shape[0]
    n_first = (m - o_lru[1].shape[0]) // tm
    d = w_lru.shape[-1]
    nj = d // tn

    def g_spec(b):
        return pl.BlockSpec((tm, tn), lambda i, j: (i, b * nj + j))

    def w_spec():
        return pl.BlockSpec((None, half, tn), lambda i, j: (l, 0, j))

    o_specs = _two_source_specs(tm, half, n_first)
    return pl.pallas_call(
        functools.partial(_merge_body, n_first=n_first),
        grid=(m // tm, nj),
        in_specs=[*o_specs, *o_specs, *o_specs, g_spec(0), g_spec(1), g_spec(2), w_spec(), w_spec(), w_spec()],
        out_specs=pl.BlockSpec((tm, tn), lambda i, j: (i, j)),
        out_shape=jax.ShapeDtypeStruct((m, d), BF16),
        compiler_params=_params("arbitrary", "arbitrary"),
        name="merge",
    )(*o_lru, *o_dn, *o_pool, gates, gates, gates, w_lru, w_dn, w_pool)


def _softplus(x):
    return jnp.maximum(x, 0.0) + jnp.log1p(jnp.exp(-jnp.abs(x)))


def _shift_rows(cur, prev, k, chained=False):
    ax = cur.ndim - 2
    row = lax.broadcasted_iota(jnp.int32, cur.shape, ax)
    if chained:
        rolled = pltpu.roll(jnp.concatenate([prev[None], cur], axis=0), k, ax)
        return jnp.where(row >= k, rolled[1:], rolled[:-1])
    return jnp.where(row >= k, pltpu.roll(cur, k, ax), pltpu.roll(prev, k, ax))


def _rows_before(state_ref, tile_rows):
    r, nb, c = state_ref.shape
    row = lax.broadcasted_iota(jnp.int32, (tile_rows, c), 0)
    tiles = []
    for b in range(nb):
        t = jnp.zeros((tile_rows, c), F32)
        for j in range(r):
            t = jnp.where(row == tile_rows - r + j, state_ref[j, b:b + 1, :], t)
        tiles.append(t)
    return jnp.stack(tiles, axis=0)


def _causal_conv(x, prev, cw, chained):
    y = cw[CONV_WIDTH - 1:CONV_WIDTH] * x
    for k in range(1, CONV_WIDTH):
        y = y + cw[CONV_WIDTH - 1 - k:CONV_WIDTH - k] * _shift_rows(x, prev, k, chained)
    return y


def _lru_body(*refs, long_seq):
    for _ in _lru_stages(refs[:11], refs[11:14], refs[14:], long_seq=long_seq, is_first=pl.program_id(1) == 0):
        pass


def _lru_stages(in_refs, out_refs, scratch, *, long_seq, is_first):
    (x_ref, gate_ref, prev_ref, h0_ref, cw_ref, cb_ref, wa_ref, ba_ref, wx_ref, bx_ref, lam_ref) = in_refs
    o_ref, hl_ref, nb_ref = out_refs
    cx_scr, ch_scr = scratch
    g_n, _, c = x_ref.shape
    rows = g_n * SUBLANES
    x = x_ref[...]
    if long_seq:
        @pl.when(is_first)
        def _():
            cx_scr[...] = prev_ref[0]
            ch_scr[...] = h0_ref[0]

        prev = cx_scr[...]
    else:
        prev = _rows_before(prev_ref, SUBLANES)
    xc = _causal_conv(x, prev, cw_ref[...], long_seq) + cb_ref[...]
    if long_seq:
        cx_scr[...] = x[g_n - 1]
        nb_ref[0] = x[g_n - 1]
    else:
        nb_ref[...] = x
    yield

    xc2 = xc.reshape(rows, c)
    xb = xc2.astype(BF16)
    nblk = c // GATE_TILE

    def gate(w_ref, b_ref):
        parts = [jnp.dot(xb[:, i * GATE_TILE:(i + 1) * GATE_TILE], w_ref[i], preferred_element_type=F32)
                 for i in range(nblk)]
        return jax.nn.sigmoid(jnp.concatenate(parts, axis=1) + b_ref[...])

    r = gate(wa_ref, ba_ref)
    yield
    i_g = gate(wx_ref, bx_ref)
    yield
    log_a = (-LRU_C) * r * _softplus(-lam_ref[...])
    a = jnp.exp(log_a)
    one_minus_a2 = -jnp.tanh(log_a) * (a * a + 1.0)
    b = jnp.sqrt(one_minus_a2) * (i_g * xc2)
    yield

    a3 = a.reshape(g_n, SUBLANES, c)
    b3 = b.reshape(g_n, SUBLANES, c)
    row = lax.broadcasted_iota(jnp.int32, a3.shape, 1)
    s = 1
    while s < SUBLANES:
        m = row >= s
        a_sh = pltpu.roll(a3, s, 1)
        b_sh = pltpu.roll(b3, s, 1)
        b3 = jnp.where(m, a3 * b_sh + b3, b3)
        a3 = jnp.where(m, a3 * a_sh, a3)
        s *= 2
        yield

    h_in = ch_scr[...] if long_seq else None
    h_groups = []
    for g in range(g_n):
        if not long_seq:
            h_in = h0_ref[g]
        hg = a3[g] * h_in + b3[g]
        h_groups.append(hg)
        h_in = hg[SUBLANES - 1:SUBLANES]
        if not long_seq:
            hl_ref[g] = h_in
    if long_seq:
        ch_scr[...] = h_in
        hl_ref[0] = h_in
    yield
    h = jnp.stack(h_groups, axis=0).reshape(rows, c)
    gl = gate_ref[...].reshape(rows, c)
    o_ref[...] = (h * jax.nn.gelu(gl)).astype(o_ref.dtype)


def _lru_call(proj3, col0, prev, h0, weights, l, *, long_seq, n_outer, n_inner, g_n, group0, state_layer):
    cw, cb, wa, ba, wx, bx, lam = weights
    c = cw.shape[-1]
    m = proj3.shape[0] * SUBLANES
    nseq_blk = 1 if long_seq else g_n
    nseq = n_outer * nseq_blk
    gb0 = group0 // g_n

    def x_spec(col):
        return pl.BlockSpec((g_n, SUBLANES, c), lambda i, t: (gb0 + i * n_inner + t, 0, col))

    if state_layer is None:
        prev_spec = pl.BlockSpec((nseq_blk, SUBLANES, c), lambda i, t: (i, 0, 0))
        h0_spec = pl.BlockSpec((nseq_blk, 1, c), lambda i, t: (i, 0, 0))
    else:
        prev_spec = pl.BlockSpec((None, prev.shape[1], nseq_blk, c), lambda i, t: (state_layer, 0, i, 0))
        h0_spec = pl.BlockSpec((None, nseq_blk, 1, c), lambda i, t: (state_layer, i, 0, 0))

    ng = c // GATE_TILE
    in_specs = [x_spec(col0), x_spec(col0 + 1), prev_spec, h0_spec,
                _layer(l, (CONV_WIDTH, c)), _layer(l, (1, c)),
                _layer(l, (ng, GATE_TILE, GATE_TILE)), _layer(l, (1, c)),
                _layer(l, (ng, GATE_TILE, GATE_TILE)), _layer(l, (1, c)), _layer(l, (1, c))]
    args = [proj3, proj3, prev, h0, cw, cb, wa, ba, wx, bx, lam]
    out_specs = [pl.BlockSpec((g_n * SUBLANES, c), lambda i, t: (i * n_inner + t, 0)),
                 pl.BlockSpec((nseq_blk, 1, c), lambda i, t: (i, 0, 0)),
                 pl.BlockSpec((nseq_blk, SUBLANES, c), lambda i, t: (i, 0, 0))]
    out_shape = [jax.ShapeDtypeStruct((n_outer * n_inner * g_n * SUBLANES, c), BF16),
                 jax.ShapeDtypeStruct((nseq, 1, c), F32),
                 jax.ShapeDtypeStruct((nseq, SUBLANES, c), F32)]
    return pl.pallas_call(
        functools.partial(_lru_body, long_seq=long_seq),
        grid=(n_outer, n_inner),
        in_specs=in_specs, out_specs=out_specs, out_shape=out_shape,
        scratch_shapes=[pltpu.VMEM((SUBLANES, c), F32), pltpu.VMEM((1, c), F32)],
        compiler_params=_params("parallel", "arbitrary"),
        name="lru_long" if long_seq else "lru_short",
    )(*args)


def _pool_body(*refs, long_seq, past_len):
    u_ref, hist_ref, pw_ref, ps_ref, o_ref, nh_ref, c_scr = refs
    g_n, _, c = u_ref.shape
    rows = g_n * SUBLANES
    x = u_ref[...]
    t = pl.program_id(1)
    if long_seq:
        @pl.when(t == 0)
        def _():
            c_scr[0] = hist_ref[0, 0:SUBLANES]
            c_scr[1] = hist_ref[0, SUBLANES:2 * SUBLANES]

        ext = jnp.concatenate([c_scr[...], x], axis=0)
        gs = 1
    else:
        hist = _rows_before(hist_ref, 2 * SUBLANES)
        ext = jnp.concatenate([hist[:, 0:SUBLANES], hist[:, SUBLANES:2 * SUBLANES], x], axis=0)
        gs = g_n
    nh_ref[:, 0:SUBLANES] = ext[-2 * gs:-gs]
    nh_ref[:, SUBLANES:2 * SUBLANES] = ext[-gs:]
    if long_seq:
        c_scr[0] = ext[g_n]
        c_scr[1] = ext[g_n + 1]

    def prev(a):
        return jnp.concatenate([a[:gs], a[:-gs]], axis=0)

    n_grp = len(POOL_WINDOWS)
    pg = c // n_grp
    i0 = lax.broadcasted_iota(jnp.int32, (g_n, SUBLANES, pg), 0)
    i1 = lax.broadcasted_iota(jnp.int32, (g_n, SUBLANES, pg), 1)
    t_abs = (t * g_n + i0) * SUBLANES + i1 if long_seq else i1
    outs = []
    for gi, w in enumerate(POOL_WINDOWS):
        eg = ext[:, :, gi * pg:(gi + 1) * pg]
        s = eg
        k = 1
        while k < w:
            if k < SUBLANES:
                rolled = pltpu.roll(s, k, 1)
                row = lax.broadcasted_iota(jnp.int32, s.shape, 1)
                s = s + jnp.where(row >= k, rolled, prev(rolled))
            else:
                s = s + prev(s)
            k *= 2
        cnt = jnp.minimum(t_abs + (1 + past_len), w).astype(F32)
        d = s[2 * gs:] / cnt - eg[2 * gs:]
        d2 = d.reshape(rows, pg).astype(BF16)
        outs.append(jnp.dot(d2, pw_ref[gi], preferred_element_type=F32))
    y = jnp.concatenate(outs, axis=1) * ps_ref[...]
    o_ref[...] = y.astype(o_ref.dtype)


def _pool_call(proj3, col, hist, pool_w, pool_scale, l, *, long_seq, past_len, n_outer, n_inner, g_n,
               group0, state_layer):
    c = pool_scale.shape[-1]
    m = proj3.shape[0] * SUBLANES
    nseq_blk = 1 if long_seq else g_n
    nseq = n_outer * nseq_blk
    gb0 = group0 // g_n
    pg = c // len(POOL_WINDOWS)
    hrows = 2 * SUBLANES
    if state_layer is None:
        h_spec = pl.BlockSpec((nseq_blk, hrows, c), lambda i, t: (i, 0, 0))
    else:
        h_spec = pl.BlockSpec((None, hist.shape[1], nseq_blk, c), lambda i, t: (state_layer, 0, i, 0))
    in_specs = [pl.BlockSpec((g_n, SUBLANES, c), lambda i, t: (gb0 + i * n_inner + t, 0, col)), h_spec,
                _layer(l, (len(POOL_WINDOWS), pg, pg)), _layer(l, (1, c))]
    args = [proj3, hist, pool_w, pool_scale]
    return pl.pallas_call(
        functools.partial(_pool_body, long_seq=long_seq, past_len=past_len),
        grid=(n_outer, n_inner),
        in_specs=in_specs,
        out_specs=[pl.BlockSpec((g_n * SUBLANES, c), lambda i, t: (i * n_inner + t, 0)),
                   pl.BlockSpec((nseq_blk, hrows, c), lambda i, t: (i, 0, 0))],
        out_shape=[jax.ShapeDtypeStruct((n_outer * n_inner * g_n * SUBLANES, c), BF16),
                   jax.ShapeDtypeStruct((nseq, hrows, c), F32)],
        scratch_shapes=[pltpu.VMEM((2, SUBLANES, c), F32)],
        compiler_params=_params("parallel", "arbitrary"),
        name="pool_long" if long_seq else "pool_short",
    )(*args)


CHUNK_STAGE_LAG = 2
CHUNK_STAGE_LAG_SHORT = 3
SOLVE_PASSES = 1


def _mm(a, b):
    a_hi, b_hi = a.astype(BF16), b.astype(BF16)
    out = jnp.dot(a_hi, b_hi, preferred_element_type=F32)
    if SOLVE_PASSES == 3:
        a_lo = (a - a_hi.astype(F32)).astype(BF16)
        b_lo = (b - b_hi.astype(F32)).astype(BF16)
        out = out + jnp.dot(a_hi, b_lo, preferred_element_type=F32) + jnp.dot(a_lo, b_hi, preferred_element_type=F32)
    return out


def _delta_body(*refs, long_seq, tseq, n_aliased):
    (q_ref, k_ref, v_ref, z_ref, ba_ref, prev_ref, s0_ref, cw_ref, alog_ref, dtb_ref, nw_ref) = refs[:11]
    o_ref, s_ref, nb_ref, cx_scr = refs[11 + n_aliased:]
    g_n, _, c = q_ref.shape
    rows = g_n * SUBLANES
    n_sub = rows // DN_CHUNK
    nseq_chunk = DN_CHUNK // tseq
    heads = c // HEAD_DIM

    if long_seq:
        @pl.when(pl.program_id(1) == 0)
        def _():
            cx_scr[...] = prev_ref[0]
            s_ref[...] = s0_ref[...]

    cw = cw_ref[...]
    streams = []
    for idx, x_ref in enumerate((q_ref, k_ref, v_ref)):
        x = x_ref[...]
        lo, hi = idx * c, (idx + 1) * c
        if long_seq:
            prev = cx_scr[:, lo:hi]
            cx_scr[:, lo:hi] = x[g_n - 1]
            nb_ref[0, :, lo:hi] = x[g_n - 1]
        else:
            prev = _rows_before(prev_ref.at[:, :, lo:hi], SUBLANES)
            nb_ref[:, :, lo:hi] = x
        y = _causal_conv(x, prev, cw[:, lo:hi], long_seq)
        streams.append((y * jax.nn.sigmoid(y)).reshape(rows, c))
    q_all, k_all, v_all = streams
    z_all = z_ref[...].reshape(rows, c)

    ba = ba_ref[...]
    beta_all = jax.nn.sigmoid(ba[:, 0:HEAD_DIM])
    g_all = -jnp.exp(alog_ref[...]) * _softplus(ba[:, HEAD_DIM:2 * HEAD_DIM] + dtb_ref[...])

    ri = lax.broadcasted_iota(jnp.int32, (DN_CHUNK, DN_CHUNK), 0)
    ci = lax.broadcasted_iota(jnp.int32, (DN_CHUNK, DN_CHUNK), 1)
    same = (ri // tseq) == (ci // tseq)
    incl = same & (ci <= ri)
    strict = same & (ci < ri)
    eye = (ri == ci).astype(F32)
    incl_f, same_f = incl.astype(F32), same.astype(F32)

    state = [s_ref[0, h] for h in range(heads)] if long_seq else None
    masks = (ri, ci, incl, strict, eye, incl_f, same_f)
    waiting = [_delta_chunk([(sub, h) for h in range(heads)], state, masks,
                            (q_all, k_all, v_all, z_all, beta_all, g_all), (s0_ref, s_ref, nw_ref, o_ref),
                            long_seq, tseq) for sub in range(n_sub)]
    running, rnd = [], 0
    while waiting or running:
        if waiting and rnd % (CHUNK_STAGE_LAG if long_seq else CHUNK_STAGE_LAG_SHORT) == 0:
            running.append(waiting.pop(0))
        for gen in list(running):
            if next(gen, "done") == "done":
                running.remove(gen)
        rnd += 1
    if long_seq:
        for h in range(heads):
            s_ref[0, h] = state[h]


def _delta_chunk(pairs, state, masks, streams, refs, long_seq, tseq):
    ri, ci, incl, strict, eye, incl_f, same_f = masks
    q_all, k_all, v_all, z_all, beta_all, g_all = streams
    s0_ref, s_ref, nw_ref, o_ref = refs
    nseq_chunk = DN_CHUNK // tseq
    gamma_s, gtot_s, gamma_ts = {}, {}, {}
    for sub in sorted({sub for sub, _ in pairs}):
        g_sub = g_all[sub * DN_CHUNK:(sub + 1) * DN_CHUNK]
        gamma = jnp.dot(incl_f, g_sub, precision=_HI, preferred_element_type=F32)
        gamma_s[sub] = gamma
        gtot_s[sub] = jnp.dot(same_f, g_sub, precision=_HI, preferred_element_type=F32)
        gamma_ts[sub] = gamma.T

    q_l, k_l, v_l, beta_l, gcol_l, gtot_l, decay_l, qb_l, kb_l, a_l = ([] for _ in range(10))
    for sub, h in pairs:
        rs = slice(sub * DN_CHUNK, (sub + 1) * DN_CHUNK)
        hs = slice(h * HEAD_DIM, (h + 1) * HEAD_DIM)
        q_h, k_h = q_all[rs, hs], k_all[rs, hs]
        q_h = q_h * lax.rsqrt(jnp.sum(q_h * q_h, axis=-1, keepdims=True) + EPS) * (HEAD_DIM ** -0.5)
        k_h = k_h * lax.rsqrt(jnp.sum(k_h * k_h, axis=-1, keepdims=True) + EPS)
        gcol = gamma_s[sub][:, h:h + 1]
        decay = jnp.where(incl, jnp.exp(gcol - gamma_ts[sub][h:h + 1, :]), 0.0)
        qb, kb = q_h.astype(BF16), k_h.astype(BF16)
        beta = beta_all[rs, h:h + 1]
        kk = lax.dot_general(kb, kb, _NT, preferred_element_type=F32)
        q_l.append(q_h)
        k_l.append(k_h)
        v_l.append(v_all[rs, hs])
        beta_l.append(beta)
        gcol_l.append(gcol)
        gtot_l.append(gtot_s[sub][:, h:h + 1])
        decay_l.append(decay)
        qb_l.append(qb)
        kb_l.append(kb)
        a_l.append(jnp.where(strict, beta * decay * kk, 0.0))
    yield

    t_l = [eye - jnp.where(ri // 2 == ci // 2, a, 0.0) for a in a_l]
    s = 2
    while s < tseq:
        off_mask = (ri // (2 * s) == ci // (2 * s)) & (ri % (2 * s) >= s) & (ci % (2 * s) < s)
        t_off = [_mm(t, jnp.where(off_mask, a, 0.0)) for t, a in zip(t_l, a_l)]
        t_l = [t - _mm(to, t) for t, to in zip(t_l, t_off)]
        s *= 2
        yield

    eg_l = [jnp.exp(g) for g in gcol_l]
    sol_l = [_mm(t, jnp.concatenate([b * v, (b * e) * k], axis=1))
             for t, b, v, e, k in zip(t_l, beta_l, v_l, eg_l, k_l)]
    qk_l = [(lax.dot_general(qb, kb, _NT, preferred_element_type=F32) * d).astype(BF16)
            for qb, kb, d in zip(qb_l, kb_l, decay_l)]
    qg_l = [(q * e).astype(BF16) for q, e in zip(q_l, eg_l)]
    kd_l = [(k * jnp.exp(gt - g)).astype(BF16) for k, gt, g in zip(k_l, gtot_l, gcol_l)]
    yield

    w_l, o_l, s_old = [], [], []
    for i, (sub, h) in enumerate(pairs):
        u_c, wkb = sol_l[i][:, :HEAD_DIM], sol_l[i][:, HEAD_DIM:].astype(BF16)
        w_parts, o_parts, s_mats = [], [], []
        for sq in range(nseq_chunk):
            rq = slice(sq * tseq, (sq + 1) * tseq)
            s_mat = state[h] if long_seq else s0_ref[sub * nseq_chunk + sq, h]
            sb = s_mat.astype(BF16)
            s_mats.append(s_mat)
            w_parts.append(u_c[rq] - jnp.dot(wkb[rq], sb, preferred_element_type=F32))
            o_parts.append(jnp.dot(qg_l[i][rq], sb, preferred_element_type=F32))
        w_l.append((jnp.concatenate(w_parts, axis=0) if nseq_chunk > 1 else w_parts[0]).astype(BF16))
        o_l.append(jnp.concatenate(o_parts, axis=0) if nseq_chunk > 1 else o_parts[0])
        s_old.append(s_mats)
    yield

    for i, (sub, h) in enumerate(pairs):
        w = w_l[i]
        for sq in range(nseq_chunk):
            rq = slice(sq * tseq, (sq + 1) * tseq)
            g_last = jnp.exp(gtot_l[i][sq * tseq:sq * tseq + 1])
            s_upd = g_last * s_old[i][sq] + lax.dot_general(kd_l[i][rq], w[rq], _TN, preferred_element_type=F32)
            if long_seq:
                state[h] = s_upd
            else:
                s_ref[sub * nseq_chunk + sq, h] = s_upd
        o = o_l[i] + jnp.dot(qk_l[i], w, preferred_element_type=F32)
        o = o * lax.rsqrt(jnp.mean(o * o, axis=-1, keepdims=True) + EPS) * nw_ref[...]
        rs = slice(sub * DN_CHUNK, (sub + 1) * DN_CHUNK)
        hs = slice(h * HEAD_DIM, (h + 1) * HEAD_DIM)
        z_h = z_all[rs, hs]
        o_ref[rs, hs] = (o * (z_h * jax.nn.sigmoid(z_h))).astype(o_ref.dtype)


def _delta_call(proj3, col0, ba, prev, s0, conv_w, alog, dtb, norm_w, l, *, long_seq, tseq, n_outer, n_inner,
                group0, state_layer, n_sub, s_stack=None):
    c = conv_w.shape[-1] // 3
    heads = c // HEAD_DIM
    blk_rows = n_sub * DN_CHUNK
    g_n = blk_rows // SUBLANES
    m = proj3.shape[0] * SUBLANES
    nseq_blk = 1 if long_seq else blk_rows // tseq
    nseq = n_outer * nseq_blk
    gb0 = group0 // g_n

    def x_spec(col):
        return pl.BlockSpec((g_n, SUBLANES, c), lambda i, t: (gb0 + i * n_inner + t, 0, col))

    if state_layer is None:
        p_spec = pl.BlockSpec((nseq_blk, SUBLANES, 3 * c), lambda i, t: (i, 0, 0))
        s_spec = pl.BlockSpec((nseq_blk, heads, HEAD_DIM, HEAD_DIM), lambda i, t: (i, 0, 0, 0))
    else:
        p_spec = pl.BlockSpec((None, prev.shape[1], nseq_blk, 3 * c), lambda i, t: (state_layer, 0, i, 0))
        s_spec = pl.BlockSpec((None, nseq_blk, heads, HEAD_DIM, HEAD_DIM), lambda i, t: (state_layer, i, 0, 0, 0))
    in_specs = [x_spec(col0), x_spec(col0 + 1), x_spec(col0 + 2), x_spec(col0 + 3),
                pl.BlockSpec((blk_rows, 2 * HEAD_DIM), lambda i, t: (gb0 + i * n_inner + t, 0)),
                p_spec, s_spec,
                _layer(l, (CONV_WIDTH, 3 * c)), _layer(l, (1, HEAD_DIM)), _layer(l, (1, HEAD_DIM)),
                _layer(l, (1, HEAD_DIM))]
    args = [proj3, proj3, proj3, proj3, ba, prev, s0, conv_w, alog, dtb, norm_w]
    aliases = {}
    s_block = (nseq_blk, heads, HEAD_DIM, HEAD_DIM)
    if s_stack is None:
        s_out_spec = pl.BlockSpec(s_block, lambda i, t: (i, 0, 0, 0))
        s_out_shape = jax.ShapeDtypeStruct((nseq,) + s_block[1:], F32)
    else:
        depth, stacked = s_stack
        s_out_spec = pl.BlockSpec((None,) + s_block, lambda i, t: (l, i, 0, 0, 0))
        s_out_shape = jax.ShapeDtypeStruct((depth, nseq) + s_block[1:], F32)
        if stacked is not None:
            in_specs.append(pl.BlockSpec(memory_space=pl.ANY))
            args.append(stacked)
            aliases[len(args) - 1] = 1
    return pl.pallas_call(
        functools.partial(_delta_body, long_seq=long_seq, tseq=tseq, n_aliased=len(aliases)),
        grid=(n_outer, n_inner),
        in_specs=in_specs,
        out_specs=[pl.BlockSpec((blk_rows, c), lambda i, t: (i * n_inner + t, 0)),
                   s_out_spec,
                   pl.BlockSpec((nseq_blk, SUBLANES, 3 * c), lambda i, t: (i, 0, 0))],
        out_shape=[jax.ShapeDtypeStruct((n_outer * n_inner * blk_rows, c), BF16),
                   s_out_shape,
                   jax.ShapeDtypeStruct((nseq, SUBLANES, 3 * c), F32)],
        scratch_shapes=[pltpu.VMEM((SUBLANES, 3 * c), F32)],
        input_output_aliases=aliases,
        compiler_params=_params("parallel", "arbitrary"),
        name="delta_long" if long_seq else "delta_short",
    )(*args)


def _cast_rows(w, steps):
    rows = w.shape[1]
    nblk = 1
    while nblk * 2 <= steps and rows % (nblk * 2) == 0 and rows // (nblk * 2) >= 2 * SUBLANES:
        nblk *= 2
    return rows // nblk


def _block_diag_tiles(w):
    depth, nb, b, _ = w.shape
    per = GATE_TILE // b
    w5 = w.reshape(depth, nb // per, per, b, b)
    eye = jnp.eye(per, dtype=w.dtype)
    t = jnp.einsum('lijcd,jk->lijckd', w5, eye)
    return t.reshape(depth, nb // per, GATE_TILE, GATE_TILE).astype(BF16)


def kernel(x_prompt, x_sample, state_lru_h, state_lru_conv, state_dn_s, state_dn_conv, state_pool, norm_mix, w_in, lru_conv_w, lru_conv_b, lru_gate_a_w, lru_gate_a_b, lru_gate_x_w, lru_gate_x_b, lru_lambda, dn_conv_w, dn_a_log, dn_dt_bias, dn_norm_w, pool_w, pool_scale, w_br_lru, w_br_dn, w_br_pool, w_out, norm_mlp, w_up, w_down, norm_final):
    bp, tp, d = x_prompt.shape
    bs, ts, _ = x_sample.shape
    depth = w_in.shape[0]
    half = d // 2
    heads = half // HEAD_DIM
    mp, ms = bp * tp, bs * ts
    m = mp + ms
    assert ts == SUBLANES and tp % DN_CHUNK == 0 and bs % (DN_CHUNK // ts) == 0
    assert lru_gate_a_w.shape[-1] == LRU_BLOCK and half % GATE_TILE == 0
    assert state_pool.shape[-2] == POOL_BUF

    n_pre = 6 * half
    ba0 = n_pre
    pool0 = ba0 + 2 * heads
    gates0 = pool0 + half
    w_in_t = jnp.swapaxes(w_in, 1, 2)
    n_gate_tiles, n_stream_tiles = 3 * d // half, n_pre // half + 1
    assert gates0 % SUBLANES == 0 and pool0 % SUBLANES == 0

    def gate_row0(j):
        return pl.multiple_of(gates0 + j * half, SUBLANES)

    def stream_row0(j):
        return pl.multiple_of(jnp.where(j < n_stream_tiles - 1, j * half, pool0), SUBLANES)

    row_fill = ((0, 0), (0, HEAD_DIM - heads), (0, 0))
    w_ba = jnp.concatenate([jnp.pad(w_in_t[:, ba0:ba0 + heads], row_fill),
                            jnp.pad(w_in_t[:, ba0 + heads:ba0 + 2 * heads], row_fill)], axis=1)
    col0 = 0

    def row(a):
        return a.reshape(depth, 1, a.shape[-1])

    def lane_pad(a):
        return jnp.pad(a, ((0, 0), (0, HEAD_DIM - a.shape[-1]))).reshape(depth, 1, HEAD_DIM)

    lru_w = (lru_conv_w, row(lru_conv_b), _block_diag_tiles(lru_gate_a_w), row(lru_gate_a_b),
             _block_diag_tiles(lru_gate_x_w), row(lru_gate_x_b), row(lru_lambda))
    alog, dtb, dn_nw = lane_pad(dn_a_log), lane_pad(dn_dt_bias), row(dn_norm_w)
    pool_wb, pool_sc = pool_w.astype(BF16), row(pool_scale)
    g_mix, g_mlp = row(norm_mix), row(norm_mlp)

    s_lru_h0 = state_lru_h.reshape(depth, bs, 1, half)
    s_lru_prev = jnp.swapaxes(state_lru_conv, 1, 2)
    s_dn_prev = jnp.swapaxes(state_dn_conv, 1, 2)
    s_pool_hist = jnp.swapaxes(state_pool, 1, 2)
    p_lru_h0 = jnp.zeros((bp, 1, half), F32)
    p_lru_prev = jnp.zeros((bp, SUBLANES, half), F32)
    p_dn_prev = jnp.zeros((bp, SUBLANES, 3 * half), F32)
    p_dn_s0 = jnp.zeros((bp, heads, HEAD_DIM, HEAD_DIM), F32)
    p_pool_hist = jnp.zeros((bp, 2 * SUBLANES, half), F32)

    tm = 512 if (mp % 512 == 0 and ms % 512 == 0) else 256
    tm_big = next(t for t in (1536, 1024, tm) if m % t == 0)
    lru_g = 64
    assert (tp // SUBLANES) % lru_g == 0 and bs % lru_g == 0 and m % tm == 0 and mp % tm == 0
    n_t = tp // SUBLANES // lru_g
    long_kw = dict(long_seq=True, n_outer=bp, n_inner=n_t, g_n=lru_g, group0=0, state_layer=None)
    short_kw = dict(long_seq=False, n_outer=bs // lru_g, n_inner=1, g_n=lru_g, group0=mp // SUBLANES)
    seq_per_blk = DN_CHUNK // ts
    dn_sub, dn_sub_s = 8, 2
    assert tp % (dn_sub * DN_CHUNK) == 0 and bs % (dn_sub_s * seq_per_blk) == 0

    p_states, s_states = [], []
    s_dn_stack = None
    norm_fin = norm_final.reshape(1, 1, d)
    x = (x_prompt.reshape(mp, d), x_sample.reshape(ms, d))
    xn, ba = _norm_cast(*x, g_mix, w_ba, 0, tm=tm)
    for l in range(depth):
        w_rows = (None, pl.Element(half), pl.Element(d))
        m_tiles = m // tm_big
        proj, w_br_dn_b, w_br_pool_b = _stream_matmul(
            xn, w_in_t, pl.BlockSpec(w_rows, lambda j, i, l=l: (l, stream_row0(j), 0)), (half, d), n_stream_tiles,
            tm=tm_big, tn=half, out_dtype=F32, w_rows_are_outputs=True,
            casts=[(w, l, _cast_rows(w, n_stream_tiles * m_tiles)) for w in (w_br_dn, w_br_pool)])
        proj3 = proj.reshape(m // SUBLANES, SUBLANES, proj.shape[-1])

        gates, w_out_b, w_br_lru_b = _stream_matmul(
            xn, w_in_t, pl.BlockSpec(w_rows, lambda j, i, l=l: (l, gate_row0(j), 0)), (half, d), n_gate_tiles,
            tm=tm_big, tn=half, out_dtype=BF16, w_rows_are_outputs=True, act="sigmoid",
            casts=[(w, l, _cast_rows(w, n_gate_tiles * m_tiles)) for w in (w_out, w_br_lru)])
        o_lru_p, p_h, p_cb = _lru_call(proj3, col0, p_lru_prev, p_lru_h0, lru_w, l, **long_kw)
        o_lru_s, s_h, s_cb = _lru_call(proj3, col0, s_lru_prev, s_lru_h0, lru_w, l, state_layer=l, **short_kw)

        o_dn_p, p_s, p_db = _delta_call(proj3, col0 + 2, ba, p_dn_prev, p_dn_s0, dn_conv_w, alog, dtb, dn_nw, l,
                                        long_seq=True, tseq=DN_CHUNK, n_outer=bp, n_inner=tp // (dn_sub * DN_CHUNK),
                                        group0=0, state_layer=None, n_sub=dn_sub)
        o_dn_s, s_dn_stack, s_db = _delta_call(proj3, col0 + 2, ba, s_dn_prev, state_dn_s, dn_conv_w, alog, dtb,
                                               dn_nw, l, long_seq=False, tseq=ts,
                                               n_outer=bs // (dn_sub_s * seq_per_blk), n_inner=1,
                                               group0=mp // SUBLANES, state_layer=l, n_sub=dn_sub_s,
                                               s_stack=(depth, s_dn_stack))

        o_pool_p, p_ph = _pool_call(proj3, col0 + 6, p_pool_hist, pool_wb, pool_sc, l, past_len=0, **long_kw)
        o_pool_s, s_ph = _pool_call(proj3, col0 + 6, s_pool_hist, pool_wb, pool_sc, l, past_len=PAST_LEN,
                                    state_layer=l, **short_kw)

        merged = _merge((o_lru_p, o_lru_s), (o_dn_p, o_dn_s), (o_pool_p, o_pool_s), gates,
                        w_br_lru_b, w_br_dn_b, w_br_pool_b, 0, tm=tm, tn=d)
        x, xn = _matmul_res(merged, w_out_b, 0, x, tm=tm, tk=d, gain=g_mlp, gain_layer=l)
        w_up_spec = pl.BlockSpec((None, d, half), lambda j, i, l=l: (l, 0, j))
        n_up_tiles = w_up.shape[-1] // half
        hm, w_down_b = _stream_matmul(xn, w_up, w_up_spec, (d, half), n_up_tiles, tm=tm_big, tn=half,
                                      out_dtype=BF16, w_rows_are_outputs=False, act="relu2",
                                      casts=[(w_down, l, _cast_rows(w_down, n_up_tiles * m_tiles))])
        if l + 1 < depth:
            x, xn, ba = _matmul_res(hm, w_down_b, 0, x, tm=tm, tk=d, gain=g_mix, gain_layer=l + 1, w_small=w_ba)
        else:
            _, y_prompt, y_sample = _matmul_res(hm, w_down_b, 0, x, tm=tm, tk=half, gain=norm_fin, gain_layer=0,
                                                split_rows=mp)
        p_states.append((p_h, p_cb, p_s, p_db, p_ph))
        s_states.append((s_h, s_cb, None, s_db, s_ph))

    y_prompt = y_prompt.reshape(bp, tp, d)
    y_sample = y_sample.reshape(bs, ts, d)

    def collect(states, nseq, sm=None):
        tail = CONV_WIDTH - 1
        h = jnp.stack([s[0] for s in states]).reshape(depth, nseq, half)
        cb = jnp.stack([s[1] for s in states])[:, :, SUBLANES - tail:]
        if sm is None:
            sm = jnp.stack([s[2] for s in states])
        db = jnp.stack([s[3] for s in states])[:, :, SUBLANES - tail:]
        ph = jnp.stack([s[4] for s in states])[:, :, 2 * SUBLANES - POOL_BUF:]
        return h, cb, sm, db, ph

    return (y_prompt, y_sample) + collect(p_states, bp) + collect(s_states, bs, s_dn_stack)
```

```python
import functools

import jax
import jax.numpy as jnp
from jax import lax
from jax.experimental import pallas as pl
from jax.experimental.pallas import tpu as pltpu

F32 = jnp.float32
BF16 = jnp.bfloat16

EPS = 1e-6
LRU_C = 8.0
CONV_WIDTH = 4
HEAD_DIM = 128
LRU_BLOCK = 64
POOL_WINDOWS = (2, 4, 8, 16)
POOL_BUF = 15
DN_CHUNK = 64
PAST_LEN = 16384
SUBLANES = 8
GATE_TILE = 256
VMEM_LIMIT = 56 * 1024 * 1024

_NT = (((1,), (1,)), ((), ()))
_TN = (((0,), (0,)), ((), ()))
_HI = lax.Precision.HIGHEST


def _params(*sem):
    return pltpu.CompilerParams(dimension_semantics=sem, vmem_limit_bytes=VMEM_LIMIT)


def _layer(l, shape):
    return pl.BlockSpec((None,) + shape, lambda *_: (l,) + (0,) * len(shape))


def _rms(x, gain):
    return x * lax.rsqrt(jnp.mean(x * x, axis=-1, keepdims=True) + EPS) * gain


def _two_source_specs(tm, c, n_first):
    first = pl.BlockSpec((tm, c), lambda i, *_: (jnp.minimum(i, n_first - 1), 0))
    second = pl.BlockSpec((tm, c), lambda i, *_: (jnp.maximum(i - n_first, 0), 0))
    return first, second


def _pick_rows(first_ref, second_ref, n_first):
    return jnp.where(pl.program_id(0) < n_first, first_ref[...], second_ref[...])


def _norm_cast_body(xa_ref, xb_ref, g_ref, ws_ref, xn_ref, os_ref, *, n_first):
    x = _pick_rows(xa_ref, xb_ref, n_first)
    xn = _rms(x, g_ref[...]).astype(BF16)
    xn_ref[...] = xn
    os_ref[...] = lax.dot_general(xn, ws_ref[...].astype(BF16), _NT, preferred_element_type=F32)


def _norm_cast(xa, xb, gain, w_small, l, *, tm):
    d = xa.shape[-1]
    m = xa.shape[0] + xb.shape[0]
    n_first = xa.shape[0] // tm
    ns = w_small.shape[-2]
    return pl.pallas_call(
        functools.partial(_norm_cast_body, n_first=n_first),
        grid=(m // tm,),
        in_specs=[*_two_source_specs(tm, d, n_first), _layer(l, (1, d)), _layer(l, (ns, d))],
        out_specs=[pl.BlockSpec((tm, d), lambda i: (i, 0)), pl.BlockSpec((tm, ns), lambda i: (i, 0))],
        out_shape=[jax.ShapeDtypeStruct((m, d), BF16), jax.ShapeDtypeStruct((m, ns), F32)],
        compiler_params=_params("arbitrary"),
        name="norm_cast",
    )(xa, xb, gain, w_small)


def _stream_matmul_body(*refs, w_rows_are_outputs, act, cast_blocks):
    n_cast = len(cast_blocks)
    x_ref, w_ref = refs[:2]
    src_refs = refs[2:2 + n_cast]
    o_ref = refs[2 + n_cast]
    dst_refs = refs[3 + n_cast:3 + 2 * n_cast]
    wb_ref = refs[3 + 2 * n_cast]

    @pl.when(pl.program_id(1) == 0)
    def _():
        wb_ref[...] = w_ref[...].astype(BF16)

    step = pl.program_id(0) * pl.num_programs(1) + pl.program_id(1)
    for src, dst, nblk in zip(src_refs, dst_refs, cast_blocks):
        @pl.when(step < nblk)
        def _(src=src, dst=dst):
            dst[...] = src[...].astype(BF16)

    dims = _NT if w_rows_are_outputs else (((1,), (0,)), ((), ()))
    acc = lax.dot_general(x_ref[...], wb_ref[...], dims, preferred_element_type=F32)
    if act == "relu2":
        acc = jnp.square(jnp.maximum(acc, 0.0))
    elif act == "sigmoid":
        acc = jax.nn.sigmoid(acc)
    o_ref[...] = acc.astype(o_ref.dtype)


def _stream_matmul(x, w, w_spec, w_block, n_tiles, *, tm, tn, out_dtype, w_rows_are_outputs, act=None, casts=()):
    m, d = x.shape
    m_tiles = m // tm
    in_specs = [pl.BlockSpec((tm, d), lambda j, i: (i, 0)), w_spec]
    out_specs = [pl.BlockSpec((tm, tn), lambda j, i: (i, j))]
    out_shape = [jax.ShapeDtypeStruct((m, n_tiles * tn), out_dtype)]
    cast_blocks = []
    for src, layer, rps in casts:
        _, rows, cols = src.shape
        nblk = rows // rps
        assert rows % rps == 0 and nblk <= n_tiles * m_tiles
        cast_blocks.append(nblk)
        in_specs.append(pl.BlockSpec((None, rps, cols),
                                     lambda j, i, layer=layer, nblk=nblk: (layer, jnp.minimum(j * m_tiles + i, nblk - 1), 0)))
        out_specs.append(pl.BlockSpec((None, rps, cols),
                                      lambda j, i, nblk=nblk: (0, jnp.minimum(j * m_tiles + i, nblk - 1), 0)))
        out_shape.append(jax.ShapeDtypeStruct((1, rows, cols), BF16))
    res = pl.pallas_call(
        functools.partial(_stream_matmul_body, w_rows_are_outputs=w_rows_are_outputs, act=act,
                          cast_blocks=tuple(cast_blocks)),
        grid=(n_tiles, m_tiles),
        in_specs=in_specs, out_specs=out_specs, out_shape=out_shape,
        scratch_shapes=[pltpu.VMEM(w_block, BF16)],
        compiler_params=_params("arbitrary", "arbitrary"),
        name="stream_matmul_t" if w_rows_are_outputs else "stream_matmul",
    )(x, w, *[c[0] for c in casts])
    return res if casts else res[0]


def _matmul_res_body(*refs, n_k, emit_norm, emit_small, n_first, res_first):
    a_ref, w_ref = refs[:2]
    if res_first is None:
        r_ref = refs[2]
        refs = refs[3:]

        def residual():
            return r_ref[...]
    else:
        ra_ref, rb_ref = refs[2:4]
        refs = refs[4:]

        def residual():
            return _pick_rows(ra_ref, rb_ref, res_first)
    if emit_norm:
        g_ref = refs[0]
        refs = refs[1:]
    if emit_small:
        ws_ref = refs[0]
        refs = refs[1:]
    o_ref = refs[0]
    k = pl.program_id(1)

    def emit(acc):
        if n_first is not None:
            y = _rms(acc, g_ref[...])
            i = pl.program_id(0)

            @pl.when(i < n_first)
            def _():
                refs[1][...] = y

            @pl.when(i >= n_first)
            def _():
                refs[2][...] = y
            return
        xn = _rms(acc, g_ref[...]).astype(BF16)
        refs[1][...] = xn
        if emit_small:
            refs[2][...] = lax.dot_general(xn, ws_ref[...].astype(BF16), _NT, preferred_element_type=F32)

    if n_k == 1:
        acc = residual() + jnp.dot(a_ref[...], w_ref[...], preferred_element_type=F32)
        o_ref[...] = acc
        if emit_norm:
            emit(acc)
        return

    @pl.when(k == 0)
    def _():
        o_ref[...] = residual()

    o_ref[...] += jnp.dot(a_ref[...], w_ref[...], preferred_element_type=F32)

    if emit_norm:
        pl.when(k == n_k - 1)(lambda: emit(o_ref[...]))


def _matmul_res(a, w, l, res, *, tm, tk, gain=None, gain_layer=0, w_small=None, split_rows=None):
    m, k = a.shape
    d = w.shape[-1]
    n_k = k // tk
    emit_norm = gain is not None
    emit_small = w_small is not None
    n_first = None if split_rows is None else split_rows // tm
    in_specs = [pl.BlockSpec((tm, tk), lambda i, kk: (i, kk)),
                pl.BlockSpec((None, tk, d), lambda i, kk: (l, kk, 0))]
    if isinstance(res, tuple):
        res_first = res[0].shape[0] // tm
        in_specs.extend(_two_source_specs(tm, d, res_first))
        args = [a, w, *res]
    else:
        res_first = None
        in_specs.append(pl.BlockSpec((tm, d), lambda i, kk: (i, 0)))
        args = [a, w, res]
    out_specs = [pl.BlockSpec((tm, d), lambda i, kk: (i, 0))]
    out_shape = [jax.ShapeDtypeStruct((m, d), F32)]
    if emit_norm:
        in_specs.append(_layer(gain_layer, (1, d)))
        args.append(gain)
        if n_first is None:
            out_specs.append(pl.BlockSpec((tm, d), lambda i, kk: (i, 0)))
            out_shape.append(jax.ShapeDtypeStruct((m, d), BF16))
        else:
            assert split_rows % tm == 0 and not emit_small
            out_specs.extend(_two_source_specs(tm, d, n_first))
            out_shape.extend([jax.ShapeDtypeStruct((split_rows, d), F32),
                              jax.ShapeDtypeStruct((m - split_rows, d), F32)])
    if emit_small:
        ns = w_small.shape[-2]
        in_specs.append(_layer(gain_layer, (ns, d)))
        args.append(w_small)
        out_specs.append(pl.BlockSpec((tm, ns), lambda i, kk: (i, 0)))
        out_shape.append(jax.ShapeDtypeStruct((m, ns), F32))
    return pl.pallas_call(
        functools.partial(_matmul_res_body, n_k=n_k, emit_norm=emit_norm, emit_small=emit_small, n_first=n_first,
                          res_first=res_first),
        grid=(m // tm, n_k),
        in_specs=in_specs, out_specs=out_specs, out_shape=out_shape,
        compiler_params=_params("arbitrary", "arbitrary"),
        name="matmul_res",
    )(*args)


def _merge_body(ol_a, ol_b, od_a, od_b, op_a, op_b, gl_ref, gd_ref, gp_ref, wl_ref, wd_ref, wp_ref, o_ref, *, n_first):
    def branch(o_a, o_b, g_ref_, w_ref_):
        o = _pick_rows(o_a, o_b, n_first)
        return g_ref_[...].astype(F32) * jnp.dot(o, w_ref_[...], preferred_element_type=F32)

    m = (branch(ol_a, ol_b, gl_ref, wl_ref) + branch(od_a, od_b, gd_ref, wd_ref)
         + branch(op_a, op_b, gp_ref, wp_ref))
    o_ref[...] = m.astype(o_ref.dtype)


def _merge(o_lru, o_dn, o_pool, gates, w_lru, w_dn, w_pool, l, *, tm, tn):
    half = o_lru[0].shape[-1]
    m = gates.shape[0]
    n_first = (m - o_lru[1].shape[0]) // tm
    d = w_lru.shape[-1]
    nj = d // tn

    def g_spec(b):
        return pl.BlockSpec((tm, tn), lambda i, j: (i, b * nj + j))

    def w_spec():
        return pl.BlockSpec((None, half, tn), lambda i, j: (l, 0, j))

    o_specs = _two_source_specs(tm, half, n_first)
    return pl.pallas_call(
        functools.partial(_merge_body, n_first=n_first),
        grid=(m // tm, nj),
        in_specs=[*o_specs, *o_specs, *o_specs, g_spec(0), g_spec(1), g_spec(2), w_spec(), w_spec(), w_spec()],
        out_specs=pl.BlockSpec((tm, tn), lambda i, j: (i, j)),
        out_shape=jax.ShapeDtypeStruct((m, d), BF16),
        compiler_params=_params("arbitrary", "arbitrary"),
        name="merge",
    )(*o_lru, *o_dn, *o_pool, gates, gates, gates, w_lru, w_dn, w_pool)


def _softplus(x):
    return jnp.maximum(x, 0.0) + jnp.log1p(jnp.exp(-jnp.abs(x)))


def _shift_rows(cur, prev, k, chained=False):
    ax = cur.ndim - 2
    row = lax.broadcasted_iota(jnp.int32, cur.shape, ax)
    if chained:
        rolled = pltpu.roll(jnp.concatenate([prev[None], cur], axis=0), k, ax)
        return jnp.where(row >= k, rolled[1:], rolled[:-1])
    return jnp.where(row >= k, pltpu.roll(cur, k, ax), pltpu.roll(prev, k, ax))


def _rows_before(state_ref, tile_rows):
    r, nb, c = state_ref.shape
    row = lax.broadcasted_iota(jnp.int32, (tile_rows, c), 0)
    tiles = []
    for b in range(nb):
        t = jnp.zeros((tile_rows, c), F32)
        for j in range(r):
            t = jnp.where(row == tile_rows - r + j, state_ref[j, b:b + 1, :], t)
        tiles.append(t)
    return jnp.stack(tiles, axis=0)


def _causal_conv(x, prev, cw, chained):
    y = cw[CONV_WIDTH - 1:CONV_WIDTH] * x
    for k in range(1, CONV_WIDTH):
        y = y + cw[CONV_WIDTH - 1 - k:CONV_WIDTH - k] * _shift_rows(x, prev, k, chained)
    return y


def _lru_body(*refs, long_seq):
    for _ in _lru_stages(refs[:11], refs[11:14], refs[14:], long_seq=long_seq, is_first=pl.program_id(1) == 0):
        pass


def _lru_stages(in_refs, out_refs, scratch, *, long_seq, is_first):
    (x_ref, gate_ref, prev_ref, h0_ref, cw_ref, cb_ref, wa_ref, ba_ref, wx_ref, bx_ref, lam_ref) = in_refs
    o_ref, hl_ref, nb_ref = out_refs
    cx_scr, ch_scr = scratch
    g_n, _, c = x_ref.shape
    rows = g_n * SUBLANES
    x = x_ref[...]
    if long_seq:
        @pl.when(is_first)
        def _():
            cx_scr[...] = prev_ref[0]
            ch_scr[...] = h0_ref[0]

        prev = cx_scr[...]
    else:
        prev = _rows_before(prev_ref, SUBLANES)
    xc = _causal_conv(x, prev, cw_ref[...], long_seq) + cb_ref[...]
    if long_seq:
        cx_scr[...] = x[g_n - 1]
        nb_ref[0] = x[g_n - 1]
    else:
        nb_ref[...] = x
    yield

    xc2 = xc.reshape(rows, c)
    xb = xc2.astype(BF16)
    nblk = c // GATE_TILE

    def gate(w_ref, b_ref):
        parts = [jnp.dot(xb[:, i * GATE_TILE:(i + 1) * GATE_TILE], w_ref[i], preferred_element_type=F32)
                 for i in range(nblk)]
        return jax.nn.sigmoid(jnp.concatenate(parts, axis=1) + b_ref[...])

    r = gate(wa_ref, ba_ref)
    yield
    i_g = gate(wx_ref, bx_ref)
    yield
    log_a = (-LRU_C) * r * _softplus(-lam_ref[...])
    a = jnp.exp(log_a)
    one_minus_a2 = -jnp.tanh(log_a) * (a * a + 1.0)
    b = jnp.sqrt(one_minus_a2) * (i_g * xc2)
    yield

    a3 = a.reshape(g_n, SUBLANES, c)
    b3 = b.reshape(g_n, SUBLANES, c)
    row = lax.broadcasted_iota(jnp.int32, a3.shape, 1)
    s = 1
    while s < SUBLANES:
        m = row >= s
        a_sh = pltpu.roll(a3, s, 1)
        b_sh = pltpu.roll(b3, s, 1)
        b3 = jnp.where(m, a3 * b_sh + b3, b3)
        a3 = jnp.where(m, a3 * a_sh, a3)
        s *= 2
        yield

    h_in = ch_scr[...] if long_seq else None
    h_groups = []
    for g in range(g_n):
        if not long_seq:
            h_in = h0_ref[g]
        hg = a3[g] * h_in + b3[g]
        h_groups.append(hg)
        h_in = hg[SUBLANES - 1:SUBLANES]
        if not long_seq:
            hl_ref[g] = h_in
    if long_seq:
        ch_scr[...] = h_in
        hl_ref[0] = h_in
    yield
    h = jnp.stack(h_groups, axis=0).reshape(rows, c)
    gl = gate_ref[...].reshape(rows, c)
    o_ref[...] = (h * jax.nn.gelu(gl)).astype(o_ref.dtype)


def _lru_call(proj3, col0, prev, h0, weights, l, *, long_seq, n_outer, n_inner, g_n, group0, state_layer):
    cw, cb, wa, ba, wx, bx, lam = weights
    c = cw.shape[-1]
    nseq_blk = 1 if long_seq else g_n
    nseq = n_outer * nseq_blk
    gb0 = group0 // g_n

    def x_spec(col):
        return pl.BlockSpec((g_n, SUBLANES, c), lambda i, t: (gb0 + i * n_inner + t, 0, col))

    if state_layer is None:
        prev_spec = pl.BlockSpec((nseq_blk, SUBLANES, c), lambda i, t: (i, 0, 0))
        h0_spec = pl.BlockSpec((nseq_blk, 1, c), lambda i, t: (i, 0, 0))
    else:
        prev_spec = pl.BlockSpec((None, prev.shape[1], nseq_blk, c), lambda i, t: (state_layer, 0, i, 0))
        h0_spec = pl.BlockSpec((None, nseq_blk, 1, c), lambda i, t: (state_layer, i, 0, 0))

    ng = c // GATE_TILE
    in_specs = [x_spec(col0), x_spec(col0 + 1), prev_spec, h0_spec,
                _layer(l, (CONV_WIDTH, c)), _layer(l, (1, c)),
                _layer(l, (ng, GATE_TILE, GATE_TILE)), _layer(l, (1, c)),
                _layer(l, (ng, GATE_TILE, GATE_TILE)), _layer(l, (1, c)), _layer(l, (1, c))]
    args = [proj3, proj3, prev, h0, cw, cb, wa, ba, wx, bx, lam]
    out_specs = [pl.BlockSpec((g_n * SUBLANES, c), lambda i, t: (i * n_inner + t, 0)),
                 pl.BlockSpec((nseq_blk, 1, c), lambda i, t: (i, 0, 0)),
                 pl.BlockSpec((nseq_blk, SUBLANES, c), lambda i, t: (i, 0, 0))]
    out_shape = [jax.ShapeDtypeStruct((n_outer * n_inner * g_n * SUBLANES, c), BF16),
                 jax.ShapeDtypeStruct((nseq, 1, c), F32),
                 jax.ShapeDtypeStruct((nseq, SUBLANES, c), F32)]
    return pl.pallas_call(
        functools.partial(_lru_body, long_seq=long_seq),
        grid=(n_outer, n_inner),
        in_specs=in_specs, out_specs=out_specs, out_shape=out_shape,
        scratch_shapes=[pltpu.VMEM((SUBLANES, c), F32), pltpu.VMEM((1, c), F32)],
        compiler_params=_params("parallel", "arbitrary"),
        name="lru_long" if long_seq else "lru_short",
    )(*args)


def _pool_body(*refs, long_seq, past_len):
    u_ref, hist_ref, pw_ref, ps_ref, o_ref, nh_ref, c_scr = refs
    g_n, _, c = u_ref.shape
    rows = g_n * SUBLANES
    x = u_ref[...]
    t = pl.program_id(1)
    if long_seq:
        @pl.when(t == 0)
        def _():
            c_scr[0] = hist_ref[0, 0:SUBLANES]
            c_scr[1] = hist_ref[0, SUBLANES:2 * SUBLANES]

        ext = jnp.concatenate([c_scr[...], x], axis=0)
        gs = 1
    else:
        hist = _rows_before(hist_ref, 2 * SUBLANES)
        ext = jnp.concatenate([hist[:, 0:SUBLANES], hist[:, SUBLANES:2 * SUBLANES], x], axis=0)
        gs = g_n
    nh_ref[:, 0:SUBLANES] = ext[-2 * gs:-gs]
    nh_ref[:, SUBLANES:2 * SUBLANES] = ext[-gs:]
    if long_seq:
        c_scr[0] = ext[g_n]
        c_scr[1] = ext[g_n + 1]

    def prev(a):
        return jnp.concatenate([a[:gs], a[:-gs]], axis=0)

    n_grp = len(POOL_WINDOWS)
    pg = c // n_grp
    i0 = lax.broadcasted_iota(jnp.int32, (g_n, SUBLANES, pg), 0)
    i1 = lax.broadcasted_iota(jnp.int32, (g_n, SUBLANES, pg), 1)
    t_abs = (t * g_n + i0) * SUBLANES + i1 if long_seq else i1
    outs = []
    for gi, w in enumerate(POOL_WINDOWS):
        eg = ext[:, :, gi * pg:(gi + 1) * pg]
        s = eg
        k = 1
        while k < w:
            if k < SUBLANES:
                rolled = pltpu.roll(s, k, 1)
                row = lax.broadcasted_iota(jnp.int32, s.shape, 1)
                s = s + jnp.where(row >= k, rolled, prev(rolled))
            else:
                s = s + prev(s)
            k *= 2
        cnt = jnp.minimum(t_abs + (1 + past_len), w).astype(F32)
        d = s[2 * gs:] / cnt - eg[2 * gs:]
        d2 = d.reshape(rows, pg).astype(BF16)
        outs.append(jnp.dot(d2, pw_ref[gi], preferred_element_type=F32))
    y = jnp.concatenate(outs, axis=1) * ps_ref[...]
    o_ref[...] = y.astype(o_ref.dtype)


def _pool_call(proj3, col, hist, pool_w, pool_scale, l, *, long_seq, past_len, n_outer, n_inner, g_n,
               group0, state_layer):
    c = pool_scale.shape[-1]
    nseq_blk = 1 if long_seq else g_n
    nseq = n_outer * nseq_blk
    gb0 = group0 // g_n
    pg = c // len(POOL_WINDOWS)
    hrows = 2 * SUBLANES
    if state_layer is None:
        h_spec = pl.BlockSpec((nseq_blk, hrows, c), lambda i, t: (i, 0, 0))
    else:
        h_spec = pl.BlockSpec((None, hist.shape[1], nseq_blk, c), lambda i, t: (state_layer, 0, i, 0))
    in_specs = [pl.BlockSpec((g_n, SUBLANES, c), lambda i, t: (gb0 + i * n_inner + t, 0, col)), h_spec,
                _layer(l, (len(POOL_WINDOWS), pg, pg)), _layer(l, (1, c))]
    args = [proj3, hist, pool_w, pool_scale]
    return pl.pallas_call(
        functools.partial(_pool_body, long_seq=long_seq, past_len=past_len),
        grid=(n_outer, n_inner),
        in_specs=in_specs,
        out_specs=[pl.BlockSpec((g_n * SUBLANES, c), lambda i, t: (i * n_inner + t, 0)),
                   pl.BlockSpec((nseq_blk, hrows, c), lambda i, t: (i, 0, 0))],
        out_shape=[jax.ShapeDtypeStruct((n_outer * n_inner * g_n * SUBLANES, c), BF16),
                   jax.ShapeDtypeStruct((nseq, hrows, c), F32)],
        scratch_shapes=[pltpu.VMEM((2, SUBLANES, c), F32)],
        compiler_params=_params("parallel", "arbitrary"),
        name="pool_long" if long_seq else "pool_short",
    )(*args)


CHUNK_STAGE_LAG = 2
CHUNK_STAGE_LAG_SHORT = 3
SOLVE_PASSES = 1


def _mm(a, b):
    a_hi, b_hi = a.astype(BF16), b.astype(BF16)
    out = jnp.dot(a_hi, b_hi, preferred_element_type=F32)
    if SOLVE_PASSES == 3:
        a_lo = (a - a_hi.astype(F32)).astype(BF16)
        b_lo = (b - b_hi.astype(F32)).astype(BF16)
        out = out + jnp.dot(a_hi, b_lo, preferred_element_type=F32) + jnp.dot(a_lo, b_hi, preferred_element_type=F32)
    return out


def _delta_body(*refs, long_seq, tseq, n_aliased):
    (q_ref, k_ref, v_ref, z_ref, ba_ref, prev_ref, s0_ref, cw_ref, alog_ref, dtb_ref, nw_ref) = refs[:11]
    o_ref, s_ref, nb_ref, cx_scr = refs[11 + n_aliased:]
    g_n, _, c = q_ref.shape
    rows = g_n * SUBLANES
    n_sub = rows // DN_CHUNK
    nseq_chunk = DN_CHUNK // tseq
    heads = c // HEAD_DIM

    if long_seq:
        @pl.when(pl.program_id(1) == 0)
        def _():
            cx_scr[...] = prev_ref[0]
            s_ref[...] = s0_ref[...]

    cw = cw_ref[...]
    streams = []
    for idx, x_ref in enumerate((q_ref, k_ref, v_ref)):
        x = x_ref[...]
        lo, hi = idx * c, (idx + 1) * c
        if long_seq:
            prev = cx_scr[:, lo:hi]
            cx_scr[:, lo:hi] = x[g_n - 1]
            nb_ref[0, :, lo:hi] = x[g_n - 1]
        else:
            prev = _rows_before(prev_ref.at[:, :, lo:hi], SUBLANES)
            nb_ref[:, :, lo:hi] = x
        y = _causal_conv(x, prev, cw[:, lo:hi], long_seq)
        streams.append((y * jax.nn.sigmoid(y)).reshape(rows, c))
    q_all, k_all, v_all = streams
    z_all = z_ref[...].reshape(rows, c)

    ba = ba_ref[...]
    beta_all = jax.nn.sigmoid(ba[:, 0:HEAD_DIM])
    g_all = -jnp.exp(alog_ref[...]) * _softplus(ba[:, HEAD_DIM:2 * HEAD_DIM] + dtb_ref[...])

    ri = lax.broadcasted_iota(jnp.int32, (DN_CHUNK, DN_CHUNK), 0)
    ci = lax.broadcasted_iota(jnp.int32, (DN_CHUNK, DN_CHUNK), 1)
    same = (ri // tseq) == (ci // tseq)
    incl = same & (ci <= ri)
    strict = same & (ci < ri)
    eye = (ri == ci).astype(F32)
    incl_f, same_f = incl.astype(F32), same.astype(F32)

    state = [s_ref[0, h] for h in range(heads)] if long_seq else None
    masks = (ri, ci, incl, strict, eye, incl_f, same_f)
    waiting = [_delta_chunk([(sub, h) for h in range(heads)], state, masks,
                            (q_all, k_all, v_all, z_all, beta_all, g_all), (s0_ref, s_ref, nw_ref, o_ref),
                            long_seq, tseq) for sub in range(n_sub)]
    running, rnd = [], 0
    while waiting or running:
        if waiting and rnd % (CHUNK_STAGE_LAG if long_seq else CHUNK_STAGE_LAG_SHORT) == 0:
            running.append(waiting.pop(0))
        for gen in list(running):
            if next(gen, "done") == "done":
                running.remove(gen)
        rnd += 1
    if long_seq:
        for h in range(heads):
            s_ref[0, h] = state[h]


def _delta_chunk(pairs, state, masks, streams, refs, long_seq, tseq):
    ri, ci, incl, strict, eye, incl_f, same_f = masks
    q_all, k_all, v_all, z_all, beta_all, g_all = streams
    s0_ref, s_ref, nw_ref, o_ref = refs
    nseq_chunk = DN_CHUNK // tseq
    gamma_s, gtot_s, gamma_ts = {}, {}, {}
    for sub in sorted({sub for sub, _ in pairs}):
        g_sub = g_all[sub * DN_CHUNK:(sub + 1) * DN_CHUNK]
        gamma = jnp.dot(incl_f, g_sub, precision=_HI, preferred_element_type=F32)
        gamma_s[sub] = gamma
        gtot_s[sub] = jnp.dot(same_f, g_sub, precision=_HI, preferred_element_type=F32)
        gamma_ts[sub] = gamma.T

    q_l, k_l, v_l, beta_l, gcol_l, gtot_l, decay_l, qb_l, kb_l, a_l = ([] for _ in range(10))
    for sub, h in pairs:
        rs = slice(sub * DN_CHUNK, (sub + 1) * DN_CHUNK)
        hs = slice(h * HEAD_DIM, (h + 1) * HEAD_DIM)
        q_h, k_h = q_all[rs, hs], k_all[rs, hs]
        q_h = q_h * lax.rsqrt(jnp.sum(q_h * q_h, axis=-1, keepdims=True) + EPS) * (HEAD_DIM ** -0.5)
        k_h = k_h * lax.rsqrt(jnp.sum(k_h * k_h, axis=-1, keepdims=True) + EPS)
        gcol = gamma_s[sub][:, h:h + 1]
        decay = jnp.where(incl, jnp.exp(gcol - gamma_ts[sub][h:h + 1, :]), 0.0)
        qb, kb = q_h.astype(BF16), k_h.astype(BF16)
        beta = beta_all[rs, h:h + 1]
        kk = lax.dot_general(kb, kb, _NT, preferred_element_type=F32)
        q_l.append(q_h)
        k_l.append(k_h)
        v_l.append(v_all[rs, hs])
        beta_l.append(beta)
        gcol_l.append(gcol)
        gtot_l.append(gtot_s[sub][:, h:h + 1])
        decay_l.append(decay)
        qb_l.append(qb)
        kb_l.append(kb)
        a_l.append(jnp.where(strict, beta * decay * kk, 0.0))
    yield

    t_l = [eye - jnp.where(ri // 2 == ci // 2, a, 0.0) for a in a_l]
    s = 2
    while s < tseq:
        off_mask = (ri // (2 * s) == ci // (2 * s)) & (ri % (2 * s) >= s) & (ci % (2 * s) < s)
        t_off = [_mm(t, jnp.where(off_mask, a, 0.0)) for t, a in zip(t_l, a_l)]
        t_l = [t - _mm(to, t) for t, to in zip(t_l, t_off)]
        s *= 2
        yield

    eg_l = [jnp.exp(g) for g in gcol_l]
    sol_l = [_mm(t, jnp.concatenate([b * v, (b * e) * k], axis=1))
             for t, b, v, e, k in zip(t_l, beta_l, v_l, eg_l, k_l)]
    qk_l = [(lax.dot_general(qb, kb, _NT, preferred_element_type=F32) * d).astype(BF16)
            for qb, kb, d in zip(qb_l, kb_l, decay_l)]
    qg_l = [(q * e).astype(BF16) for q, e in zip(q_l, eg_l)]
    kd_l = [(k * jnp.exp(gt - g)).astype(BF16) for k, gt, g in zip(k_l, gtot_l, gcol_l)]
    yield

    w_l, o_l, s_old = [], [], []
    for i, (sub, h) in enumerate(pairs):
        u_c, wkb = sol_l[i][:, :HEAD_DIM], sol_l[i][:, HEAD_DIM:].astype(BF16)
        w_parts, o_parts, s_mats = [], [], []
        for sq in range(nseq_chunk):
            rq = slice(sq * tseq, (sq + 1) * tseq)
            s_mat = state[h] if long_seq else s0_ref[sub * nseq_chunk + sq, h]
            sb = s_mat.astype(BF16)
            s_mats.append(s_mat)
            w_parts.append(u_c[rq] - jnp.dot(wkb[rq], sb, preferred_element_type=F32))
            o_parts.append(jnp.dot(qg_l[i][rq], sb, preferred_element_type=F32))
        w_l.append((jnp.concatenate(w_parts, axis=0) if nseq_chunk > 1 else w_parts[0]).astype(BF16))
        o_l.append(jnp.concatenate(o_parts, axis=0) if nseq_chunk > 1 else o_parts[0])
        s_old.append(s_mats)
    yield

    for i, (sub, h) in enumerate(pairs):
        w = w_l[i]
        for sq in range(nseq_chunk):
            rq = slice(sq * tseq, (sq + 1) * tseq)
            g_last = jnp.exp(gtot_l[i][sq * tseq:sq * tseq + 1])
            s_upd = g_last * s_old[i][sq] + lax.dot_general(kd_l[i][rq], w[rq], _TN, preferred_element_type=F32)
            if long_seq:
                state[h] = s_upd
            else:
                s_ref[sub * nseq_chunk + sq, h] = s_upd
        o = o_l[i] + jnp.dot(qk_l[i], w, preferred_element_type=F32)
        o = o * lax.rsqrt(jnp.mean(o * o, axis=-1, keepdims=True) + EPS) * nw_ref[...]
        rs = slice(sub * DN_CHUNK, (sub + 1) * DN_CHUNK)
        hs = slice(h * HEAD_DIM, (h + 1) * HEAD_DIM)
        z_h = z_all[rs, hs]
        o_ref[rs, hs] = (o * (z_h * jax.nn.sigmoid(z_h))).astype(o_ref.dtype)


def _delta_call(proj3, col0, ba, prev, s0, conv_w, alog, dtb, norm_w, l, *, long_seq, tseq, n_outer, n_inner,
                group0, state_layer, n_sub, s_stack=None):
    c = conv_w.shape[-1] // 3
    heads = c // HEAD_DIM
    blk_rows = n_sub * DN_CHUNK
    g_n = blk_rows // SUBLANES
    nseq_blk = 1 if long_seq else blk_rows // tseq
    nseq = n_outer * nseq_blk
    gb0 = group0 // g_n

    def x_spec(col):
        return pl.BlockSpec((g_n, SUBLANES, c), lambda i, t: (gb0 + i * n_inner + t, 0, col))

    if state_layer is None:
        p_spec = pl.BlockSpec((nseq_blk, SUBLANES, 3 * c), lambda i, t: (i, 0, 0))
        s_spec = pl.BlockSpec((nseq_blk, heads, HEAD_DIM, HEAD_DIM), lambda i, t: (i, 0, 0, 0))
    else:
        p_spec = pl.BlockSpec((None, prev.shape[1], nseq_blk, 3 * c), lambda i, t: (state_layer, 0, i, 0))
        s_spec = pl.BlockSpec((None, nseq_blk, heads, HEAD_DIM, HEAD_DIM), lambda i, t: (state_layer, i, 0, 0, 0))
    in_specs = [x_spec(col0), x_spec(col0 + 1), x_spec(col0 + 2), x_spec(col0 + 3),
                pl.BlockSpec((blk_rows, 2 * HEAD_DIM), lambda i, t: (gb0 + i * n_inner + t, 0)),
                p_spec, s_spec,
                _layer(l, (CONV_WIDTH, 3 * c)), _layer(l, (1, HEAD_DIM)), _layer(l, (1, HEAD_DIM)),
                _layer(l, (1, HEAD_DIM))]
    args = [proj3, proj3, proj3, proj3, ba, prev, s0, conv_w, alog, dtb, norm_w]
    aliases = {}
    s_block = (nseq_blk, heads, HEAD_DIM, HEAD_DIM)
    if s_stack is None:
        s_out_spec = pl.BlockSpec(s_block, lambda i, t: (i, 0, 0, 0))
        s_out_shape = jax.ShapeDtypeStruct((nseq,) + s_block[1:], F32)
    else:
        depth, stacked = s_stack
        s_out_spec = pl.BlockSpec((None,) + s_block, lambda i, t: (l, i, 0, 0, 0))
        s_out_shape = jax.ShapeDtypeStruct((depth, nseq) + s_block[1:], F32)
        if stacked is not None:
            in_specs.append(pl.BlockSpec(memory_space=pl.ANY))
            args.append(stacked)
            aliases[len(args) - 1] = 1
    return pl.pallas_call(
        functools.partial(_delta_body, long_seq=long_seq, tseq=tseq, n_aliased=len(aliases)),
        grid=(n_outer, n_inner),
        in_specs=in_specs,
        out_specs=[pl.BlockSpec((blk_rows, c), lambda i, t: (i * n_inner + t, 0)),
                   s_out_spec,
                   pl.BlockSpec((nseq_blk, SUBLANES, 3 * c), lambda i, t: (i, 0, 0))],
        out_shape=[jax.ShapeDtypeStruct((n_outer * n_inner * blk_rows, c), BF16),
                   s_out_shape,
                   jax.ShapeDtypeStruct((nseq, SUBLANES, 3 * c), F32)],
        scratch_shapes=[pltpu.VMEM((SUBLANES, 3 * c), F32)],
        input_output_aliases=aliases,
        compiler_params=_params("parallel", "arbitrary"),
        name="delta_long" if long_seq else "delta_short",
    )(*args)


def _cast_rows(w, steps):
    rows = w.shape[1]
    nblk = 1
    while nblk * 2 <= steps and rows % (nblk * 2) == 0 and rows // (nblk * 2) >= 2 * SUBLANES:
        nblk *= 2
    return rows // nblk


def _block_diag_tiles(w):
    depth, nb, b, _ = w.shape
    per = GATE_TILE // b
    w5 = w.reshape(depth, nb // per, per, b, b)
    eye = jnp.eye(per, dtype=w.dtype)
    t = jnp.einsum('lijcd,jk->lijckd', w5, eye)
    return t.reshape(depth, nb // per, GATE_TILE, GATE_TILE).astype(BF16)


def kernel(x_prompt, x_sample, state_lru_h, state_lru_conv, state_dn_s, state_dn_conv, state_pool, norm_mix, w_in, lru_conv_w, lru_conv_b, lru_gate_a_w, lru_gate_a_b, lru_gate_x_w, lru_gate_x_b, lru_lambda, dn_conv_w, dn_a_log, dn_dt_bias, dn_norm_w, pool_w, pool_scale, w_br_lru, w_br_dn, w_br_pool, w_out, norm_mlp, w_up, w_down, norm_final):
    bp, tp, d = x_prompt.shape
    bs, ts, _ = x_sample.shape
    depth = w_in.shape[0]
    half = d // 2
    heads = half // HEAD_DIM
    mp, ms = bp * tp, bs * ts
    m = mp + ms
    assert ts == SUBLANES and tp % DN_CHUNK == 0 and bs % (DN_CHUNK // ts) == 0
    assert lru_gate_a_w.shape[-1] == LRU_BLOCK and half % GATE_TILE == 0
    assert state_pool.shape[-2] == POOL_BUF

    n_pre = 6 * half
    ba0 = n_pre
    pool0 = ba0 + 2 * heads
    gates0 = pool0 + half
    w_in_t = jnp.swapaxes(w_in, 1, 2)
    n_gate_tiles, n_stream_tiles = 3 * d // half, n_pre // half + 1
    assert gates0 % SUBLANES == 0 and pool0 % SUBLANES == 0

    def gate_row0(j):
        return pl.multiple_of(gates0 + j * half, SUBLANES)

    def stream_row0(j):
        return pl.multiple_of(jnp.where(j < n_stream_tiles - 1, j * half, pool0), SUBLANES)

    row_fill = ((0, 0), (0, HEAD_DIM - heads), (0, 0))
    w_ba = jnp.concatenate([jnp.pad(w_in_t[:, ba0:ba0 + heads], row_fill),
                            jnp.pad(w_in_t[:, ba0 + heads:ba0 + 2 * heads], row_fill)], axis=1)
    col0 = 0

    def row(a):
        return a.reshape(depth, 1, a.shape[-1])

    def lane_pad(a):
        return jnp.pad(a, ((0, 0), (0, HEAD_DIM - a.shape[-1]))).reshape(depth, 1, HEAD_DIM)

    lru_w = (lru_conv_w, row(lru_conv_b), _block_diag_tiles(lru_gate_a_w), row(lru_gate_a_b),
             _block_diag_tiles(lru_gate_x_w), row(lru_gate_x_b), row(lru_lambda))
    alog, dtb, dn_nw = lane_pad(dn_a_log), lane_pad(dn_dt_bias), row(dn_norm_w)
    pool_wb, pool_sc = pool_w.astype(BF16), row(pool_scale)
    g_mix, g_mlp = row(norm_mix), row(norm_mlp)

    s_lru_h0 = state_lru_h.reshape(depth, bs, 1, half)
    s_lru_prev = jnp.swapaxes(state_lru_conv, 1, 2)
    s_dn_prev = jnp.swapaxes(state_dn_conv, 1, 2)
    s_pool_hist = jnp.swapaxes(state_pool, 1, 2)
    p_lru_h0 = jnp.zeros((bp, 1, half), F32)
    p_lru_prev = jnp.zeros((bp, SUBLANES, half), F32)
    p_dn_prev = jnp.zeros((bp, SUBLANES, 3 * half), F32)
    p_dn_s0 = jnp.zeros((bp, heads, HEAD_DIM, HEAD_DIM), F32)
    p_pool_hist = jnp.zeros((bp, 2 * SUBLANES, half), F32)

    tm = 512 if (mp % 512 == 0 and ms % 512 == 0) else 256
    tm_big = next(t for t in (1536, 1024, tm) if m % t == 0)
    lru_g = 64
    assert (tp // SUBLANES) % lru_g == 0 and bs % lru_g == 0 and m % tm == 0 and mp % tm == 0
    n_t = tp // SUBLANES // lru_g
    long_kw = dict(long_seq=True, n_outer=bp, n_inner=n_t, g_n=lru_g, group0=0, state_layer=None)
    short_kw = dict(long_seq=False, n_outer=bs // lru_g, n_inner=1, g_n=lru_g, group0=mp // SUBLANES)
    seq_per_blk = DN_CHUNK // ts
    dn_sub, dn_sub_s = 8, 2
    assert tp % (dn_sub * DN_CHUNK) == 0 and bs % (dn_sub_s * seq_per_blk) == 0

    p_states, s_states = [], []
    s_dn_stack = None
    norm_fin = norm_final.reshape(1, 1, d)
    x = (x_prompt.reshape(mp, d), x_sample.reshape(ms, d))
    xn, ba = _norm_cast(*x, g_mix, w_ba, 0, tm=tm)
    for l in range(depth):
        w_rows = (None, pl.Element(half), pl.Element(d))
        m_tiles = m // tm_big
        proj, w_br_dn_b, w_br_pool_b = _stream_matmul(
            xn, w_in_t, pl.BlockSpec(w_rows, lambda j, i, l=l: (l, stream_row0(j), 0)), (half, d), n_stream_tiles,
            tm=tm_big, tn=half, out_dtype=F32, w_rows_are_outputs=True,
            casts=[(w, l, _cast_rows(w, n_stream_tiles * m_tiles)) for w in (w_br_dn, w_br_pool)])
        proj3 = proj.reshape(m // SUBLANES, SUBLANES, proj.shape[-1])

        gates, w_out_b, w_br_lru_b = _stream_matmul(
            xn, w_in_t, pl.BlockSpec(w_rows, lambda j, i, l=l: (l, gate_row0(j), 0)), (half, d), n_gate_tiles,
            tm=tm_big, tn=half, out_dtype=BF16, w_rows_are_outputs=True, act="sigmoid",
            casts=[(w, l, _cast_rows(w, n_gate_tiles * m_tiles)) for w in (w_out, w_br_lru)])
        o_lru_p, p_h, p_cb = _lru_call(proj3, col0, p_lru_prev, p_lru_h0, lru_w, l, **long_kw)
        o_lru_s, s_h, s_cb = _lru_call(proj3, col0, s_lru_prev, s_lru_h0, lru_w, l, state_layer=l, **short_kw)

        o_dn_p, p_s, p_db = _delta_call(proj3, col0 + 2, ba, p_dn_prev, p_dn_s0, dn_conv_w, alog, dtb, dn_nw, l,
                                        long_seq=True, tseq=DN_CHUNK, n_outer=bp, n_inner=tp // (dn_sub * DN_CHUNK),
                                        group0=0, state_layer=None, n_sub=dn_sub)
        o_dn_s, s_dn_stack, s_db = _delta_call(proj3, col0 + 2, ba, s_dn_prev, state_dn_s, dn_conv_w, alog, dtb,
                                               dn_nw, l, long_seq=False, tseq=ts,
                                               n_outer=bs // (dn_sub_s * seq_per_blk), n_inner=1,
                                               group0=mp // SUBLANES, state_layer=l, n_sub=dn_sub_s,
                                               s_stack=(depth, s_dn_stack))

        o_pool_p, p_ph = _pool_call(proj3, col0 + 6, p_pool_hist, pool_wb, pool_sc, l, past_len=0, **long_kw)
        o_pool_s, s_ph = _pool_call(proj3, col0 + 6, s_pool_hist, pool_wb, pool_sc, l, past_len=PAST_LEN,
                                    state_layer=l, **short_kw)

        merged = _merge((o_lru_p, o_lru_s), (o_dn_p, o_dn_s), (o_pool_p, o_pool_s), gates,
                        w_br_lru_b, w_br_dn_b, w_br_pool_b, 0, tm=tm, tn=d)
        x, xn = _matmul_res(merged, w_out_b, 0, x, tm=tm, tk=d, gain=g_mlp, gain_layer=l)
        w_up_spec = pl.BlockSpec((None, d, half), lambda j, i, l=l: (l, 0, j))
        n_up_tiles = w_up.shape[-1] // half
        hm, w_down_b = _stream_matmul(xn, w_up, w_up_spec, (d, half), n_up_tiles, tm=tm_big, tn=half,
                                      out_dtype=BF16, w_rows_are_outputs=False, act="relu2",
                                      casts=[(w_down, l, _cast_rows(w_down, n_up_tiles * m_tiles))])
        if l + 1 < depth:
            x, xn, ba = _matmul_res(hm, w_down_b, 0, x, tm=tm, tk=d, gain=g_mix, gain_layer=l + 1, w_small=w_ba)
        else:
            _, y_prompt, y_sample = _matmul_res(hm, w_down_b, 0, x, tm=tm, tk=half, gain=norm_fin, gain_layer=0,
                                                split_rows=mp)
        p_states.append((p_h, p_cb, p_s, p_db, p_ph))
        s_states.append((s_h, s_cb, None, s_db, s_ph))

    y_prompt = y_prompt.reshape(bp, tp, d)
    y_sample = y_sample.reshape(bs, ts, d)

    def collect(states, nseq, sm=None):
        tail = CONV_WIDTH - 1
        h = jnp.stack([s[0] for s in states]).reshape(depth, nseq, half)
        cb = jnp.stack([s[1] for s in states])[:, :, SUBLANES - tail:]
        if sm is None:
            sm = jnp.stack([s[2] for s in states])
        db = jnp.stack([s[3] for s in states])[:, :, SUBLANES - tail:]
        ph = jnp.stack([s[4] for s in states])[:, :, 2 * SUBLANES - POOL_BUF:]
        return h, cb, sm, db, ph

    return (y_prompt, y_sample) + collect(p_states, bp) + collect(s_states, bs, s_dn_stack)
```
